```python
import math
import jax
import jax.numpy as jnp
from jax import lax
import numpy as np

D_MODEL = 1024
BATCH = 8
SEQ = 4096
DEPTH = 4
DEC_BATCH = 16
DEC_SEQ = 32
PAST_LEN = 4096

CHUNK = 64
Q_BLOCK = 128
D_SSM = 2 * D_MODEL
SSM_HEAD_DIM = 64
N_SSM_HEADS = D_SSM // SSM_HEAD_DIM
SSM_GROUPS = 4
D_STATE = 128
CONV_W = 4
CONV_DIM = D_SSM + 2 * SSM_GROUPS * D_STATE
N_MLA_HEADS = 16
D_NOPE = 64
D_ROPE = 32
D_V = 64
Q_LORA = D_MODEL // 2
KV_LORA = D_MODEL // 4
D_MLA = N_MLA_HEADS * D_V
MLA_SCALE = (D_NOPE + D_ROPE) ** -0.5
ROPE_THETA = 10000.0
N_MEM = 256
MEM_HEADS = 4
MEM_HEAD_DIM = D_MODEL // MEM_HEADS
D_MEM = MEM_HEADS * MEM_HEAD_DIM
N_BRANCH = 3
SPLITS = (D_SSM, CONV_DIM, N_SSM_HEADS, Q_LORA, KV_LORA, D_ROPE, D_MLA, D_MEM, N_BRANCH * D_MODEL)
D_IN = sum(SPLITS)
SPLIT_IDX = tuple(int(v) for v in np.cumsum(SPLITS)[:-1])
ALPHA = (2 * DEPTH) ** 0.25
BETA = (8 * DEPTH) ** -0.25
EPS = 1e-5
NEG_INF = -1e30

kernel_name = 'ssd_mla_memory_streaming_encoder'


def layer_norm(x, g, b):
    xf = x.astype(jnp.float32)
    mu = jnp.mean(xf, -1, keepdims=True)
    var = jnp.mean(jnp.square(xf - mu), -1, keepdims=True)
    return ((xf - mu) * lax.rsqrt(var + EPS) * g.astype(jnp.float32) + b.astype(jnp.float32)).astype(x.dtype)


def rms_norm(x, w):
    xf = x.astype(jnp.float32)
    return (xf * lax.rsqrt(jnp.mean(jnp.square(xf), -1, keepdims=True) + EPS) * w.astype(jnp.float32)).astype(x.dtype)


def gated_group_rmsnorm(y, z, w):
    g = y.astype(jnp.float32) * jax.nn.silu(z.astype(jnp.float32))
    gs = g.reshape(g.shape[:-1] + (SSM_GROUPS, D_SSM // SSM_GROUPS))
    gs = gs * lax.rsqrt(jnp.mean(jnp.square(gs), -1, keepdims=True) + EPS)
    return gs.reshape(g.shape) * w.astype(jnp.float32)


def rope(x, pos):
    half = x.shape[-1] // 2
    inv = ROPE_THETA ** (-jnp.arange(half, dtype=jnp.float32) / half)
    ang = pos.astype(jnp.float32)[:, None] * inv[None, :]
    cos = jnp.cos(ang)[:, None, :]
    sin = jnp.sin(ang)[:, None, :]
    xf = x.astype(jnp.float32)
    x1, x2 = xf[..., :half], xf[..., half:]
    return jnp.concatenate([x1 * cos - x2 * sin, x2 * cos + x1 * sin], -1).astype(x.dtype)


def causal_conv(xbc, prev, w, bias):
    l = xbc.shape[1]
    xp = jnp.concatenate([prev.astype(xbc.dtype), xbc], axis=1)
    y = bias
    for k in range(CONV_W):
        y = y + xp[:, k:k + l] * w[k]
    return jax.nn.silu(y), xp[:, xp.shape[1] - (CONV_W - 1):]


def segsum(x):
    t = x.shape[-1]
    xc = jnp.cumsum(x, -1)
    s = xc[..., :, None] - xc[..., None, :]
    return jnp.where(jnp.tril(jnp.ones((t, t), dtype=bool)), s, -jnp.inf)


def ssd_scan(xh, dt, a, bm, cm, h0):
    b, l = xh.shape[:2]
    t = min(CHUNK, l)
    nc = l // t
    r = N_SSM_HEADS // SSM_GROUPS
    xd = (xh.astype(jnp.float32) * dt[..., None]).reshape(b, nc, t, SSM_GROUPS, r, SSM_HEAD_DIM)
    da = (dt * a).reshape(b, nc, t, SSM_GROUPS, r).transpose(0, 3, 4, 1, 2)
    bc = bm.astype(jnp.float32).reshape(b, nc, t, SSM_GROUPS, D_STATE)
    cc = cm.astype(jnp.float32).reshape(b, nc, t, SSM_GROUPS, D_STATE)
    a_cs = jnp.cumsum(da, -1)
    lmat = jnp.exp(segsum(da))
    y_diag = jnp.einsum('bclgn,bcsgn,bgrcls,bcsgrp->bclgrp', cc, bc, lmat, xd)
    decay_states = jnp.exp(a_cs[..., -1:] - a_cs)
    states = jnp.einsum('bclgn,bgrcl,bclgrp->bcgrpn', bc, decay_states, xd)
    h0 = h0.astype(jnp.float32).reshape(b, SSM_GROUPS, r, SSM_HEAD_DIM, D_STATE)
    states = jnp.concatenate([h0[:, None], states], axis=1)
    a_last = jnp.pad(a_cs[..., -1], ((0, 0), (0, 0), (0, 0), (1, 0)))
    chunk_decay = jnp.exp(segsum(a_last))
    new_states = jnp.einsum('bgrzc,bcgrpn->bzgrpn', chunk_decay, states)
    states, final = new_states[:, :-1], new_states[:, -1]
    y_off = jnp.einsum('bclgn,bcgrpn,bgrcl->bclgrp', cc, states, jnp.exp(a_cs))
    y = (y_diag + y_off).reshape(b, l, N_SSM_HEADS, SSM_HEAD_DIM)
    return y, final.reshape(b, N_SSM_HEADS, SSM_HEAD_DIM, D_STATE)


def mla_block_attend(q_nope, q_rope, k_nope, k_rope, v, q_pos, k_pos):
    s = (jnp.einsum('bqhd,bkhd->bhqk', q_nope, k_nope)
         + jnp.einsum('bqhr,bkr->bhqk', q_rope, k_rope)).astype(jnp.float32) * MLA_SCALE
    visible = k_pos[None, :] < ((q_pos // CHUNK + 1) * CHUNK)[:, None]
    s = jnp.where(visible[None, None], s, NEG_INF)
    p = jax.nn.softmax(s, axis=-1).astype(v.dtype)
    return jnp.einsum('bhqk,bkhd->bqhd', p, v)


def mla_attention(q_nope, q_rope, k_nope, k_rope, v, q_pos, k_pos):
    b, l = q_nope.shape[:2]
    if l % Q_BLOCK != 0:
        return mla_block_attend(q_nope, q_rope, k_nope, k_rope, v, q_pos, k_pos)
    nb = l // Q_BLOCK
    qn = q_nope.reshape(b, nb, Q_BLOCK, N_MLA_HEADS, D_NOPE).swapaxes(0, 1)
    qr = q_rope.reshape(b, nb, Q_BLOCK, N_MLA_HEADS, D_ROPE).swapaxes(0, 1)
    qp = q_pos.reshape(nb, Q_BLOCK)
    out = lax.map(lambda a: mla_block_attend(a[0], a[1], k_nope, k_rope, v, a[2], k_pos), (qn, qr, qp))
    return out.swapaxes(0, 1).reshape(b, l, N_MLA_HEADS, D_V)


def memory_attention(q, mk, mv):
    s = jnp.einsum('blhd,bmhd->bhlm', q, mk).astype(jnp.float32) * (MEM_HEAD_DIM ** -0.5)
    p = jax.nn.softmax(s, axis=-1).astype(mv.dtype)
    return jnp.einsum('bhlm,bmhd->blhd', p, mv)


def trunk_layer(x, pos, conv_prev, ssm_prev, ckv_prev, krope_prev, mem_k, mem_v,
                w_in, conv_w, conv_b, dt_bias, a_log, d_skip, ssm_norm_w, w_o_ssm,
                q_a_norm_w, w_q_b, kv_norm_w, w_uk, w_uv, w_o_mla, w_o_mem, w_out, ln_g, ln_b):
    b, l, _ = x.shape
    proj = jnp.einsum('bld,de->ble', x, w_in)
    z, xbc, dt_raw, q_a, kv_a, k_r, g_mla, q_mem, gates = jnp.split(proj, SPLIT_IDX, axis=-1)

    xbc, conv_new = causal_conv(xbc, conv_prev, conv_w, conv_b)
    xs, bm, cm = jnp.split(xbc, [D_SSM, D_SSM + SSM_GROUPS * D_STATE], axis=-1)
    xs = xs.reshape(b, l, N_SSM_HEADS, SSM_HEAD_DIM)
    dt = jax.nn.softplus(dt_raw.astype(jnp.float32) + dt_bias.astype(jnp.float32))
    a = -jnp.exp(a_log.astype(jnp.float32))
    y, ssm_new = ssd_scan(xs, dt, a, bm.reshape(b, l, SSM_GROUPS, D_STATE),
                          cm.reshape(b, l, SSM_GROUPS, D_STATE), ssm_prev)
    y = y + d_skip.astype(jnp.float32)[:, None] * xs.astype(jnp.float32)
    y = gated_group_rmsnorm(y.reshape(b, l, D_SSM), z, ssm_norm_w).astype(x.dtype)
    y_ssm = jnp.einsum('ble,ed->bld', y, w_o_ssm)

    q = jnp.einsum('blr,re->ble', rms_norm(q_a, q_a_norm_w), w_q_b).reshape(b, l, N_MLA_HEADS, D_NOPE + D_ROPE)
    q_nope = q[..., :D_NOPE]
    q_rope = rope(q[..., D_NOPE:], pos)
    ckv = rms_norm(kv_a, kv_norm_w)
    krope = rope(k_r[:, :, None, :], pos)[:, :, 0]
    if ckv_prev is None:
        ckv_all, krope_all, k_pos = ckv, krope, pos
    else:
        ckv_all = jnp.concatenate([ckv_prev.astype(ckv.dtype), ckv], axis=1)
        krope_all = jnp.concatenate([krope_prev.astype(krope.dtype), krope], axis=1)
        k_pos = jnp.arange(ckv_all.shape[1], dtype=jnp.int32)
    k_nope = jnp.einsum('bkc,chd->bkhd', ckv_all, w_uk)
    v = jnp.einsum('bkc,chd->bkhd', ckv_all, w_uv)
    o = mla_attention(q_nope, q_rope, k_nope, krope_all, v, pos, k_pos).reshape(b, l, D_MLA)
    y_mla = jnp.einsum('ble,ed->bld', o * jax.nn.silu(g_mla), w_o_mla)

    om = memory_attention(q_mem.reshape(b, l, MEM_HEADS, MEM_HEAD_DIM), mem_k.astype(x.dtype), mem_v.astype(x.dtype))
    y_mem = jnp.einsum('ble,ed->bld', om.reshape(b, l, D_MEM), w_o_mem)

    g = jax.nn.sigmoid(gates).reshape(b, l, N_BRANCH, D_MODEL)
    h = g[:, :, 0] * y_ssm + g[:, :, 1] * y_mla + g[:, :, 2] * y_mem
    out = jnp.einsum('bld,de->ble', h, w_out)
    x_new = layer_norm(ALPHA * x + out, ln_g, ln_b)
    return x_new, conv_new, ssm_new, ckv, krope


def setup_inputs(seed: int = 0) -> dict:
    key = jax.random.key(seed)
    ks = jax.random.split(key, 32)
    f32 = jnp.float32

    def nrm(k, shape, scale):
        return jax.random.normal(k, shape, f32) * scale

    dt0 = jnp.exp(jax.random.uniform(ks[14], (DEPTH, N_SSM_HEADS), f32, minval=math.log(1e-3), maxval=math.log(1e-1)))
    return {
        'x_prompt': nrm(ks[0], (BATCH, SEQ, D_MODEL), 1.0),
        'x_sample': nrm(ks[1], (DEC_BATCH, DEC_SEQ, D_MODEL), 1.0),
        'mem_prompt': nrm(ks[2], (BATCH, N_MEM, D_MODEL), 1.0),
        'state_ssm': nrm(ks[3], (DEPTH, DEC_BATCH, N_SSM_HEADS, SSM_HEAD_DIM, D_STATE), 0.1),
        'state_conv': nrm(ks[4], (DEPTH, DEC_BATCH, CONV_W - 1, CONV_DIM), 1.0),
        'cache_ckv': nrm(ks[5], (DEPTH, DEC_BATCH, PAST_LEN, KV_LORA), 1.0),
        'cache_krope': nrm(ks[6], (DEPTH, DEC_BATCH, PAST_LEN, D_ROPE), 1.0),
        'cache_mem_k': nrm(ks[7], (DEPTH, DEC_BATCH, N_MEM, MEM_HEADS, MEM_HEAD_DIM), 1.0),
        'cache_mem_v': nrm(ks[8], (DEPTH, DEC_BATCH, N_MEM, MEM_HEADS, MEM_HEAD_DIM), 1.0),
        'ln_in_g': 1.0 + nrm(ks[9], (D_MODEL,), 0.02),
        'ln_in_b': nrm(ks[10], (D_MODEL,), 0.02),
        'w_in': nrm(ks[11], (DEPTH, D_MODEL, D_IN), D_MODEL ** -0.5),
        'conv_w': nrm(ks[12], (DEPTH, CONV_W, CONV_DIM), CONV_W ** -0.5),
        'conv_b': nrm(ks[13], (DEPTH, CONV_DIM), 0.02),
        'dt_bias': dt0 + jnp.log(-jnp.expm1(-dt0)),
        'a_log': jnp.log(jax.random.uniform(ks[15], (DEPTH, N_SSM_HEADS), f32, minval=1.0, maxval=16.0)),
        'd_skip': 1.0 + nrm(ks[16], (DEPTH, N_SSM_HEADS), 0.1),
        'ssm_norm_w': 1.0 + nrm(ks[17], (DEPTH, D_SSM), 0.02),
        'w_o_ssm': nrm(ks[18], (DEPTH, D_SSM, D_MODEL), BETA * D_SSM ** -0.5),
        'q_a_norm_w': 1.0 + nrm(ks[19], (DEPTH, Q_LORA), 0.02),
        'w_q_b': nrm(ks[20], (DEPTH, Q_LORA, N_MLA_HEADS * (D_NOPE + D_ROPE)), Q_LORA ** -0.5),
        'kv_norm_w': 1.0 + nrm(ks[21], (DEPTH, KV_LORA), 0.02),
        'w_uk': nrm(ks[22], (DEPTH, KV_LORA, N_MLA_HEADS, D_NOPE), KV_LORA ** -0.5),
        'w_uv': nrm(ks[23], (DEPTH, KV_LORA, N_MLA_HEADS, D_V), KV_LORA ** -0.5),
        'w_o_mla': nrm(ks[24], (DEPTH, D_MLA, D_MODEL), BETA * D_MLA ** -0.5),
        'w_mem_k': nrm(ks[25], (DEPTH, D_MODEL, D_MEM), D_MODEL ** -0.5),
        'w_mem_v': nrm(ks[26], (DEPTH, D_MODEL, D_MEM), D_MODEL ** -0.5),
        'w_o_mem': nrm(ks[27], (DEPTH, D_MEM, D_MODEL), BETA * D_MEM ** -0.5),
        'w_out': nrm(ks[28], (DEPTH, D_MODEL, D_MODEL), BETA * D_MODEL ** -0.5),
        'ln_g': 1.0 + nrm(ks[29], (DEPTH, D_MODEL), 0.02),
        'ln_b': nrm(ks[30], (DEPTH, D_MODEL), 0.02),
    }


def reference(x_prompt, x_sample, mem_prompt, state_ssm, state_conv, cache_ckv, cache_krope,
              cache_mem_k, cache_mem_v, ln_in_g, ln_in_b, w_in, conv_w, conv_b, dt_bias, a_log,
              d_skip, ssm_norm_w, w_o_ssm, q_a_norm_w, w_q_b, kv_norm_w, w_uk, w_uv, w_o_mla,
              w_mem_k, w_mem_v, w_o_mem, w_out, ln_g, ln_b):
    b_p, l_p = x_prompt.shape[:2]
    l_s = x_sample.shape[1]
    past = cache_ckv.shape[2]
    pos_p = jnp.arange(l_p, dtype=jnp.int32)
    pos_s = past + jnp.arange(l_s, dtype=jnp.int32)
    xp = layer_norm(x_prompt, ln_in_g, ln_in_b)
    xs = layer_norm(x_sample, ln_in_g, ln_in_b)
    conv0 = jnp.zeros((b_p, CONV_W - 1, CONV_DIM), x_prompt.dtype)
    ssm0 = jnp.zeros((b_p, N_SSM_HEADS, SSM_HEAD_DIM, D_STATE), jnp.float32)
    p_ssm, p_conv, p_ckv, p_krope, p_mk, p_mv = [], [], [], [], [], []
    s_ssm, s_conv, s_ckv, s_krope = [], [], [], []
    for i in range(DEPTH):
        lw = (w_in[i], conv_w[i], conv_b[i], dt_bias[i], a_log[i], d_skip[i], ssm_norm_w[i], w_o_ssm[i],
              q_a_norm_w[i], w_q_b[i], kv_norm_w[i], w_uk[i], w_uv[i], w_o_mla[i], w_o_mem[i], w_out[i],
              ln_g[i], ln_b[i])
        mk = jnp.einsum('bmd,de->bme', mem_prompt, w_mem_k[i]).reshape(b_p, N_MEM, MEM_HEADS, MEM_HEAD_DIM)
        mv = jnp.einsum('bmd,de->bme', mem_prompt, w_mem_v[i]).reshape(b_p, N_MEM, MEM_HEADS, MEM_HEAD_DIM)
        xp, c_p, st_p, ckv_p, kr_p = trunk_layer(xp, pos_p, conv0, ssm0, None, None, mk, mv, *lw)
        xs, c_s, st_s, ckv_s, kr_s = trunk_layer(xs, pos_s, state_conv[i], state_ssm[i], cache_ckv[i],
                                                 cache_krope[i], cache_mem_k[i], cache_mem_v[i], *lw)
        p_ssm.append(st_p)
        p_conv.append(c_p)
        p_ckv.append(ckv_p)
        p_krope.append(kr_p)
        p_mk.append(mk)
        p_mv.append(mv)
        s_ssm.append(st_s)
        s_conv.append(c_s)
        s_ckv.append(ckv_s)
        s_krope.append(kr_s)
    return (xp, xs, jnp.stack(p_ssm), jnp.stack(p_conv), jnp.stack(p_ckv), jnp.stack(p_krope),
            jnp.stack(p_mk), jnp.stack(p_mv), jnp.stack(s_ssm), jnp.stack(s_conv), jnp.stack(s_ckv),
            jnp.stack(s_krope))
```

```python
import functools
import math

import numpy as np
import jax
import jax.numpy as jnp
from jax import lax
from jax.experimental import pallas as pl
from jax.experimental.pallas import tpu as pltpu

F32 = jnp.float32
BF16 = jnp.bfloat16

D_MODEL = 1024
DEPTH = 4
CHUNK = 64
D_SSM = 2048
SSM_HEAD_DIM = 64
N_SSM_HEADS = 32
SSM_GROUPS = 4
HEADS_PER_GROUP = N_SSM_HEADS // SSM_GROUPS
D_STATE = 128
CONV_W = 4
CONV_DIM = D_SSM + 2 * SSM_GROUPS * D_STATE
N_MLA_HEADS = 16
D_NOPE = 64
D_ROPE = 32
D_V = 64
Q_LORA = 512
KV_LORA = 256
D_MLA = N_MLA_HEADS * D_V
MLA_SCALE = (D_NOPE + D_ROPE) ** -0.5
ROPE_THETA = 10000.0
N_MEM = 256
MEM_HEADS = 4
MEM_HEAD_DIM = 256
D_MEM = MEM_HEADS * MEM_HEAD_DIM
ALPHA = (2 * DEPTH) ** 0.25
EPS = 1e-5
NEG_INF = -1e30

LANES = 128
HEAD_PAD = 128
VMEM_LIMIT = 56 * 1024 * 1024

COL_XBC = 0
COL_GATES = 3072
COL_Z = 6144
COL_GMLA = 8192
COL_QMEM = 9216
COL_QA = 10240
COL_KVA = 10752
COL_MISC_A = 11008
COL_MISC_B = 11136
N_PROJ = 11264


def _cparams(sem):
    return pltpu.CompilerParams(dimension_semantics=sem, vmem_limit_bytes=VMEM_LIMIT)


def _const_spec(shape):
    nd = len(shape)
    return pl.BlockSpec(shape, lambda *_: (0,) * nd)


def _silu(x):
    return x * jax.nn.sigmoid(x)


def _softplus(x):
    return jnp.maximum(x, 0.0) + jnp.log1p(jnp.exp(-jnp.abs(x)))


def _dot(a, b):
    return jnp.dot(a, b, preferred_element_type=F32)


def _dot_nt(a, b):
    return lax.dot_general(a, b, (((1,), (1,)), ((), ())), preferred_element_type=F32)


def _layer_norm(x, g, b):
    mu = jnp.mean(x, axis=-1, keepdims=True)
    xc = x - mu
    var = jnp.mean(xc * xc, axis=-1, keepdims=True)
    return xc * lax.rsqrt(var + EPS) * g + b


def _rms_norm(x, w):
    return x * lax.rsqrt(jnp.mean(x * x, axis=-1, keepdims=True) + EPS) * w


def _ln_body(x_ref, g_ref, b_ref, o32_ref, o16_ref):
    y = _layer_norm(x_ref[...], g_ref[...], b_ref[...])
    o32_ref[...] = y
    o16_ref[...] = y.astype(BF16)


def _ln_call(x, g, b, tm):
    t, d = x.shape
    return pl.pallas_call(
        _ln_body,
        grid=(t // tm,),
        in_specs=[pl.BlockSpec((tm, d), lambda i: (i, 0)), _const_spec((1, d)), _const_spec((1, d))],
        out_specs=[pl.BlockSpec((tm, d), lambda i: (i, 0)), pl.BlockSpec((tm, d), lambda i: (i, 0))],
        out_shape=[jax.ShapeDtypeStruct((t, d), F32), jax.ShapeDtypeStruct((t, d), BF16)],
        compiler_params=_cparams(("parallel",)),
        name="ln_in",
    )(x, g.reshape(1, d), b.reshape(1, d))


def _mm_body(x_ref, w_ref, o_ref):
    o_ref[...] = _dot(x_ref[...], w_ref[...]).astype(o_ref.dtype)


def _mm_call(x, w, tm, tn, name):
    m, k = x.shape
    n = w.shape[1]
    return pl.pallas_call(
        _mm_body,
        grid=(m // tm, n // tn),
        in_specs=[pl.BlockSpec((tm, k), lambda i, j: (i, 0)), pl.BlockSpec((k, tn), lambda i, j: (0, j))],
        out_specs=pl.BlockSpec((tm, tn), lambda i, j: (i, j)),
        out_shape=jax.ShapeDtypeStruct((m, n), F32),
        compiler_params=_cparams(("parallel", "parallel")),
        name=name,
    )(x, w)


def _cumsum_rows(x, t):
    row = lax.broadcasted_iota(jnp.int32, x.shape, 0)
    s = x
    k = 1
    while k < t:
        s = s + jnp.where(row >= k, pltpu.roll(s, k, 0), 0.0)
        k *= 2
    return s


def _expand_heads(w, e_ref):
    hi = w.astype(BF16)
    lo = (w - hi.astype(F32)).astype(BF16)
    e = e_ref[...]
    return _dot(hi, e) + _dot(lo, e)


def _ssm_body(xbc_ref, z_ref, misc_ref, tail0_ref, s0_ref, cw_ref, cb_ref, dtb_ref, alog_ref, dskip_ref,
              nw_ref, e_ref, y_ref, sfin_ref, tail_ref, cbuf, ybuf, dabuf, dtbuf, state, tail, *, rb, tc):
    r = pl.program_id(1)
    nr = pl.num_programs(1)

    @pl.when(r == 0)
    def _():
        state[...] = s0_ref[0]
        tail[...] = tail0_ref[0]

    x = xbc_ref[...]
    cw = cw_ref[...]
    cb = cb_ref[...]
    acc = cb + cw[3:4, :] * x
    for j in range(1, CONV_W):
        acc = acc + cw[3 - j:4 - j, :] * pltpu.roll(x, j, 0)
    cbuf[...] = _silu(acc)
    x8 = x[0:8, :]
    t8 = tail[...]
    row8 = lax.broadcasted_iota(jnp.int32, x8.shape, 0)
    acc8 = cb + cw[3:4, :] * x8
    for j in range(1, CONV_W):
        xj = jnp.where(row8 < j, pltpu.roll(t8, j, 0), pltpu.roll(x8, j, 0))
        acc8 = acc8 + cw[3 - j:4 - j, :] * xj
    cbuf[0:8, :] = _silu(acc8)
    tail[...] = x[rb - 8:rb, :]

    dt = _softplus(misc_ref[...] + dtb_ref[...])
    dtbuf[...] = dt
    dabuf[...] = dt * (-jnp.exp(alog_ref[...]))

    ri = lax.broadcasted_iota(jnp.int32, (tc, tc), 0)
    ci = lax.broadcasted_iota(jnp.int32, (tc, tc), 1)
    lower = ri >= ci
    pad_rows = LANES - tc

    def to_rows(a):
        if pad_rows:
            a = jnp.concatenate([a, jnp.zeros((pad_rows, LANES), F32)], axis=0)
        return a.T[:, :tc]

    def chunk(c, carry):
        r0 = pl.multiple_of(c * tc, tc)
        rows = pl.ds(r0, tc)
        da = dabuf[rows, :]
        dtc = dtbuf[rows, :]
        acs = _cumsum_rows(da, tc)
        acs_t = to_rows(acs)
        dt_t = to_rows(dtc)
        total = acs[tc - 1:tc, :]
        wexp = _expand_heads(jnp.exp(total - acs) * dtc, e_ref)
        etot = _expand_heads(jnp.broadcast_to(jnp.exp(total), (8, LANES)), e_ref)[0:1, :]
        xs = cbuf[rows, 0:D_SSM]
        xs16 = xs.astype(BF16)
        xw16 = (xs * wexp).astype(BF16)
        for g in range(SSM_GROUPS):
            bg = cbuf[rows, D_SSM + g * D_STATE:D_SSM + (g + 1) * D_STATE]
            cg = cbuf[rows, D_SSM + SSM_GROUPS * D_STATE + g * D_STATE:D_SSM + SSM_GROUPS * D_STATE + (g + 1) * D_STATE]
            bg16 = bg.astype(BF16)
            cg16 = cg.astype(BF16)
            cbm = _dot_nt(cg16, bg16)
            gcols = slice(g * HEADS_PER_GROUP * SSM_HEAD_DIM, (g + 1) * HEADS_PER_GROUP * SSM_HEAD_DIM)
            s_old = state[:, gcols]
            s16 = s_old.astype(BF16)
            for hh in range(0, HEADS_PER_GROUP, 2):
                pair = []
                for h in (g * HEADS_PER_GROUP + hh, g * HEADS_PER_GROUP + hh + 1):
                    colb = jnp.broadcast_to(acs[:, h:h + 1], (tc, LANES))
                    rowb = jnp.broadcast_to(acs_t[h:h + 1, :], (tc, tc))
                    dtb = jnp.broadcast_to(dt_t[h:h + 1, :], (tc, tc))
                    lmat = jnp.where(lower, jnp.exp(colb[:, :tc] - rowb), 0.0)
                    m16 = (cbm * lmat * dtb).astype(BF16)
                    cexp16 = (cg * jnp.exp(colb)).astype(BF16)
                    hl = h - g * HEADS_PER_GROUP
                    xh = xs16[:, h * SSM_HEAD_DIM:(h + 1) * SSM_HEAD_DIM]
                    sh = s16[:, hl * SSM_HEAD_DIM:(hl + 1) * SSM_HEAD_DIM]
                    if tc % LANES == 0:
                        yh = _dot(jnp.concatenate([m16, cexp16], axis=1), jnp.concatenate([xh, sh], axis=0))
                    else:
                        yh = _dot(m16, xh) + _dot(cexp16, sh)
                    pair.append(yh)
                h0 = g * HEADS_PER_GROUP + hh
                ybuf[rows, h0 * SSM_HEAD_DIM:(h0 + 2) * SSM_HEAD_DIM] = jnp.concatenate(pair, axis=1)
            bg_t = to_rows(bg).astype(BF16)
            state[:, gcols] = s_old * etot[:, gcols] + _dot(bg_t, xw16[:, gcols])
        y = ybuf[rows, :] + dskip_ref[...] * xs
        zc = z_ref[rows, :]
        gt = y * _silu(zc)
        gw = D_SSM // SSM_GROUPS
        for g in range(SSM_GROUPS):
            gg = gt[:, g * gw:(g + 1) * gw]
            ms = jnp.mean(gg * gg, axis=-1, keepdims=True)
            y_ref[rows, g * gw:(g + 1) * gw] = (gg * lax.rsqrt(ms + EPS) * nw_ref[:, g * gw:(g + 1) * gw]).astype(BF16)
        return carry

    lax.fori_loop(0, rb // tc, chunk, 0)

    @pl.when(r == nr - 1)
    def _():
        sfin_ref[0] = state[...]
        tail_ref[0] = tail[...]


def _ssm_call(proj, tail0, s0, cw, cb, dtb, alog, dskip, nw, emat, *, batch, seq, rb, tc):
    t = batch * seq
    nrb = seq // rb
    row = lambda b, r: b * nrb + r
    body = functools.partial(_ssm_body, rb=rb, tc=tc)
    return pl.pallas_call(
        body,
        grid=(batch, nrb),
        in_specs=[
            pl.BlockSpec((rb, CONV_DIM), lambda b, r: (row(b, r), COL_XBC // CONV_DIM)),
            pl.BlockSpec((rb, D_SSM), lambda b, r: (row(b, r), COL_Z // D_SSM)),
            pl.BlockSpec((rb, LANES), lambda b, r: (row(b, r), COL_MISC_A // LANES)),
            pl.BlockSpec((1, 8, CONV_DIM), lambda b, r: (b, 0, 0)),
            pl.BlockSpec((1, D_STATE, D_SSM), lambda b, r: (b, 0, 0)),
            _const_spec((CONV_W, CONV_DIM)),
            _const_spec((1, CONV_DIM)),
            _const_spec((1, LANES)),
            _const_spec((1, LANES)),
            _const_spec((1, D_SSM)),
            _const_spec((1, D_SSM)),
            _const_spec((LANES, D_SSM)),
        ],
        out_specs=[
            pl.BlockSpec((rb, D_SSM), lambda b, r: (row(b, r), 0)),
            pl.BlockSpec((1, D_STATE, D_SSM), lambda b, r: (b, 0, 0)),
            pl.BlockSpec((1, 8, CONV_DIM), lambda b, r: (b, 0, 0)),
        ],
        out_shape=[
            jax.ShapeDtypeStruct((t, D_SSM), BF16),
            jax.ShapeDtypeStruct((batch, D_STATE, D_SSM), F32),
            jax.ShapeDtypeStruct((batch, 8, CONV_DIM), F32),
        ],
        scratch_shapes=[
            pltpu.VMEM((rb, CONV_DIM), F32),
            pltpu.VMEM((rb, D_SSM), F32),
            pltpu.VMEM((rb, LANES), F32),
            pltpu.VMEM((rb, LANES), F32),
            pltpu.VMEM((D_STATE, D_SSM), F32),
            pltpu.VMEM((8, CONV_DIM), F32),
        ],
        compiler_params=_cparams(("parallel", "arbitrary")),
        name="ssd_mixer",
    )(proj, proj, proj, tail0, s0, cw, cb, dtb, alog, dskip, nw, emat)


def _mla_prep_body(qa_ref, kva_ref, ma_ref, mb_ref, qnw_ref, kvnw_ref, wq1_ref, wq2_ref, wk_ref, wv_ref,
                   cq_ref, sq_ref, ck_ref, sk_ref, q_ref, k_ref, v_ref, ckv_ref, kr_ref):
    qa16 = _rms_norm(qa_ref[...], qnw_ref[...]).astype(BF16)
    q1 = _dot(qa16, wq1_ref[...])
    q2 = _dot(qa16, wq2_ref[...])
    cq = jnp.tile(cq_ref[...], (1, N_MLA_HEADS))
    sq = jnp.tile(sq_ref[...], (1, N_MLA_HEADS))
    q_ref[...] = (q1 * cq + q2 * sq).astype(BF16)
    ckv = _rms_norm(kva_ref[...], kvnw_ref[...])
    ckv_ref[...] = ckv
    ckv16 = ckv.astype(BF16)
    kr = ma_ref[...] * ck_ref[...] + mb_ref[...] * sk_ref[...]
    kr_ref[...] = kr
    k_ref[...] = (_dot(ckv16, wk_ref[...]) + jnp.tile(kr, (1, N_MLA_HEADS))).astype(BF16)
    v_ref[...] = _dot(ckv16, wv_ref[...]).astype(BF16)


def _mla_prep_call(proj, qnw, kvnw, wq1, wq2, wk, wv, cq, sq, ck, sk, *, seq, tm):
    t = proj.shape[0]
    npos = seq // tm
    hp = N_MLA_HEADS * HEAD_PAD
    tab = pl.BlockSpec((tm, LANES), lambda i: (i % npos, 0))
    return pl.pallas_call(
        _mla_prep_body,
        grid=(t // tm,),
        in_specs=[
            pl.BlockSpec((tm, Q_LORA), lambda i: (i, COL_QA // Q_LORA)),
            pl.BlockSpec((tm, KV_LORA), lambda i: (i, COL_KVA // KV_LORA)),
            pl.BlockSpec((tm, LANES), lambda i: (i, COL_MISC_A // LANES)),
            pl.BlockSpec((tm, LANES), lambda i: (i, COL_MISC_B // LANES)),
            _const_spec((1, Q_LORA)), _const_spec((1, KV_LORA)),
            _const_spec((Q_LORA, hp)), _const_spec((Q_LORA, hp)),
            _const_spec((KV_LORA, hp)), _const_spec((KV_LORA, hp)),
            tab, tab, tab, tab,
        ],
        out_specs=[
            pl.BlockSpec((tm, hp), lambda i: (i, 0)),
            pl.BlockSpec((tm, hp), lambda i: (i, 0)),
            pl.BlockSpec((tm, hp), lambda i: (i, 0)),
            pl.BlockSpec((tm, KV_LORA), lambda i: (i, 0)),
            pl.BlockSpec((tm, LANES), lambda i: (i, 0)),
        ],
        out_shape=[
            jax.ShapeDtypeStruct((t, hp), BF16),
            jax.ShapeDtypeStruct((t, hp), BF16),
            jax.ShapeDtypeStruct((t, hp), BF16),
            jax.ShapeDtypeStruct((t, KV_LORA), F32),
            jax.ShapeDtypeStruct((t, LANES), F32),
        ],
        compiler_params=_cparams(("parallel",)),
        name="mla_prep",
    )(proj, proj, proj, proj, qnw, kvnw, wq1, wq2, wk, wv, cq, sq, ck, sk)


def _attn_body(q_ref, k_ref, v_ref, o_ref, m_ref, l_ref, acc_ref, *, tq, tk):
    i = pl.program_id(1)
    j = pl.program_id(2)

    @pl.when(j == 0)
    def _():
        m_ref[...] = jnp.full(m_ref.shape, NEG_INF, F32)
        l_ref[...] = jnp.zeros(l_ref.shape, F32)
        acc_ref[...] = jnp.zeros(acc_ref.shape, F32)

    @pl.when(j <= i)
    def _():
        qpos = i * tq + lax.broadcasted_iota(jnp.int32, (tq, tk), 0)
        kpos = j * tk + lax.broadcasted_iota(jnp.int32, (tq, tk), 1)
        visible = kpos < ((qpos // CHUNK + 1) * CHUNK)
        for h in range(N_MLA_HEADS):
            hs = slice(h * HEAD_PAD, (h + 1) * HEAD_PAD)
            s = _dot_nt(q_ref[0, :, hs], k_ref[0, :, hs])
            s = jnp.where(visible, s, NEG_INF)
            m_prev = m_ref[h]
            m_new = jnp.maximum(m_prev, jnp.max(s, axis=-1, keepdims=True))
            alpha = jnp.exp(m_prev - m_new)
            p = jnp.exp(s - jnp.tile(m_new, (1, tk // LANES)))
            l_ref[h] = alpha * l_ref[h] + jnp.sum(p, axis=-1, keepdims=True)
            acc_ref[h] = alpha * acc_ref[h] + _dot(p.astype(BF16), v_ref[0, :, hs])
            m_ref[h] = m_new

    @pl.when(j == i)
    def _():
        for h in range(0, N_MLA_HEADS, 2):
            a = acc_ref[h] / l_ref[h]
            b = acc_ref[h + 1] / l_ref[h + 1]
            o_ref[0, :, h * D_V:(h + 2) * D_V] = jnp.concatenate([a[:, :D_V], b[:, :D_V]], axis=1)


def _attn_call(q, k, v, *, batch, seq, tq, tk):
    hp = N_MLA_HEADS * HEAD_PAD
    q = q.reshape(batch, seq, hp)
    k = k.reshape(batch, seq, hp)
    v = v.reshape(batch, seq, hp)
    nq = seq // tq
    nk = seq // tk
    kv_spec = pl.BlockSpec((1, tk, hp), lambda b, i, j: (b, jnp.minimum(j, i), 0))
    body = functools.partial(_attn_body, tq=tq, tk=tk)
    out = pl.pallas_call(
        body,
        grid=(batch, nq, nk),
        in_specs=[pl.BlockSpec((1, tq, hp), lambda b, i, j: (b, i, 0)), kv_spec, kv_spec],
        out_specs=pl.BlockSpec((1, tq, D_MLA), lambda b, i, j: (b, i, 0)),
        out_shape=jax.ShapeDtypeStruct((batch, seq, D_MLA), F32),
        scratch_shapes=[
            pltpu.VMEM((N_MLA_HEADS, tq, LANES), F32),
            pltpu.VMEM((N_MLA_HEADS, tq, LANES), F32),
            pltpu.VMEM((N_MLA_HEADS, tq, LANES), F32),
        ],
        compiler_params=_cparams(("parallel", "parallel", "arbitrary")),
        name="mla_attn",
    )(q, k, v)
    return out.reshape(batch * seq, D_MLA)


def _mla_prep_s_body(qa_ref, kva_ref, ma_ref, mb_ref, qnw_ref, kvnw_ref, wqn_ref, wqr_ref, wqr2_ref, wukt_ref,
                     cq_ref, sq_ref, ck_ref, sk_ref, qlat_ref, qr_ref, ckv_ref, kr_ref):
    qa16 = _rms_norm(qa_ref[...], qnw_ref[...]).astype(BF16)
    qn16 = _dot(qa16, wqn_ref[...]).astype(BF16)
    for h in range(N_MLA_HEADS):
        qh = qn16[:, h * D_NOPE:(h + 1) * D_NOPE]
        qlat_ref[:, h * KV_LORA:(h + 1) * KV_LORA] = (_dot(qh, wukt_ref[h]) * MLA_SCALE).astype(BF16)
    qr = _dot(qa16, wqr_ref[...]) * cq_ref[...] + _dot(qa16, wqr2_ref[...]) * sq_ref[...]
    qr_ref[...] = qr.astype(BF16)
    ckv_ref[...] = _rms_norm(kva_ref[...], kvnw_ref[...])
    kr_ref[...] = ma_ref[...] * ck_ref[...] + mb_ref[...] * sk_ref[...]


def _mla_prep_s_call(proj, qnw, kvnw, wqn, wqr, wqr2, wukt, cq, sq, ck, sk):
    t = proj.shape[0]
    nr = N_MLA_HEADS * D_ROPE
    return pl.pallas_call(
        _mla_prep_s_body,
        grid=(1,),
        in_specs=[
            pl.BlockSpec((t, Q_LORA), lambda i: (0, COL_QA // Q_LORA)),
            pl.BlockSpec((t, KV_LORA), lambda i: (0, COL_KVA // KV_LORA)),
            pl.BlockSpec((t, LANES), lambda i: (0, COL_MISC_A // LANES)),
            pl.BlockSpec((t, LANES), lambda i: (0, COL_MISC_B // LANES)),
            _const_spec((1, Q_LORA)), _const_spec((1, KV_LORA)),
            _const_spec((Q_LORA, D_MLA)), _const_spec((Q_LORA, nr)), _const_spec((Q_LORA, nr)),
            _const_spec((N_MLA_HEADS, D_NOPE, KV_LORA)),
            _const_spec((t, nr)), _const_spec((t, nr)), _const_spec((t, LANES)), _const_spec((t, LANES)),
        ],
        out_specs=[
            _const_spec((t, N_MLA_HEADS * KV_LORA)),
            _const_spec((t, nr)),
            _const_spec((t, KV_LORA)),
            _const_spec((t, LANES)),
        ],
        out_shape=[
            jax.ShapeDtypeStruct((t, N_MLA_HEADS * KV_LORA), BF16),
            jax.ShapeDtypeStruct((t, nr), BF16),
            jax.ShapeDtypeStruct((t, KV_LORA), F32),
            jax.ShapeDtypeStruct((t, LANES), F32),
        ],
        compiler_params=_cparams(("arbitrary",)),
        name="mla_prep_dec",
    )(proj, proj, proj, proj, qnw, kvnw, wqn, wqr, wqr2, wukt, cq, sq, ck, sk)


def _attn_s_body(ql_ref, qr_ref, cc_ref, ckr_ref, nc_ref, nkr_ref, wuv_ref, o_ref, m_ref, l_ref, acc_ref, *, lq):
    j = pl.program_id(1)
    nj = pl.num_programs(1)
    ql = ql_ref[0]
    qr = qr_ref[0]

    @pl.when(j == 0)
    def _():
        m_ref[...] = jnp.full(m_ref.shape, NEG_INF, F32)
        l_ref[...] = jnp.zeros(l_ref.shape, F32)
        acc_ref[...] = jnp.zeros(acc_ref.shape, F32)

    def update(c16, kr16):
        s = _dot_nt(ql, c16) + _dot_nt(qr, kr16)
        m_prev = m_ref[...]
        m_new = jnp.maximum(m_prev, jnp.max(s, axis=-1, keepdims=True))
        alpha = jnp.exp(m_prev - m_new)
        p = jnp.exp(s - m_new)
        l_ref[...] = alpha * l_ref[...] + jnp.sum(p, axis=-1, keepdims=True)
        acc_ref[...] = alpha * acc_ref[...] + _dot(p.astype(BF16), c16)
        m_ref[...] = m_new

    update(cc_ref[0].astype(BF16), ckr_ref[0].astype(BF16))

    @pl.when(j == nj - 1)
    def _():
        update(nc_ref[0].astype(BF16), nkr_ref[0].astype(BF16))
        olat = (acc_ref[...] / l_ref[...]).astype(BF16)
        for h in range(N_MLA_HEADS):
            o_ref[0, h * lq:(h + 1) * lq, :] = _dot(olat[h * lq:(h + 1) * lq, :], wuv_ref[h])


def _attn_s_call(qlat, qr, cache_ckv, cache_kr, new_ckv, new_kr, wuv, *, tk):
    b, rows, _ = qlat.shape
    lq = rows // N_MLA_HEADS
    past = cache_ckv.shape[1]
    body = functools.partial(_attn_s_body, lq=lq)
    return pl.pallas_call(
        body,
        grid=(b, past // tk),
        in_specs=[
            pl.BlockSpec((1, rows, KV_LORA), lambda i, j: (i, 0, 0)),
            pl.BlockSpec((1, rows, D_ROPE), lambda i, j: (i, 0, 0)),
            pl.BlockSpec((1, tk, KV_LORA), lambda i, j: (i, j, 0)),
            pl.BlockSpec((1, tk, D_ROPE), lambda i, j: (i, j, 0)),
            pl.BlockSpec((1, lq, KV_LORA), lambda i, j: (i, 0, 0)),
            pl.BlockSpec((1, lq, D_ROPE), lambda i, j: (i, 0, 0)),
            _const_spec((N_MLA_HEADS, KV_LORA, D_V)),
        ],
        out_specs=pl.BlockSpec((1, rows, D_V), lambda i, j: (i, 0, 0)),
        out_shape=jax.ShapeDtypeStruct((b, rows, D_V), F32),
        scratch_shapes=[
            pltpu.VMEM((rows, 1), F32),
            pltpu.VMEM((rows, 1), F32),
            pltpu.VMEM((rows, KV_LORA), F32),
        ],
        compiler_params=_cparams(("parallel", "arbitrary")),
        name="mla_attn_dec",
    )(qlat, qr, cache_ckv, cache_kr, new_ckv, new_kr, wuv)


def _mem_attn_body(q_ref, mk_ref, mv_ref, o_ref):
    scale = MEM_HEAD_DIM ** -0.5
    for h in range(MEM_HEADS):
        hs = slice(h * MEM_HEAD_DIM, (h + 1) * MEM_HEAD_DIM)
        s = _dot_nt(q_ref[:, hs].astype(BF16), mk_ref[0, :, hs].astype(BF16)) * scale
        p = jnp.exp(s - jnp.max(s, axis=-1, keepdims=True))
        l = jnp.sum(p, axis=-1, keepdims=True)
        o_ref[:, hs] = (_dot(p.astype(BF16), mv_ref[0, :, hs].astype(BF16)) / l).astype(BF16)


def _mem_attn_call(proj, mk, mv, *, batch, seq, tm):
    t = batch * seq
    nrb = seq // tm
    return pl.pallas_call(
        _mem_attn_body,
        grid=(batch, nrb),
        in_specs=[
            pl.BlockSpec((tm, D_MEM), lambda b, r: (b * nrb + r, COL_QMEM // D_MEM)),
            pl.BlockSpec((1, N_MEM, D_MEM), lambda b, r: (b, 0, 0)),
            pl.BlockSpec((1, N_MEM, D_MEM), lambda b, r: (b, 0, 0)),
        ],
        out_specs=pl.BlockSpec((tm, D_MEM), lambda b, r: (b * nrb + r, 0)),
        out_shape=jax.ShapeDtypeStruct((t, D_MEM), BF16),
        compiler_params=_cparams(("parallel", "parallel")),
        name="mem_attn",
    )(proj, mk, mv)


def _merge_body(y_ref, o_ref, g_ref, om_ref, gates_ref, x_ref, wos_ref, woa_ref, wom_ref, wout_ref,
                lng_ref, lnb_ref, x32_ref, x16_ref):
    ys = _dot(y_ref[...], wos_ref[...])
    ya = _dot((o_ref[...] * _silu(g_ref[...])).astype(BF16), woa_ref[...])
    ym = _dot(om_ref[...], wom_ref[...])
    gs = jax.nn.sigmoid(gates_ref[...])
    h = gs[:, 0:D_MODEL] * ys + gs[:, D_MODEL:2 * D_MODEL] * ya + gs[:, 2 * D_MODEL:3 * D_MODEL] * ym
    out = _dot(h.astype(BF16), wout_ref[...])
    xn = _layer_norm(ALPHA * x_ref[...] + out, lng_ref[...], lnb_ref[...])
    x32_ref[...] = xn
    x16_ref[...] = xn.astype(BF16)


def _merge_call(y, o, proj, om, x, wos, woa, wom, wout, lng, lnb, *, tm):
    t = x.shape[0]
    rowspec = lambda w: pl.BlockSpec((tm, w), lambda i: (i, 0))
    return pl.pallas_call(
        _merge_body,
        grid=(t // tm,),
        in_specs=[
            rowspec(D_SSM), rowspec(D_MLA),
            pl.BlockSpec((tm, D_MLA), lambda i: (i, COL_GMLA // D_MLA)),
            rowspec(D_MEM),
            pl.BlockSpec((tm, 3 * D_MODEL), lambda i: (i, COL_GATES // (3 * D_MODEL))),
            rowspec(D_MODEL),
            _const_spec((D_SSM, D_MODEL)), _const_spec((D_MLA, D_MODEL)),
            _const_spec((D_MEM, D_MODEL)), _const_spec((D_MODEL, D_MODEL)),
            _const_spec((1, D_MODEL)), _const_spec((1, D_MODEL)),
        ],
        out_specs=[rowspec(D_MODEL), rowspec(D_MODEL)],
        out_shape=[jax.ShapeDtypeStruct((t, D_MODEL), F32), jax.ShapeDtypeStruct((t, D_MODEL), BF16)],
        compiler_params=_cparams(("parallel",)),
        name="merge_out",
    )(y, o, proj, om, proj, x, wos, woa, wom, wout, lng, lnb)


def _rot_cols(w):
    half = D_ROPE // 2
    return jnp.concatenate([-w[..., half:], w[..., :half]], axis=-1)


def _pack_layer(p, i):
    w_in = p['w_in'][i]
    seg = {}
    off = 0
    for name, width in (('z', D_SSM), ('xbc', CONV_DIM), ('dt', N_SSM_HEADS), ('qa', Q_LORA), ('kva', KV_LORA),
                        ('kr', D_ROPE), ('gmla', D_MLA), ('qmem', D_MEM), ('gates', 3 * D_MODEL)):
        seg[name] = w_in[:, off:off + width]
        off += width
    zc = lambda n: jnp.zeros((D_MODEL, n), F32)
    w_proj = jnp.concatenate([
        seg['xbc'], seg['gates'], seg['z'], seg['gmla'], seg['qmem'], seg['qa'], seg['kva'],
        seg['dt'], zc(32), seg['kr'], zc(32),
        zc(64), _rot_cols(seg['kr']), zc(32),
    ], axis=1).astype(BF16)

    wqb = p['w_q_b'][i].reshape(Q_LORA, N_MLA_HEADS, D_NOPE + D_ROPE)
    wq_n = wqb[:, :, :D_NOPE]
    wq_r = wqb[:, :, D_NOPE:]
    zq = jnp.zeros((Q_LORA, N_MLA_HEADS, HEAD_PAD - D_NOPE - D_ROPE), F32)
    wq1 = jnp.concatenate([wq_n, wq_r, zq], axis=-1).reshape(Q_LORA, -1).astype(BF16)
    wq2 = jnp.concatenate([jnp.zeros_like(wq_n), _rot_cols(wq_r), zq], axis=-1).reshape(Q_LORA, -1).astype(BF16)
    wuk = p['w_uk'][i]
    wuv = p['w_uv'][i]
    zk = jnp.zeros((KV_LORA, N_MLA_HEADS, HEAD_PAD - D_NOPE), F32)
    wk = jnp.concatenate([wuk, zk], axis=-1).reshape(KV_LORA, -1).astype(BF16)
    wv = jnp.concatenate([wuv, zk], axis=-1).reshape(KV_LORA, -1).astype(BF16)

    pad128 = lambda v: jnp.pad(v, (0, LANES - v.shape[0])).reshape(1, LANES)
    return dict(
        w_proj=w_proj,
        cw=p['conv_w'][i], cb=p['conv_b'][i].reshape(1, CONV_DIM),
        dtb=pad128(p['dt_bias'][i]), alog=pad128(p['a_log'][i]),
        dskip=jnp.repeat(p['d_skip'][i], SSM_HEAD_DIM).reshape(1, D_SSM),
        nw=p['ssm_norm_w'][i].reshape(1, D_SSM),
        qnw=p['q_a_norm_w'][i].reshape(1, Q_LORA), kvnw=p['kv_norm_w'][i].reshape(1, KV_LORA),
        wq1=wq1, wq2=wq2, wk=wk, wv=wv,
        wqn=wq_n.reshape(Q_LORA, -1).astype(BF16),
        wqr=wq_r.reshape(Q_LORA, -1).astype(BF16),
        wqr2=_rot_cols(wq_r).reshape(Q_LORA, -1).astype(BF16),
        wukt=jnp.transpose(wuk, (1, 2, 0)).astype(BF16),
        wuv=jnp.transpose(wuv, (1, 0, 2)).astype(BF16),
        wmk=p['w_mem_k'][i].astype(BF16), wmv=p['w_mem_v'][i].astype(BF16),
        wos=p['w_o_ssm'][i].astype(BF16), woa=p['w_o_mla'][i].astype(BF16),
        wom=p['w_o_mem'][i].astype(BF16), wout=p['w_out'][i].astype(BF16),
        lng=p['ln_g'][i].reshape(1, D_MODEL), lnb=p['ln_b'][i].reshape(1, D_MODEL),
    )


def _rope_tables(pos):
    half = D_ROPE // 2
    inv = ROPE_THETA ** (-jnp.arange(half, dtype=F32) / half)
    ang = pos.astype(F32)[:, None] * inv[None, :]
    cos = jnp.cos(ang)
    sin = jnp.sin(ang)
    return jnp.concatenate([cos, cos], axis=-1), jnp.concatenate([sin, sin], axis=-1)


def _expansion_matrix():
    e = np.zeros((LANES, D_SSM), np.float32)
    for h in range(N_SSM_HEADS):
        e[h, h * SSM_HEAD_DIM:(h + 1) * SSM_HEAD_DIM] = 1.0
    return jnp.asarray(e, dtype=BF16)


def _ssm_branch(proj, lw, conv_prev, ssm_prev, emat, *, batch, seq, rb, tc):
    tail0 = jnp.pad(conv_prev, ((0, 0), (8 - (CONV_W - 1), 0), (0, 0)))
    s0 = jnp.transpose(ssm_prev, (0, 3, 1, 2)).reshape(batch, D_STATE, D_SSM)
    y, sfin, tail = _ssm_call(proj, tail0, s0, lw['cw'], lw['cb'], lw['dtb'], lw['alog'], lw['dskip'], lw['nw'],
                              emat, batch=batch, seq=seq, rb=rb, tc=tc)
    ssm_new = jnp.transpose(sfin.reshape(batch, D_STATE, N_SSM_HEADS, SSM_HEAD_DIM), (0, 2, 3, 1))
    return y, ssm_new, tail[:, 8 - (CONV_W - 1):, :]


def kernel(x_prompt, x_sample, mem_prompt, state_ssm, state_conv, cache_ckv, cache_krope, cache_mem_k, cache_mem_v,
           ln_in_g, ln_in_b, w_in, conv_w, conv_b, dt_bias, a_log, d_skip, ssm_norm_w, w_o_ssm, q_a_norm_w, w_q_b,
           kv_norm_w, w_uk, w_uv, w_o_mla, w_mem_k, w_mem_v, w_o_mem, w_out, ln_g, ln_b):
    params = dict(w_in=w_in, conv_w=conv_w, conv_b=conv_b, dt_bias=dt_bias, a_log=a_log, d_skip=d_skip,
                  ssm_norm_w=ssm_norm_w, w_o_ssm=w_o_ssm, q_a_norm_w=q_a_norm_w, w_q_b=w_q_b, kv_norm_w=kv_norm_w,
                  w_uk=w_uk, w_uv=w_uv, w_o_mla=w_o_mla, w_mem_k=w_mem_k, w_mem_v=w_mem_v, w_o_mem=w_o_mem,
                  w_out=w_out, ln_g=ln_g, ln_b=ln_b)
    bp, lp, _ = x_prompt.shape
    bs, ls, _ = x_sample.shape
    past = cache_ckv.shape[2]
    tp = bp * lp
    ts = bs * ls
    assert (past // CHUNK + 1) * CHUNK >= past + ls, "decode block would need a causal mask"

    emat = _expansion_matrix()
    tm_p = min(512, tp)
    xp32, xp16 = _ln_call(x_prompt.reshape(tp, D_MODEL), ln_in_g, ln_in_b, tm_p)
    xs32, xs16 = _ln_call(x_sample.reshape(ts, D_MODEL), ln_in_g, ln_in_b, ts)
    mem16 = mem_prompt.reshape(bp * N_MEM, D_MODEL).astype(BF16)

    cos_p, sin_p = _rope_tables(jnp.arange(lp, dtype=jnp.int32))
    cos_s, sin_s = _rope_tables(past + jnp.arange(ls, dtype=jnp.int32))
    z64 = lambda c: jnp.zeros((c.shape[0], 64), F32)
    z32 = lambda c: jnp.zeros((c.shape[0], 32), F32)
    one64 = jnp.ones((lp, 64), F32)
    cq_p = jnp.concatenate([one64, cos_p, z32(cos_p)], axis=1) * MLA_SCALE
    sq_p = jnp.concatenate([z64(sin_p), sin_p, z32(sin_p)], axis=1) * MLA_SCALE
    ck_p = jnp.concatenate([z64(cos_p), cos_p, z32(cos_p)], axis=1)
    sk_p = jnp.concatenate([z64(sin_p), sin_p, z32(sin_p)], axis=1)
    cq_s = jnp.tile(jnp.tile(cos_s, (1, N_MLA_HEADS)), (bs, 1)) * MLA_SCALE
    sq_s = jnp.tile(jnp.tile(sin_s, (1, N_MLA_HEADS)), (bs, 1)) * MLA_SCALE
    ck_s = jnp.tile(jnp.concatenate([z64(cos_s), cos_s, z32(cos_s)], axis=1), (bs, 1))
    sk_s = jnp.tile(jnp.concatenate([z64(sin_s), sin_s, z32(sin_s)], axis=1), (bs, 1))

    conv0 = jnp.zeros((bp, CONV_W - 1, CONV_DIM), F32)
    ssm0 = jnp.zeros((bp, N_SSM_HEADS, SSM_HEAD_DIM, D_STATE), F32)

    rb_p = min(256, lp)
    tc_p = min(128, lp)
    tq = min(512, lp)
    tmm = min(1024, tp)

    outs = {k: [] for k in ('p_ssm', 'p_conv', 'p_ckv', 'p_kr', 'p_mk', 'p_mv', 's_ssm', 's_conv', 's_ckv', 's_kr')}
    for i in range(DEPTH):
        lw = _pack_layer(params, i)

        proj = _mm_call(xp16, lw['w_proj'], tmm, 1024, "in_proj")
        y, ssm_new, conv_new = _ssm_branch(proj, lw, conv0, ssm0, emat, batch=bp, seq=lp, rb=rb_p, tc=tc_p)
        q, k, v, ckv, kr = _mla_prep_call(proj, lw['qnw'], lw['kvnw'], lw['wq1'], lw['wq2'], lw['wk'], lw['wv'],
                                          cq_p, sq_p, ck_p, sk_p, seq=lp, tm=tq)
        o = _attn_call(q, k, v, batch=bp, seq=lp, tq=tq, tk=tq)
        mk = _mm_call(mem16, lw['wmk'], bp * N_MEM, 1024, "mem_k_proj")
        mv = _mm_call(mem16, lw['wmv'], bp * N_MEM, 1024, "mem_v_proj")
        om = _mem_attn_call(proj, mk.reshape(bp, N_MEM, D_MEM), mv.reshape(bp, N_MEM, D_MEM), batch=bp, seq=lp, tm=tq)
        xp32, xp16 = _merge_call(y, o, proj, om, xp32, lw['wos'], lw['woa'], lw['wom'], lw['wout'],
                                 lw['lng'], lw['lnb'], tm=min(256, tp))
        outs['p_ssm'].append(ssm_new)
        outs['p_conv'].append(conv_new)
        outs['p_ckv'].append(ckv.reshape(bp, lp, KV_LORA))
        outs['p_kr'].append(kr[:, 64:64 + D_ROPE].reshape(bp, lp, D_ROPE))
        outs['p_mk'].append(mk.reshape(bp, N_MEM, MEM_HEADS, MEM_HEAD_DIM))
        outs['p_mv'].append(mv.reshape(bp, N_MEM, MEM_HEADS, MEM_HEAD_DIM))

        proj_s = _mm_call(xs16, lw['w_proj'], ts, 1024, "in_proj_dec")
        y_s, ssm_new_s, conv_new_s = _ssm_branch(proj_s, lw, state_conv[i], state_ssm[i], emat,
                                                 batch=bs, seq=ls, rb=ls, tc=ls)
        qlat, qr, ckv_s, kr_s = _mla_prep_s_call(proj_s, lw['qnw'], lw['kvnw'], lw['wqn'], lw['wqr'], lw['wqr2'],
                                                 lw['wukt'], cq_s, sq_s, ck_s, sk_s)
        kr_s = kr_s[:, 64:64 + D_ROPE]
        to_hq = lambda a, w: jnp.transpose(a.reshape(bs, ls, N_MLA_HEADS, w), (0, 2, 1, 3)).reshape(bs, N_MLA_HEADS * ls, w)
        o_s = _attn_s_call(to_hq(qlat, KV_LORA), to_hq(qr, D_ROPE), cache_ckv[i], cache_krope[i],
                           ckv_s.reshape(bs, ls, KV_LORA), kr_s.reshape(bs, ls, D_ROPE), lw['wuv'], tk=min(1024, past))
        o_s = jnp.transpose(o_s.reshape(bs, N_MLA_HEADS, ls, D_V), (0, 2, 1, 3)).reshape(ts, D_MLA)
        om_s = _mem_attn_call(proj_s, cache_mem_k[i].reshape(bs, N_MEM, D_MEM), cache_mem_v[i].reshape(bs, N_MEM, D_MEM),
                              batch=bs, seq=ls, tm=ls)
        xs32, xs16 = _merge_call(y_s, o_s, proj_s, om_s, xs32, lw['wos'], lw['woa'], lw['wom'], lw['wout'],
                                 lw['lng'], lw['lnb'], tm=min(256, ts))
        outs['s_ssm'].append(ssm_new_s)
        outs['s_conv'].append(conv_new_s)
        outs['s_ckv'].append(ckv_s.reshape(bs, ls, KV_LORA))
        outs['s_kr'].append(kr_s.reshape(bs, ls, D_ROPE))

    st = lambda k: jnp.stack(outs[k])
    return (xp32.reshape(bp, lp, D_MODEL), xs32.reshape(bs, ls, D_MODEL), st('p_ssm'), st('p_conv'), st('p_ckv'),
            st('p_kr'), st('p_mk'), st('p_mv'), st('s_ssm'), st('s_conv'), st('s_ckv'), st('s_kr'))
```

```python
import functools
import math

import numpy as np
import jax
import jax.numpy as jnp
from jax import lax
from jax.experimental import pallas as pl
from jax.experimental.pallas import tpu as pltpu

F32 = jnp.float32
BF16 = jnp.bfloat16

D_MODEL = 1024
DEPTH = 4
CHUNK = 64
D_SSM = 2048
SSM_HEAD_DIM = 64
N_SSM_HEADS = 32
SSM_GROUPS = 4
HEADS_PER_GROUP = N_SSM_HEADS // SSM_GROUPS
D_STATE = 128
CONV_W = 4
CONV_DIM = D_SSM + 2 * SSM_GROUPS * D_STATE
N_MLA_HEADS = 16
D_NOPE = 64
D_ROPE = 32
D_V = 64
Q_LORA = 512
KV_LORA = 256
D_MLA = N_MLA_HEADS * D_V
MLA_SCALE = (D_NOPE + D_ROPE) ** -0.5
ROPE_THETA = 10000.0
N_MEM = 256
MEM_HEADS = 4
MEM_HEAD_DIM = 256
D_MEM = MEM_HEADS * MEM_HEAD_DIM
ALPHA = (2 * DEPTH) ** 0.25
EPS = 1e-5
NEG_INF = -1e30
LOG2E = math.log2(math.e)

LANES = 128
HEAD_PAD = 128
VMEM_LIMIT = 56 * 1024 * 1024

COL_XBC = 0
COL_GATES = 3072
COL_Z = 6144
COL_GMLA = 8192
COL_QMEM = 9216
COL_QA = 10240
COL_KVA = 10752
COL_MISC_A = 11008
COL_MISC_B = 11136
N_PROJ = 11264


def _cparams(sem):
    return pltpu.CompilerParams(dimension_semantics=sem, vmem_limit_bytes=VMEM_LIMIT)


def _const_spec(shape):
    nd = len(shape)
    return pl.BlockSpec(shape, lambda *_: (0,) * nd)


def _silu(x):
    return x * jax.nn.sigmoid(x)


def _softplus(x):
    return jnp.maximum(x, 0.0) + jnp.log1p(jnp.exp(-jnp.abs(x)))


def _dot(a, b):
    return jnp.dot(a, b, preferred_element_type=F32)


def _dot_nt(a, b):
    return lax.dot_general(a, b, (((1,), (1,)), ((), ())), preferred_element_type=F32)


def _layer_norm(x, g, b):
    mu = jnp.mean(x, axis=-1, keepdims=True)
    xc = x - mu
    var = jnp.mean(xc * xc, axis=-1, keepdims=True)
    return xc * lax.rsqrt(var + EPS) * g + b


def _rms_norm(x, w):
    return x * lax.rsqrt(jnp.mean(x * x, axis=-1, keepdims=True) + EPS) * w


def _ln_body(x_ref, g_ref, b_ref, o32_ref, o16_ref):
    y = _layer_norm(x_ref[...], g_ref[...], b_ref[...])
    o32_ref[...] = y
    o16_ref[...] = y.astype(BF16)


def _ln_call(x, g, b, tm):
    t, d = x.shape
    return pl.pallas_call(
        _ln_body,
        grid=(t // tm,),
        in_specs=[pl.BlockSpec((tm, d), lambda i: (i, 0)), _const_spec((1, d)), _const_spec((1, d))],
        out_specs=[pl.BlockSpec((tm, d), lambda i: (i, 0)), pl.BlockSpec((tm, d), lambda i: (i, 0))],
        out_shape=[jax.ShapeDtypeStruct((t, d), F32), jax.ShapeDtypeStruct((t, d), BF16)],
        compiler_params=_cparams(("parallel",)),
        name="ln_in",
    )(x, g.reshape(1, d), b.reshape(1, d))


def _mm_body(x_ref, w_ref, o_ref):
    o_ref[...] = _dot(x_ref[...], w_ref[...]).astype(o_ref.dtype)


def _mm_call(x, w, tm, tn, name):
    m, k = x.shape
    n = w.shape[1]
    return pl.pallas_call(
        _mm_body,
        grid=(m // tm, n // tn),
        in_specs=[pl.BlockSpec((tm, k), lambda i, j: (i, 0)), pl.BlockSpec((k, tn), lambda i, j: (0, j))],
        out_specs=pl.BlockSpec((tm, tn), lambda i, j: (i, j)),
        out_shape=jax.ShapeDtypeStruct((m, n), F32),
        compiler_params=_cparams(("parallel", "parallel")),
        name=name,
    )(x, w)


def _cumsum_rows(x, t):
    row = lax.broadcasted_iota(jnp.int32, x.shape, 0)
    s = x
    k = 1
    while k < t:
        s = s + jnp.where(row >= k, pltpu.roll(s, k, 0), 0.0)
        k *= 2
    return s


def _expand_heads(w, e_ref):
    hi = w.astype(BF16)
    lo = (w - hi.astype(F32)).astype(BF16)
    e = e_ref[...]
    return _dot(hi, e) + _dot(lo, e)


def _ssm_body(xbc_ref, z_ref, misc_ref, tail0_ref, s0_ref, cw_ref, cb_ref, dtb_ref, alog_ref, dskip_ref,
              nw_ref, e_ref, y_ref, sfin_ref, tail_ref, cbuf, ybuf, dabuf, dtbuf, state, tail, *, rb, tc):
    r = pl.program_id(1)
    nr = pl.num_programs(1)

    @pl.when(r == 0)
    def _():
        state[...] = s0_ref[0]
        tail[...] = tail0_ref[0]

    x = xbc_ref[...]
    cw = cw_ref[...]
    cb = cb_ref[...]
    acc = cb + cw[3:4, :] * x
    for j in range(1, CONV_W):
        acc = acc + cw[3 - j:4 - j, :] * pltpu.roll(x, j, 0)
    cbuf[...] = _silu(acc)
    x8 = x[0:8, :]
    t8 = tail[...]
    row8 = lax.broadcasted_iota(jnp.int32, x8.shape, 0)
    acc8 = cb + cw[3:4, :] * x8
    for j in range(1, CONV_W):
        xj = jnp.where(row8 < j, pltpu.roll(t8, j, 0), pltpu.roll(x8, j, 0))
        acc8 = acc8 + cw[3 - j:4 - j, :] * xj
    cbuf[0:8, :] = _silu(acc8)
    tail[...] = x[rb - 8:rb, :]

    dt = _softplus(misc_ref[...] + dtb_ref[...])
    dtbuf[...] = dt
    dabuf[...] = dt * (-jnp.exp(alog_ref[...]))

    ri = lax.broadcasted_iota(jnp.int32, (tc, tc), 0)
    ci = lax.broadcasted_iota(jnp.int32, (tc, tc), 1)
    lower = ri >= ci
    pad_rows = LANES - tc

    def to_rows(a):
        if pad_rows:
            a = jnp.concatenate([a, jnp.zeros((pad_rows, LANES), F32)], axis=0)
        return a.T[:, :tc]

    def chunk(c, carry):
        r0 = pl.multiple_of(c * tc, tc)
        rows = pl.ds(r0, tc)
        da = dabuf[rows, :]
        dtc = dtbuf[rows, :]
        acs = _cumsum_rows(da, tc)
        acs_t = to_rows(acs)
        dt_t = to_rows(dtc)
        total = acs[tc - 1:tc, :]
        wexp = _expand_heads(jnp.exp(total - acs) * dtc, e_ref)
        etot = _expand_heads(jnp.broadcast_to(jnp.exp(total), (8, LANES)), e_ref)[0:1, :]
        xs = cbuf[rows, 0:D_SSM]
        xs16 = xs.astype(BF16)
        xw16 = (xs * wexp).astype(BF16)
        for g in range(SSM_GROUPS):
            bg = cbuf[rows, D_SSM + g * D_STATE:D_SSM + (g + 1) * D_STATE]
            cg = cbuf[rows, D_SSM + SSM_GROUPS * D_STATE + g * D_STATE:D_SSM + SSM_GROUPS * D_STATE + (g + 1) * D_STATE]
            bg16 = bg.astype(BF16)
            cg16 = cg.astype(BF16)
            cbm = _dot_nt(cg16, bg16)
            gcols = slice(g * HEADS_PER_GROUP * SSM_HEAD_DIM, (g + 1) * HEADS_PER_GROUP * SSM_HEAD_DIM)
            s_old = state[:, gcols]
            s16 = s_old.astype(BF16)
            for hh in range(0, HEADS_PER_GROUP, 2):
                pair = []
                for h in (g * HEADS_PER_GROUP + hh, g * HEADS_PER_GROUP + hh + 1):
                    colb = jnp.broadcast_to(acs[:, h:h + 1], (tc, LANES))
                    rowb = jnp.broadcast_to(acs_t[h:h + 1, :], (tc, tc))
                    dtb = jnp.broadcast_to(dt_t[h:h + 1, :], (tc, tc))
                    lmat = jnp.where(lower, jnp.exp(colb[:, :tc] - rowb), 0.0)
                    m16 = (cbm * lmat * dtb).astype(BF16)
                    cexp16 = (cg * jnp.exp(colb)).astype(BF16)
                    hl = h - g * HEADS_PER_GROUP
                    xh = xs16[:, h * SSM_HEAD_DIM:(h + 1) * SSM_HEAD_DIM]
                    sh = s16[:, hl * SSM_HEAD_DIM:(hl + 1) * SSM_HEAD_DIM]
                    if tc % LANES == 0:
                        yh = _dot(jnp.concatenate([m16, cexp16], axis=1), jnp.concatenate([xh, sh], axis=0))
                    else:
                        yh = _dot(m16, xh) + _dot(cexp16, sh)
                    pair.append(yh)
                h0 = g * HEADS_PER_GROUP + hh
                ybuf[rows, h0 * SSM_HEAD_DIM:(h0 + 2) * SSM_HEAD_DIM] = jnp.concatenate(pair, axis=1)
            bg_t = to_rows(bg).astype(BF16)
            state[:, gcols] = s_old * etot[:, gcols] + _dot(bg_t, xw16[:, gcols])
        y = ybuf[rows, :] + dskip_ref[...] * xs
        zc = z_ref[rows, :]
        gt = y * _silu(zc)
        gw = D_SSM // SSM_GROUPS
        for g in range(SSM_GROUPS):
            gg = gt[:, g * gw:(g + 1) * gw]
            ms = jnp.mean(gg * gg, axis=-1, keepdims=True)
            y_ref[rows, g * gw:(g + 1) * gw] = (gg * lax.rsqrt(ms + EPS) * nw_ref[:, g * gw:(g + 1) * gw]).astype(BF16)
        return carry

    lax.fori_loop(0, rb // tc, chunk, 0)

    @pl.when(r == nr - 1)
    def _():
        sfin_ref[0] = state[...]
        tail_ref[0] = tail[...]


def _ssm_call(proj, tail0, s0, cw, cb, dtb, alog, dskip, nw, emat, *, batch, seq, rb, tc):
    t = batch * seq
    nrb = seq // rb
    row = lambda b, r: b * nrb + r
    body = functools.partial(_ssm_body, rb=rb, tc=tc)
    return pl.pallas_call(
        body,
        grid=(batch, nrb),
        in_specs=[
            pl.BlockSpec((rb, CONV_DIM), lambda b, r: (row(b, r), COL_XBC // CONV_DIM)),
            pl.BlockSpec((rb, D_SSM), lambda b, r: (row(b, r), COL_Z // D_SSM)),
            pl.BlockSpec((rb, LANES), lambda b, r: (row(b, r), COL_MISC_A // LANES)),
            pl.BlockSpec((1, 8, CONV_DIM), lambda b, r: (b, 0, 0)),
            pl.BlockSpec((1, D_STATE, D_SSM), lambda b, r: (b, 0, 0)),
            _const_spec((CONV_W, CONV_DIM)),
            _const_spec((1, CONV_DIM)),
            _const_spec((1, LANES)),
            _const_spec((1, LANES)),
            _const_spec((1, D_SSM)),
            _const_spec((1, D_SSM)),
            _const_spec((LANES, D_SSM)),
        ],
        out_specs=[
            pl.BlockSpec((rb, D_SSM), lambda b, r: (row(b, r), 0)),
            pl.BlockSpec((1, D_STATE, D_SSM), lambda b, r: (b, 0, 0)),
            pl.BlockSpec((1, 8, CONV_DIM), lambda b, r: (b, 0, 0)),
        ],
        out_shape=[
            jax.ShapeDtypeStruct((t, D_SSM), BF16),
            jax.ShapeDtypeStruct((batch, D_STATE, D_SSM), F32),
            jax.ShapeDtypeStruct((batch, 8, CONV_DIM), F32),
        ],
        scratch_shapes=[
            pltpu.VMEM((rb, CONV_DIM), F32),
            pltpu.VMEM((rb, D_SSM), F32),
            pltpu.VMEM((rb, LANES), F32),
            pltpu.VMEM((rb, LANES), F32),
            pltpu.VMEM((D_STATE, D_SSM), F32),
            pltpu.VMEM((8, CONV_DIM), F32),
        ],
        compiler_params=_cparams(("parallel", "arbitrary")),
        name="ssd_mixer",
    )(proj, proj, proj, tail0, s0, cw, cb, dtb, alog, dskip, nw, emat)


def _mla_prep_body(qa_ref, kva_ref, ma_ref, mb_ref, qnw_ref, kvnw_ref, wq1_ref, wq2_ref, wk_ref, wv_ref,
                   cq_ref, sq_ref, ck_ref, sk_ref, q_ref, k_ref, v_ref, ckv_ref, kr_ref):
    qa16 = _rms_norm(qa_ref[...], qnw_ref[...]).astype(BF16)
    q1 = _dot(qa16, wq1_ref[...])
    q2 = _dot(qa16, wq2_ref[...])
    cq = jnp.tile(cq_ref[...], (1, N_MLA_HEADS))
    sq = jnp.tile(sq_ref[...], (1, N_MLA_HEADS))
    q_ref[...] = (q1 * cq + q2 * sq).astype(BF16)
    ckv = _rms_norm(kva_ref[...], kvnw_ref[...])
    ckv_ref[...] = ckv
    ckv16 = ckv.astype(BF16)
    kr = ma_ref[...] * ck_ref[...] + mb_ref[...] * sk_ref[...]
    kr_ref[...] = kr
    k_ref[...] = (_dot(ckv16, wk_ref[...]) + jnp.tile(kr, (1, N_MLA_HEADS))).astype(BF16)
    lane = lax.broadcasted_iota(jnp.int32, (1, N_MLA_HEADS * HEAD_PAD), 1)
    ones_col = jnp.where(lane % HEAD_PAD == D_V, 1.0, 0.0)
    v_ref[...] = (_dot(ckv16, wv_ref[...]) + ones_col).astype(BF16)


def _mla_prep_call(proj, qnw, kvnw, wq1, wq2, wk, wv, cq, sq, ck, sk, *, seq, tm):
    t = proj.shape[0]
    npos = seq // tm
    hp = N_MLA_HEADS * HEAD_PAD
    tab = pl.BlockSpec((tm, LANES), lambda i: (i % npos, 0))
    return pl.pallas_call(
        _mla_prep_body,
        grid=(t // tm,),
        in_specs=[
            pl.BlockSpec((tm, Q_LORA), lambda i: (i, COL_QA // Q_LORA)),
            pl.BlockSpec((tm, KV_LORA), lambda i: (i, COL_KVA // KV_LORA)),
            pl.BlockSpec((tm, LANES), lambda i: (i, COL_MISC_A // LANES)),
            pl.BlockSpec((tm, LANES), lambda i: (i, COL_MISC_B // LANES)),
            _const_spec((1, Q_LORA)), _const_spec((1, KV_LORA)),
            _const_spec((Q_LORA, hp)), _const_spec((Q_LORA, hp)),
            _const_spec((KV_LORA, hp)), _const_spec((KV_LORA, hp)),
            tab, tab, tab, tab,
        ],
        out_specs=[
            pl.BlockSpec((tm, hp), lambda i: (i, 0)),
            pl.BlockSpec((tm, hp), lambda i: (i, 0)),
            pl.BlockSpec((tm, hp), lambda i: (i, 0)),
            pl.BlockSpec((tm, KV_LORA), lambda i: (i, 0)),
            pl.BlockSpec((tm, LANES), lambda i: (i, 0)),
        ],
        out_shape=[
            jax.ShapeDtypeStruct((t, hp), BF16),
            jax.ShapeDtypeStruct((t, hp), BF16),
            jax.ShapeDtypeStruct((t, hp), BF16),
            jax.ShapeDtypeStruct((t, KV_LORA), F32),
            jax.ShapeDtypeStruct((t, LANES), F32),
        ],
        compiler_params=_cparams(("parallel",)),
        name="mla_prep",
    )(proj, proj, proj, proj, qnw, kvnw, wq1, wq2, wk, wv, cq, sq, ck, sk)


def _attn_body(qi_ref, kj_ref, q_ref, k_ref, v_ref, o_ref, m_ref, acc_ref, *, tq, tk):
    t = pl.program_id(1)
    i = qi_ref[t]
    j = kj_ref[t]

    @pl.when(j == 0)
    def _():
        m_ref[...] = jnp.full(m_ref.shape, NEG_INF, F32)
        acc_ref[...] = jnp.zeros(acc_ref.shape, F32)

    def step(visible):
        for h in range(N_MLA_HEADS):
            hs = slice(h * HEAD_PAD, (h + 1) * HEAD_PAD)
            s = _dot_nt(q_ref[0, :, hs], k_ref[0, :, hs])
            if visible is not None:
                s = jnp.where(visible, s, NEG_INF)
            m_prev = m_ref[h]
            m_new = jnp.maximum(m_prev, jnp.max(s, axis=-1, keepdims=True))
            p = jnp.exp2(s - jnp.tile(m_new, (1, tk // LANES)))
            acc_ref[h] = jnp.exp2(m_prev - m_new) * acc_ref[h] + _dot(p.astype(BF16), v_ref[0, :, hs])
            m_ref[h] = m_new

    @pl.when(j < i)
    def _():
        step(None)

    @pl.when(j == i)
    def _():
        row = lax.broadcasted_iota(jnp.int32, (tq, tk), 0)
        col = lax.broadcasted_iota(jnp.int32, (tq, tk), 1)
        step(col < ((row // CHUNK + 1) * CHUNK))
        for h in range(0, N_MLA_HEADS, 2):
            a = acc_ref[h]
            b = acc_ref[h + 1]
            a = a[:, :D_V] / a[:, D_V:D_V + 1]
            b = b[:, :D_V] / b[:, D_V:D_V + 1]
            o_ref[0, :, h * D_V:(h + 2) * D_V] = jnp.concatenate([a, b], axis=1)


def _attn_call(q, k, v, *, batch, seq, tq):
    hp = N_MLA_HEADS * HEAD_PAD
    q = q.reshape(batch, seq, hp)
    k = k.reshape(batch, seq, hp)
    v = v.reshape(batch, seq, hp)
    nq = seq // tq
    pairs = [(i, j) for i in range(nq) for j in range(i + 1)]
    qi = jnp.asarray([p[0] for p in pairs], jnp.int32)
    kj = jnp.asarray([p[1] for p in pairs], jnp.int32)
    kv_spec = pl.BlockSpec((1, tq, hp), lambda b, t, qi, kj: (b, kj[t], 0))
    body = functools.partial(_attn_body, tq=tq, tk=tq)
    out = pl.pallas_call(
        body,
        grid_spec=pltpu.PrefetchScalarGridSpec(
            num_scalar_prefetch=2,
            grid=(batch, len(pairs)),
            in_specs=[pl.BlockSpec((1, tq, hp), lambda b, t, qi, kj: (b, qi[t], 0)), kv_spec, kv_spec],
            out_specs=pl.BlockSpec((1, tq, D_MLA), lambda b, t, qi, kj: (b, qi[t], 0)),
            scratch_shapes=[
                pltpu.VMEM((N_MLA_HEADS, tq, LANES), F32),
                pltpu.VMEM((N_MLA_HEADS, tq, LANES), F32),
            ],
        ),
        out_shape=jax.ShapeDtypeStruct((batch, seq, D_MLA), F32),
        compiler_params=_cparams(("parallel", "arbitrary")),
        name="mla_attn",
    )(qi, kj, q, k, v)
    return out.reshape(batch * seq, D_MLA)


def _mla_prep_s_body(qa_ref, kva_ref, ma_ref, mb_ref, qnw_ref, kvnw_ref, wqn_ref, wqr_ref, wqr2_ref, wukt_ref,
                     cq_ref, sq_ref, ck_ref, sk_ref, qlat_ref, qr_ref, ckv_ref, kr_ref):
    qa16 = _rms_norm(qa_ref[...], qnw_ref[...]).astype(BF16)
    qn16 = _dot(qa16, wqn_ref[...]).astype(BF16)
    for h in range(N_MLA_HEADS):
        qh = qn16[:, h * D_NOPE:(h + 1) * D_NOPE]
        qlat_ref[:, h * KV_LORA:(h + 1) * KV_LORA] = (_dot(qh, wukt_ref[h]) * MLA_SCALE).astype(BF16)
    qr = _dot(qa16, wqr_ref[...]) * cq_ref[...] + _dot(qa16, wqr2_ref[...]) * sq_ref[...]
    qr_ref[...] = qr.astype(BF16)
    ckv_ref[...] = _rms_norm(kva_ref[...], kvnw_ref[...])
    kr_ref[...] = ma_ref[...] * ck_ref[...] + mb_ref[...] * sk_ref[...]


def _mla_prep_s_call(proj, qnw, kvnw, wqn, wqr, wqr2, wukt, cq, sq, ck, sk):
    t = proj.shape[0]
    nr = N_MLA_HEADS * D_ROPE
    return pl.pallas_call(
        _mla_prep_s_body,
        grid=(1,),
        in_specs=[
            pl.BlockSpec((t, Q_LORA), lambda i: (0, COL_QA // Q_LORA)),
            pl.BlockSpec((t, KV_LORA), lambda i: (0, COL_KVA // KV_LORA)),
            pl.BlockSpec((t, LANES), lambda i: (0, COL_MISC_A // LANES)),
            pl.BlockSpec((t, LANES), lambda i: (0, COL_MISC_B // LANES)),
            _const_spec((1, Q_LORA)), _const_spec((1, KV_LORA)),
            _const_spec((Q_LORA, D_MLA)), _const_spec((Q_LORA, nr)), _const_spec((Q_LORA, nr)),
            _const_spec((N_MLA_HEADS, D_NOPE, KV_LORA)),
            _const_spec((t, nr)), _const_spec((t, nr)), _const_spec((t, LANES)), _const_spec((t, LANES)),
        ],
        out_specs=[
            _const_spec((t, N_MLA_HEADS * KV_LORA)),
            _const_spec((t, nr)),
            _const_spec((t, KV_LORA)),
            _const_spec((t, LANES)),
        ],
        out_shape=[
            jax.ShapeDtypeStruct((t, N_MLA_HEADS * KV_LORA), BF16),
            jax.ShapeDtypeStruct((t, nr), BF16),
            jax.ShapeDtypeStruct((t, KV_LORA), F32),
            jax.ShapeDtypeStruct((t, LANES), F32),
        ],
        compiler_params=_cparams(("arbitrary",)),
        name="mla_prep_dec",
    )(proj, proj, proj, proj, qnw, kvnw, wqn, wqr, wqr2, wukt, cq, sq, ck, sk)


def _attn_s_body(ql_ref, qr_ref, cc_ref, ckr_ref, nc_ref, nkr_ref, wuv_ref, o_ref, m_ref, l_ref, acc_ref, *, lq):
    j = pl.program_id(1)
    nj = pl.num_programs(1)
    ql = ql_ref[0]
    qr = qr_ref[0]

    @pl.when(j == 0)
    def _():
        m_ref[...] = jnp.full(m_ref.shape, NEG_INF, F32)
        l_ref[...] = jnp.zeros(l_ref.shape, F32)
        acc_ref[...] = jnp.zeros(acc_ref.shape, F32)

    def update(c16, kr16):
        s = _dot_nt(ql, c16) + _dot_nt(qr, kr16)
        m_prev = m_ref[...]
        m_new = jnp.maximum(m_prev, jnp.max(s, axis=-1, keepdims=True))
        alpha = jnp.exp(m_prev - m_new)
        p = jnp.exp(s - m_new)
        l_ref[...] = alpha * l_ref[...] + jnp.sum(p, axis=-1, keepdims=True)
        acc_ref[...] = alpha * acc_ref[...] + _dot(p.astype(BF16), c16)
        m_ref[...] = m_new

    update(cc_ref[0].astype(BF16), ckr_ref[0].astype(BF16))

    @pl.when(j == nj - 1)
    def _():
        update(nc_ref[0].astype(BF16), nkr_ref[0].astype(BF16))
        olat = (acc_ref[...] / l_ref[...]).astype(BF16)
        for h in range(N_MLA_HEADS):
            o_ref[0, h * lq:(h + 1) * lq, :] = _dot(olat[h * lq:(h + 1) * lq, :], wuv_ref[h])


def _attn_s_call(qlat, qr, cache_ckv, cache_kr, new_ckv, new_kr, wuv, *, tk):
    b, rows, _ = qlat.shape
    lq = rows // N_MLA_HEADS
    past = cache_ckv.shape[1]
    body = functools.partial(_attn_s_body, lq=lq)
    return pl.pallas_call(
        body,
        grid=(b, past // tk),
        in_specs=[
            pl.BlockSpec((1, rows, KV_LORA), lambda i, j: (i, 0, 0)),
            pl.BlockSpec((1, rows, D_ROPE), lambda i, j: (i, 0, 0)),
            pl.BlockSpec((1, tk, KV_LORA), lambda i, j: (i, j, 0)),
            pl.BlockSpec((1, tk, D_ROPE), lambda i, j: (i, j, 0)),
            pl.BlockSpec((1, lq, KV_LORA), lambda i, j: (i, 0, 0)),
            pl.BlockSpec((1, lq, D_ROPE), lambda i, j: (i, 0, 0)),
            _const_spec((N_MLA_HEADS, KV_LORA, D_V)),
        ],
        out_specs=pl.BlockSpec((1, rows, D_V), lambda i, j: (i, 0, 0)),
        out_shape=jax.ShapeDtypeStruct((b, rows, D_V), F32),
        scratch_shapes=[
            pltpu.VMEM((rows, 1), F32),
            pltpu.VMEM((rows, 1), F32),
            pltpu.VMEM((rows, KV_LORA), F32),
        ],
        compiler_params=_cparams(("parallel", "arbitrary")),
        name="mla_attn_dec",
    )(qlat, qr, cache_ckv, cache_kr, new_ckv, new_kr, wuv)


def _mem_attn_body(q_ref, mk_ref, mv_ref, o_ref):
    scale = MEM_HEAD_DIM ** -0.5
    for h in range(MEM_HEADS):
        hs = slice(h * MEM_HEAD_DIM, (h + 1) * MEM_HEAD_DIM)
        s = _dot_nt(q_ref[:, hs].astype(BF16), mk_ref[0, :, hs].astype(BF16)) * scale
        p = jnp.exp(s - jnp.max(s, axis=-1, keepdims=True))
        l = jnp.sum(p, axis=-1, keepdims=True)
        o_ref[:, hs] = (_dot(p.astype(BF16), mv_ref[0, :, hs].astype(BF16)) / l).astype(BF16)


def _mem_attn_call(proj, mk, mv, *, batch, seq, tm):
    t = batch * seq
    nrb = seq // tm
    return pl.pallas_call(
        _mem_attn_body,
        grid=(batch, nrb),
        in_specs=[
            pl.BlockSpec((tm, D_MEM), lambda b, r: (b * nrb + r, COL_QMEM // D_MEM)),
            pl.BlockSpec((1, N_MEM, D_MEM), lambda b, r: (b, 0, 0)),
            pl.BlockSpec((1, N_MEM, D_MEM), lambda b, r: (b, 0, 0)),
        ],
        out_specs=pl.BlockSpec((tm, D_MEM), lambda b, r: (b * nrb + r, 0)),
        out_shape=jax.ShapeDtypeStruct((t, D_MEM), BF16),
        compiler_params=_cparams(("parallel", "parallel")),
        name="mem_attn",
    )(proj, mk, mv)


def _merge_body(y_ref, o_ref, g_ref, om_ref, gates_ref, x_ref, wos_ref, woa_ref, wom_ref, wout_ref,
                lng_ref, lnb_ref, x32_ref, x16_ref):
    ys = _dot(y_ref[...], wos_ref[...])
    ya = _dot((o_ref[...] * _silu(g_ref[...])).astype(BF16), woa_ref[...])
    ym = _dot(om_ref[...], wom_ref[...])
    gs = jax.nn.sigmoid(gates_ref[...])
    h = gs[:, 0:D_MODEL] * ys + gs[:, D_MODEL:2 * D_MODEL] * ya + gs[:, 2 * D_MODEL:3 * D_MODEL] * ym
    out = _dot(h.astype(BF16), wout_ref[...])
    xn = _layer_norm(ALPHA * x_ref[...] + out, lng_ref[...], lnb_ref[...])
    x32_ref[...] = xn
    x16_ref[...] = xn.astype(BF16)


def _merge_call(y, o, proj, om, x, wos, woa, wom, wout, lng, lnb, *, tm):
    t = x.shape[0]
    rowspec = lambda w: pl.BlockSpec((tm, w), lambda i: (i, 0))
    return pl.pallas_call(
        _merge_body,
        grid=(t // tm,),
        in_specs=[
            rowspec(D_SSM), rowspec(D_MLA),
            pl.BlockSpec((tm, D_MLA), lambda i: (i, COL_GMLA // D_MLA)),
            rowspec(D_MEM),
            pl.BlockSpec((tm, 3 * D_MODEL), lambda i: (i, COL_GATES // (3 * D_MODEL))),
            rowspec(D_MODEL),
            _const_spec((D_SSM, D_MODEL)), _const_spec((D_MLA, D_MODEL)),
            _const_spec((D_MEM, D_MODEL)), _const_spec((D_MODEL, D_MODEL)),
            _const_spec((1, D_MODEL)), _const_spec((1, D_MODEL)),
        ],
        out_specs=[rowspec(D_MODEL), rowspec(D_MODEL)],
        out_shape=[jax.ShapeDtypeStruct((t, D_MODEL), F32), jax.ShapeDtypeStruct((t, D_MODEL), BF16)],
        compiler_params=_cparams(("parallel",)),
        name="merge_out",
    )(y, o, proj, om, proj, x, wos, woa, wom, wout, lng, lnb)


def _rot_cols(w):
    half = D_ROPE // 2
    return jnp.concatenate([-w[..., half:], w[..., :half]], axis=-1)


def _pack_layer(p, i):
    w_in = p['w_in'][i]
    seg = {}
    off = 0
    for name, width in (('z', D_SSM), ('xbc', CONV_DIM), ('dt', N_SSM_HEADS), ('qa', Q_LORA), ('kva', KV_LORA),
                        ('kr', D_ROPE), ('gmla', D_MLA), ('qmem', D_MEM), ('gates', 3 * D_MODEL)):
        seg[name] = w_in[:, off:off + width]
        off += width
    zc = lambda n: jnp.zeros((D_MODEL, n), F32)
    w_proj = jnp.concatenate([
        seg['xbc'], seg['gates'], seg['z'], seg['gmla'], seg['qmem'], seg['qa'], seg['kva'],
        seg['dt'], zc(32), seg['kr'], zc(32),
        zc(64), _rot_cols(seg['kr']), zc(32),
    ], axis=1).astype(BF16)

    wqb = p['w_q_b'][i].reshape(Q_LORA, N_MLA_HEADS, D_NOPE + D_ROPE)
    wq_n = wqb[:, :, :D_NOPE]
    wq_r = wqb[:, :, D_NOPE:]
    zq = jnp.zeros((Q_LORA, N_MLA_HEADS, HEAD_PAD - D_NOPE - D_ROPE), F32)
    wq1 = jnp.concatenate([wq_n, wq_r, zq], axis=-1).reshape(Q_LORA, -1).astype(BF16)
    wq2 = jnp.concatenate([jnp.zeros_like(wq_n), _rot_cols(wq_r), zq], axis=-1).reshape(Q_LORA, -1).astype(BF16)
    wuk = p['w_uk'][i]
    wuv = p['w_uv'][i]
    zk = jnp.zeros((KV_LORA, N_MLA_HEADS, HEAD_PAD - D_NOPE), F32)
    wk = jnp.concatenate([wuk, zk], axis=-1).reshape(KV_LORA, -1).astype(BF16)
    wv = jnp.concatenate([wuv, zk], axis=-1).reshape(KV_LORA, -1).astype(BF16)

    pad128 = lambda v: jnp.pad(v, (0, LANES - v.shape[0])).reshape(1, LANES)
    return dict(
        w_proj=w_proj,
        cw=p['conv_w'][i], cb=p['conv_b'][i].reshape(1, CONV_DIM),
        dtb=pad128(p['dt_bias'][i]), alog=pad128(p['a_log'][i]),
        dskip=jnp.repeat(p['d_skip'][i], SSM_HEAD_DIM).reshape(1, D_SSM),
        nw=p['ssm_norm_w'][i].reshape(1, D_SSM),
        qnw=p['q_a_norm_w'][i].reshape(1, Q_LORA), kvnw=p['kv_norm_w'][i].reshape(1, KV_LORA),
        wq1=wq1, wq2=wq2, wk=wk, wv=wv,
        wqn=wq_n.reshape(Q_LORA, -1).astype(BF16),
        wqr=wq_r.reshape(Q_LORA, -1).astype(BF16),
        wqr2=_rot_cols(wq_r).reshape(Q_LORA, -1).astype(BF16),
        wukt=jnp.transpose(wuk, (1, 2, 0)).astype(BF16),
        wuv=jnp.transpose(wuv, (1, 0, 2)).astype(BF16),
        wmk=p['w_mem_k'][i].astype(BF16), wmv=p['w_mem_v'][i].astype(BF16),
        wos=p['w_o_ssm'][i].astype(BF16), woa=p['w_o_mla'][i].astype(BF16),
        wom=p['w_o_mem'][i].astype(BF16), wout=p['w_out'][i].astype(BF16),
        lng=p['ln_g'][i].reshape(1, D_MODEL), lnb=p['ln_b'][i].reshape(1, D_MODEL),
    )


def _rope_tables(pos):
    half = D_ROPE // 2
    inv = ROPE_THETA ** (-jnp.arange(half, dtype=F32) / half)
    ang = pos.astype(F32)[:, None] * inv[None, :]
    cos = jnp.cos(ang)
    sin = jnp.sin(ang)
    return jnp.concatenate([cos, cos], axis=-1), jnp.concatenate([sin, sin], axis=-1)


def _expansion_matrix():
    e = np.zeros((LANES, D_SSM), np.float32)
    for h in range(N_SSM_HEADS):
        e[h, h * SSM_HEAD_DIM:(h + 1) * SSM_HEAD_DIM] = 1.0
    return jnp.asarray(e, dtype=BF16)


def _ssm_branch(proj, lw, conv_prev, ssm_prev, emat, *, batch, seq, rb, tc):
    tail0 = jnp.pad(conv_prev, ((0, 0), (8 - (CONV_W - 1), 0), (0, 0)))
    s0 = jnp.transpose(ssm_prev, (0, 3, 1, 2)).reshape(batch, D_STATE, D_SSM)
    y, sfin, tail = _ssm_call(proj, tail0, s0, lw['cw'], lw['cb'], lw['dtb'], lw['alog'], lw['dskip'], lw['nw'],
                              emat, batch=batch, seq=seq, rb=rb, tc=tc)
    ssm_new = jnp.transpose(sfin.reshape(batch, D_STATE, N_SSM_HEADS, SSM_HEAD_DIM), (0, 2, 3, 1))
    return y, ssm_new, tail[:, 8 - (CONV_W - 1):, :]


def kernel(x_prompt, x_sample, mem_prompt, state_ssm, state_conv, cache_ckv, cache_krope, cache_mem_k, cache_mem_v,
           ln_in_g, ln_in_b, w_in, conv_w, conv_b, dt_bias, a_log, d_skip, ssm_norm_w, w_o_ssm, q_a_norm_w, w_q_b,
           kv_norm_w, w_uk, w_uv, w_o_mla, w_mem_k, w_mem_v, w_o_mem, w_out, ln_g, ln_b):
    params = dict(w_in=w_in, conv_w=conv_w, conv_b=conv_b, dt_bias=dt_bias, a_log=a_log, d_skip=d_skip,
                  ssm_norm_w=ssm_norm_w, w_o_ssm=w_o_ssm, q_a_norm_w=q_a_norm_w, w_q_b=w_q_b, kv_norm_w=kv_norm_w,
                  w_uk=w_uk, w_uv=w_uv, w_o_mla=w_o_mla, w_mem_k=w_mem_k, w_mem_v=w_mem_v, w_o_mem=w_o_mem,
                  w_out=w_out, ln_g=ln_g, ln_b=ln_b)
    bp, lp, _ = x_prompt.shape
    bs, ls, _ = x_sample.shape
    past = cache_ckv.shape[2]
    tp = bp * lp
    ts = bs * ls
    assert (past // CHUNK + 1) * CHUNK >= past + ls, "decode block would need a causal mask"

    emat = _expansion_matrix()
    tm_p = min(512, tp)
    xp32, xp16 = _ln_call(x_prompt.reshape(tp, D_MODEL), ln_in_g, ln_in_b, tm_p)
    xs32, xs16 = _ln_call(x_sample.reshape(ts, D_MODEL), ln_in_g, ln_in_b, ts)
    mem16 = mem_prompt.reshape(bp * N_MEM, D_MODEL).astype(BF16)

    cos_p, sin_p = _rope_tables(jnp.arange(lp, dtype=jnp.int32))
    cos_s, sin_s = _rope_tables(past + jnp.arange(ls, dtype=jnp.int32))
    z64 = lambda c: jnp.zeros((c.shape[0], 64), F32)
    z32 = lambda c: jnp.zeros((c.shape[0], 32), F32)
    one64 = jnp.ones((lp, 64), F32)
    cq_p = jnp.concatenate([one64, cos_p, z32(cos_p)], axis=1) * (MLA_SCALE * LOG2E)
    sq_p = jnp.concatenate([z64(sin_p), sin_p, z32(sin_p)], axis=1) * (MLA_SCALE * LOG2E)
    ck_p = jnp.concatenate([z64(cos_p), cos_p, z32(cos_p)], axis=1)
    sk_p = jnp.concatenate([z64(sin_p), sin_p, z32(sin_p)], axis=1)
    cq_s = jnp.tile(jnp.tile(cos_s, (1, N_MLA_HEADS)), (bs, 1)) * MLA_SCALE
    sq_s = jnp.tile(jnp.tile(sin_s, (1, N_MLA_HEADS)), (bs, 1)) * MLA_SCALE
    ck_s = jnp.tile(jnp.concatenate([z64(cos_s), cos_s, z32(cos_s)], axis=1), (bs, 1))
    sk_s = jnp.tile(jnp.concatenate([z64(sin_s), sin_s, z32(sin_s)], axis=1), (bs, 1))

    conv0 = jnp.zeros((bp, CONV_W - 1, CONV_DIM), F32)
    ssm0 = jnp.zeros((bp, N_SSM_HEADS, SSM_HEAD_DIM, D_STATE), F32)

    rb_p = min(256, lp)
    tc_p = min(128, lp)
    tq = min(512, lp)
    tmm = min(1024, tp)

    outs = {k: [] for k in ('p_ssm', 'p_conv', 'p_ckv', 'p_kr', 'p_mk', 'p_mv', 's_ssm', 's_conv', 's_ckv', 's_kr')}
    for i in range(DEPTH):
        lw = _pack_layer(params, i)

        proj = _mm_call(xp16, lw['w_proj'], tmm, 1024, "in_proj")
        y, ssm_new, conv_new = _ssm_branch(proj, lw, conv0, ssm0, emat, batch=bp, seq=lp, rb=rb_p, tc=tc_p)
        q, k, v, ckv, kr = _mla_prep_call(proj, lw['qnw'], lw['kvnw'], lw['wq1'], lw['wq2'], lw['wk'], lw['wv'],
                                          cq_p, sq_p, ck_p, sk_p, seq=lp, tm=tq)
        o = _attn_call(q, k, v, batch=bp, seq=lp, tq=tq)
        mk = _mm_call(mem16, lw['wmk'], bp * N_MEM, 1024, "mem_k_proj")
        mv = _mm_call(mem16, lw['wmv'], bp * N_MEM, 1024, "mem_v_proj")
        om = _mem_attn_call(proj, mk.reshape(bp, N_MEM, D_MEM), mv.reshape(bp, N_MEM, D_MEM), batch=bp, seq=lp, tm=tq)
        xp32, xp16 = _merge_call(y, o, proj, om, xp32, lw['wos'], lw['woa'], lw['wom'], lw['wout'],
                                 lw['lng'], lw['lnb'], tm=min(256, tp))
        outs['p_ssm'].append(ssm_new)
        outs['p_conv'].append(conv_new)
        outs['p_ckv'].append(ckv.reshape(bp, lp, KV_LORA))
        outs['p_kr'].append(kr[:, 64:64 + D_ROPE].reshape(bp, lp, D_ROPE))
        outs['p_mk'].append(mk.reshape(bp, N_MEM, MEM_HEADS, MEM_HEAD_DIM))
        outs['p_mv'].append(mv.reshape(bp, N_MEM, MEM_HEADS, MEM_HEAD_DIM))

        proj_s = _mm_call(xs16, lw['w_proj'], ts, 1024, "in_proj_dec")
        y_s, ssm_new_s, conv_new_s = _ssm_branch(proj_s, lw, state_conv[i], state_ssm[i], emat,
                                                 batch=bs, seq=ls, rb=ls, tc=ls)
        qlat, qr, ckv_s, kr_s = _mla_prep_s_call(proj_s, lw['qnw'], lw['kvnw'], lw['wqn'], lw['wqr'], lw['wqr2'],
                                                 lw['wukt'], cq_s, sq_s, ck_s, sk_s)
        kr_s = kr_s[:, 64:64 + D_ROPE]
        to_hq = lambda a, w: jnp.transpose(a.reshape(bs, ls, N_MLA_HEADS, w), (0, 2, 1, 3)).reshape(bs, N_MLA_HEADS * ls, w)
        o_s = _attn_s_call(to_hq(qlat, KV_LORA), to_hq(qr, D_ROPE), cache_ckv[i], cache_krope[i],
                           ckv_s.reshape(bs, ls, KV_LORA), kr_s.reshape(bs, ls, D_ROPE), lw['wuv'], tk=min(1024, past))
        o_s = jnp.transpose(o_s.reshape(bs, N_MLA_HEADS, ls, D_V), (0, 2, 1, 3)).reshape(ts, D_MLA)
        om_s = _mem_attn_call(proj_s, cache_mem_k[i].reshape(bs, N_MEM, D_MEM), cache_mem_v[i].reshape(bs, N_MEM, D_MEM),
                              batch=bs, seq=ls, tm=ls)
        xs32, xs16 = _merge_call(y_s, o_s, proj_s, om_s, xs32, lw['wos'], lw['woa'], lw['wom'], lw['wout'],
                                 lw['lng'], lw['lnb'], tm=min(256, ts))
        outs['s_ssm'].append(ssm_new_s)
        outs['s_conv'].append(conv_new_s)
        outs['s_ckv'].append(ckv_s.reshape(bs, ls, KV_LORA))
        outs['s_kr'].append(kr_s.reshape(bs, ls, D_ROPE))

    st = lambda k: jnp.stack(outs[k])
    return (xp32.reshape(bp, lp, D_MODEL), xs32.reshape(bs, ls, D_MODEL), st('p_ssm'), st('p_conv'), st('p_ckv'),
            st('p_kr'), st('p_mk'), st('p_mv'), st('s_ssm'), st('s_conv'), st('s_ckv'), st('s_kr'))
```

```python
import functools
import math

import numpy as np
import jax
import jax.numpy as jnp
from jax import lax
from jax.experimental import pallas as pl
from jax.experimental.pallas import tpu as pltpu

F32 = jnp.float32
BF16 = jnp.bfloat16

D_MODEL = 1024
DEPTH = 4
CHUNK = 64
D_SSM = 2048
SSM_HEAD_DIM = 64
N_SSM_HEADS = 32
SSM_GROUPS = 4
HEADS_PER_GROUP = N_SSM_HEADS // SSM_GROUPS
D_STATE = 128
CONV_W = 4
CONV_DIM = D_SSM + 2 * SSM_GROUPS * D_STATE
N_MLA_HEADS = 16
D_NOPE = 64
D_ROPE = 32
D_V = 64
Q_LORA = 512
KV_LORA = 256
D_MLA = N_MLA_HEADS * D_V
MLA_SCALE = (D_NOPE + D_ROPE) ** -0.5
ROPE_THETA = 10000.0
N_MEM = 256
MEM_HEADS = 4
MEM_HEAD_DIM = 256
D_MEM = MEM_HEADS * MEM_HEAD_DIM
ALPHA = (2 * DEPTH) ** 0.25
EPS = 1e-5
NEG_INF = -1e30
LOG2E = math.log2(math.e)

LANES = 128
HEAD_PAD = 128
VMEM_LIMIT = 56 * 1024 * 1024

COL_XBC = 0
COL_GATES = 3072
COL_Z = 6144
COL_GMLA = 8192
COL_QMEM = 9216
COL_QA = 10240
COL_KVA = 10752
COL_MISC_A = 11008
COL_MISC_B = 11136
N_PROJ = 11264


def _cparams(sem):
    return pltpu.CompilerParams(dimension_semantics=sem, vmem_limit_bytes=VMEM_LIMIT)


def _const_spec(shape):
    nd = len(shape)
    return pl.BlockSpec(shape, lambda *_: (0,) * nd)


def _layer_spec(shape, layer):
    nd = len(shape)
    return pl.BlockSpec((None,) + tuple(shape), lambda *_: (layer,) + (0,) * nd, pipeline_mode=pl.Buffered(1))


def _silu(x):
    return x * jax.nn.sigmoid(x)


def _softplus(x):
    return jnp.maximum(x, 0.0) + jnp.log1p(jnp.exp(-jnp.abs(x)))


def _dot(a, b):
    return jnp.dot(a, b, preferred_element_type=F32)


def _dot_nt(a, b):
    return lax.dot_general(a, b, (((1,), (1,)), ((), ())), preferred_element_type=F32)


def _layer_norm(x, g, b):
    mu = jnp.mean(x, axis=-1, keepdims=True)
    xc = x - mu
    var = jnp.mean(xc * xc, axis=-1, keepdims=True)
    return xc * lax.rsqrt(var + EPS) * g + b


def _rms_norm(x, w):
    return x * lax.rsqrt(jnp.mean(x * x, axis=-1, keepdims=True) + EPS) * w


def _ln_body(x_ref, g_ref, b_ref, o32_ref, o16_ref):
    y = _layer_norm(x_ref[...], g_ref[...], b_ref[...])
    o32_ref[...] = y
    o16_ref[...] = y.astype(BF16)


def _ln_call(x, g, b, tm):
    t, d = x.shape
    assert t % tm == 0, (t, tm)
    return pl.pallas_call(
        _ln_body,
        grid=(t // tm,),
        in_specs=[pl.BlockSpec((tm, d), lambda i: (i, 0)), _const_spec((1, d)), _const_spec((1, d))],
        out_specs=[pl.BlockSpec((tm, d), lambda i: (i, 0)), pl.BlockSpec((tm, d), lambda i: (i, 0))],
        out_shape=[jax.ShapeDtypeStruct((t, d), F32), jax.ShapeDtypeStruct((t, d), BF16)],
        compiler_params=_cparams(("parallel",)),
        name="ln_in",
    )(x, g.reshape(1, d), b.reshape(1, d))


def _mm_body(x_ref, w_ref, o_ref):
    o_ref[...] = _dot(x_ref[...], w_ref[...]).astype(o_ref.dtype)


def _mm_call(x, w, layer, tm, tn, name):
    m, k = x.shape
    n = w.shape[2]
    assert m % tm == 0 and n % tn == 0, (m, n, tm, tn)
    return pl.pallas_call(
        _mm_body,
        grid=(m // tm, n // tn),
        in_specs=[pl.BlockSpec((tm, k), lambda i, j: (i, 0)),
                  pl.BlockSpec((None, k, tn), lambda i, j: (layer, 0, j))],
        out_specs=pl.BlockSpec((tm, tn), lambda i, j: (i, j)),
        out_shape=jax.ShapeDtypeStruct((m, n), F32),
        compiler_params=_cparams(("parallel", "parallel")),
        name=name,
    )(x, w)


def _cumsum_rows(x, t):
    row = lax.broadcasted_iota(jnp.int32, x.shape, 0)
    s = x
    k = 1
    while k < t:
        s = s + jnp.where(row >= k, pltpu.roll(s, k, 0), 0.0)
        k *= 2
    return s


def _expand_heads(w, e_ref):
    hi = w.astype(BF16)
    lo = (w - hi.astype(F32)).astype(BF16)
    e = e_ref[...]
    return _dot(hi, e) + _dot(lo, e)


def _ssm_body(*refs, rb, tc, has_prev):
    if has_prev:
        (xbc_ref, z_ref, misc_ref, tail0_ref, s0_ref, cw_ref, cb_ref, dtb_ref, alog_ref, dskip_ref, nw_ref, e_ref,
         y_ref, sfin_ref, tail_ref, cbuf, ybuf, dabuf, dtbuf, state, tail) = refs
    else:
        (xbc_ref, z_ref, misc_ref, cw_ref, cb_ref, dtb_ref, alog_ref, dskip_ref, nw_ref, e_ref,
         y_ref, sfin_ref, tail_ref, cbuf, ybuf, dabuf, dtbuf, state, tail) = refs
    r = pl.program_id(1)
    nr = pl.num_programs(1)
    pair_w = 2 * SSM_HEAD_DIM

    @pl.when(r == 0)
    def _():
        if has_prev:
            for hp in range(N_SSM_HEADS // 2):
                blk = s0_ref[0, 2 * hp:2 * hp + 2].reshape(pair_w, D_STATE)
                state[:, hp * pair_w:(hp + 1) * pair_w] = blk.T
            tail[...] = tail0_ref[0]
        else:
            state[...] = jnp.zeros(state.shape, F32)
            tail[...] = jnp.zeros(tail.shape, F32)

    x = xbc_ref[...]
    cw = cw_ref[...]
    cb = cb_ref[...]
    acc = cb + cw[3:4, :] * x
    for j in range(1, CONV_W):
        acc = acc + cw[3 - j:4 - j, :] * pltpu.roll(x, j, 0)
    cbuf[...] = _silu(acc)
    x8 = x[0:8, :]
    t8 = tail[...]
    row8 = lax.broadcasted_iota(jnp.int32, x8.shape, 0)
    acc8 = cb + cw[3:4, :] * x8
    for j in range(1, CONV_W):
        xj = jnp.where(row8 < j, pltpu.roll(t8, j, 0), pltpu.roll(x8, j, 0))
        acc8 = acc8 + cw[3 - j:4 - j, :] * xj
    cbuf[0:8, :] = _silu(acc8)
    tail[...] = x[rb - 8:rb, :]

    dt = _softplus(misc_ref[...] + dtb_ref[...])
    dtbuf[...] = dt
    dabuf[...] = dt * (-jnp.exp(alog_ref[...]))

    ri = lax.broadcasted_iota(jnp.int32, (tc, tc), 0)
    ci = lax.broadcasted_iota(jnp.int32, (tc, tc), 1)
    lower = ri >= ci
    pad_rows = LANES - tc

    def to_rows(a):
        if pad_rows:
            a = jnp.concatenate([a, jnp.zeros((pad_rows, LANES), F32)], axis=0)
        return a.T[:, :tc]

    def chunk(c, carry):
        r0 = pl.multiple_of(c * tc, tc)
        rows = pl.ds(r0, tc)
        da = dabuf[rows, :]
        dtc = dtbuf[rows, :]
        acs = _cumsum_rows(da, tc)
        acs_t = to_rows(acs)
        dt_t = to_rows(dtc)
        total = acs[tc - 1:tc, :]
        wexp = _expand_heads(jnp.exp(total - acs) * dtc, e_ref)
        etot = _expand_heads(jnp.broadcast_to(jnp.exp(total), (8, LANES)), e_ref)[0:1, :]
        xs = cbuf[rows, 0:D_SSM]
        xs16 = xs.astype(BF16)
        xw16 = (xs * wexp).astype(BF16)
        for g in range(SSM_GROUPS):
            bg = cbuf[rows, D_SSM + g * D_STATE:D_SSM + (g + 1) * D_STATE]
            cg = cbuf[rows, D_SSM + SSM_GROUPS * D_STATE + g * D_STATE:D_SSM + SSM_GROUPS * D_STATE + (g + 1) * D_STATE]
            bg16 = bg.astype(BF16)
            cg16 = cg.astype(BF16)
            cbm = _dot_nt(cg16, bg16)
            gcols = slice(g * HEADS_PER_GROUP * SSM_HEAD_DIM, (g + 1) * HEADS_PER_GROUP * SSM_HEAD_DIM)
            s_old = state[:, gcols]
            s16 = s_old.astype(BF16)
            for hh in range(0, HEADS_PER_GROUP, 2):
                pair = []
                for h in (g * HEADS_PER_GROUP + hh, g * HEADS_PER_GROUP + hh + 1):
                    colb = jnp.broadcast_to(acs[:, h:h + 1], (tc, LANES))
                    rowb = jnp.broadcast_to(acs_t[h:h + 1, :], (tc, tc))
                    dtb = jnp.broadcast_to(dt_t[h:h + 1, :], (tc, tc))
                    lmat = jnp.where(lower, jnp.exp(colb[:, :tc] - rowb), 0.0)
                    m16 = (cbm * lmat * dtb).astype(BF16)
                    cexp16 = (cg * jnp.exp(colb)).astype(BF16)
                    hl = h - g * HEADS_PER_GROUP
                    xh = xs16[:, h * SSM_HEAD_DIM:(h + 1) * SSM_HEAD_DIM]
                    sh = s16[:, hl * SSM_HEAD_DIM:(hl + 1) * SSM_HEAD_DIM]
                    if tc % LANES == 0:
                        yh = _dot(jnp.concatenate([m16, cexp16], axis=1), jnp.concatenate([xh, sh], axis=0))
                    else:
                        yh = _dot(m16, xh) + _dot(cexp16, sh)
                    pair.append(yh)
                h0 = g * HEADS_PER_GROUP + hh
                ybuf[rows, h0 * SSM_HEAD_DIM:(h0 + 2) * SSM_HEAD_DIM] = jnp.concatenate(pair, axis=1)
            bg_t = to_rows(bg).astype(BF16)
            state[:, gcols] = s_old * etot[:, gcols] + _dot(bg_t, xw16[:, gcols])
        y = ybuf[rows, :] + dskip_ref[...] * xs
        zc = z_ref[rows, :]
        gt = y * _silu(zc)
        gw = D_SSM // SSM_GROUPS
        for g in range(SSM_GROUPS):
            gg = gt[:, g * gw:(g + 1) * gw]
            ms = jnp.mean(gg * gg, axis=-1, keepdims=True)
            y_ref[rows, g * gw:(g + 1) * gw] = (gg * lax.rsqrt(ms + EPS) * nw_ref[:, g * gw:(g + 1) * gw]).astype(BF16)
        return carry

    lax.fori_loop(0, rb // tc, chunk, 0)

    @pl.when(r == nr - 1)
    def _():
        for hp in range(N_SSM_HEADS // 2):
            blk = state[:, hp * pair_w:(hp + 1) * pair_w].T
            sfin_ref[0, 2 * hp:2 * hp + 2] = blk.reshape(2, SSM_HEAD_DIM, D_STATE)
        tail_ref[0] = tail[...]


def _ssm_call(proj, prev, w, emat, layer, *, batch, seq, rb, tc):
    t = batch * seq
    assert seq % rb == 0 and rb % tc == 0 and rb % 8 == 0, (seq, rb, tc)
    nrb = seq // rb
    row = lambda b, r: b * nrb + r
    has_prev = prev is not None
    body = functools.partial(_ssm_body, rb=rb, tc=tc, has_prev=has_prev)
    state_block = (1, N_SSM_HEADS, SSM_HEAD_DIM, D_STATE)
    prev_specs = []
    prev_args = []
    if has_prev:
        prev_specs = [pl.BlockSpec((1, 8, CONV_DIM), lambda b, r: (b, 0, 0)),
                      pl.BlockSpec((None,) + state_block, lambda b, r: (layer, b, 0, 0, 0))]
        prev_args = list(prev)
    return pl.pallas_call(
        body,
        grid=(batch, nrb),
        in_specs=[
            pl.BlockSpec((rb, CONV_DIM), lambda b, r: (row(b, r), COL_XBC // CONV_DIM)),
            pl.BlockSpec((rb, D_SSM), lambda b, r: (row(b, r), COL_Z // D_SSM)),
            pl.BlockSpec((rb, LANES), lambda b, r: (row(b, r), COL_MISC_A // LANES)),
            *prev_specs,
            _layer_spec((CONV_W, CONV_DIM), layer),
            _layer_spec((1, CONV_DIM), layer),
            _layer_spec((1, LANES), layer),
            _layer_spec((1, LANES), layer),
            _layer_spec((1, D_SSM), layer),
            _layer_spec((1, D_SSM), layer),
            _const_spec((LANES, D_SSM)),
        ],
        out_specs=[
            pl.BlockSpec((rb, D_SSM), lambda b, r: (row(b, r), 0)),
            pl.BlockSpec(state_block, lambda b, r: (b, 0, 0, 0)),
            pl.BlockSpec((1, 8, CONV_DIM), lambda b, r: (b, 0, 0)),
        ],
        out_shape=[
            jax.ShapeDtypeStruct((t, D_SSM), BF16),
            jax.ShapeDtypeStruct((batch,) + state_block[1:], F32),
            jax.ShapeDtypeStruct((batch, 8, CONV_DIM), F32),
        ],
        scratch_shapes=[
            pltpu.VMEM((rb, CONV_DIM), F32),
            pltpu.VMEM((rb, D_SSM), F32),
            pltpu.VMEM((rb, LANES), F32),
            pltpu.VMEM((rb, LANES), F32),
            pltpu.VMEM((D_STATE, D_SSM), F32),
            pltpu.VMEM((8, CONV_DIM), F32),
        ],
        compiler_params=_cparams(("parallel", "arbitrary")),
        name="ssd_mixer",
    )(proj, proj, proj, *prev_args, w['cw'], w['cb'], w['dtb'], w['alog'], w['dskip'], w['nw'], emat)


def _mla_prep_body(qa_ref, kva_ref, ma_ref, mb_ref, qnw_ref, kvnw_ref, wq1_ref, wq2_ref, wk_ref, wv_ref,
                   cq_ref, sq_ref, ck_ref, sk_ref, q_ref, k_ref, v_ref, ckv_ref, kr_ref):
    qa16 = _rms_norm(qa_ref[...], qnw_ref[...]).astype(BF16)
    q1 = _dot(qa16, wq1_ref[...])
    q2 = _dot(qa16, wq2_ref[...])
    cq = jnp.tile(cq_ref[...], (1, N_MLA_HEADS))
    sq = jnp.tile(sq_ref[...], (1, N_MLA_HEADS))
    q_ref[...] = (q1 * cq + q2 * sq).astype(BF16)
    ckv = _rms_norm(kva_ref[...], kvnw_ref[...])
    ckv_ref[...] = ckv
    ckv16 = ckv.astype(BF16)
    kr = ma_ref[...] * ck_ref[...] + mb_ref[...] * sk_ref[...]
    kr_ref[...] = kr
    k_ref[...] = (_dot(ckv16, wk_ref[...]) + jnp.tile(kr, (1, N_MLA_HEADS))).astype(BF16)
    lane = lax.broadcasted_iota(jnp.int32, (1, N_MLA_HEADS * HEAD_PAD), 1)
    ones_col = jnp.where(lane % HEAD_PAD == D_V, 1.0, 0.0)
    v_ref[...] = (_dot(ckv16, wv_ref[...]) + ones_col).astype(BF16)


def _mla_prep_call(proj, w, cq, sq, ck, sk, layer, *, seq, tm):
    t = proj.shape[0]
    assert seq % tm == 0 and t % seq == 0, (t, seq, tm)
    npos = seq // tm
    hp = N_MLA_HEADS * HEAD_PAD
    tab = pl.BlockSpec((tm, LANES), lambda i: (i % npos, 0))
    return pl.pallas_call(
        _mla_prep_body,
        grid=(t // tm,),
        in_specs=[
            pl.BlockSpec((tm, Q_LORA), lambda i: (i, COL_QA // Q_LORA)),
            pl.BlockSpec((tm, KV_LORA), lambda i: (i, COL_KVA // KV_LORA)),
            pl.BlockSpec((tm, LANES), lambda i: (i, COL_MISC_A // LANES)),
            pl.BlockSpec((tm, LANES), lambda i: (i, COL_MISC_B // LANES)),
            _layer_spec((1, Q_LORA), layer), _layer_spec((1, KV_LORA), layer),
            _layer_spec((Q_LORA, hp), layer), _layer_spec((Q_LORA, hp), layer),
            _layer_spec((KV_LORA, hp), layer), _layer_spec((KV_LORA, hp), layer),
            tab, tab, tab, tab,
        ],
        out_specs=[
            pl.BlockSpec((tm, hp), lambda i: (i, 0)),
            pl.BlockSpec((tm, hp), lambda i: (i, 0)),
            pl.BlockSpec((tm, hp), lambda i: (i, 0)),
            pl.BlockSpec((tm, KV_LORA), lambda i: (i, 0)),
            pl.BlockSpec((tm, LANES), lambda i: (i, 0)),
        ],
        out_shape=[
            jax.ShapeDtypeStruct((t, hp), BF16),
            jax.ShapeDtypeStruct((t, hp), BF16),
            jax.ShapeDtypeStruct((t, hp), BF16),
            jax.ShapeDtypeStruct((t, KV_LORA), F32),
            jax.ShapeDtypeStruct((t, LANES), F32),
        ],
        compiler_params=_cparams(("parallel",)),
        name="mla_prep",
    )(proj, proj, proj, proj, w['qnw'], w['kvnw'], w['wq1'], w['wq2'], w['wk'], w['wv'], cq, sq, ck, sk)


def _attn_body(qi_ref, kj_ref, q_ref, k_ref, v_ref, o_ref, m_ref, acc_ref, *, tq, tk):
    t = pl.program_id(1)
    i = qi_ref[t]
    j = kj_ref[t]

    @pl.when(j == 0)
    def _():
        m_ref[...] = jnp.full(m_ref.shape, NEG_INF, F32)
        acc_ref[...] = jnp.zeros(acc_ref.shape, F32)

    def step(visible):
        for h in range(N_MLA_HEADS):
            hs = slice(h * HEAD_PAD, (h + 1) * HEAD_PAD)
            s = _dot_nt(q_ref[0, :, hs], k_ref[0, :, hs])
            if visible is not None:
                s = jnp.where(visible, s, NEG_INF)
            m_prev = m_ref[h]
            m_new = jnp.maximum(m_prev, jnp.max(s, axis=-1, keepdims=True))
            p = jnp.exp2(s - jnp.tile(m_new, (1, tk // LANES)))
            acc_ref[h] = jnp.exp2(m_prev - m_new) * acc_ref[h] + _dot(p.astype(BF16), v_ref[0, :, hs])
            m_ref[h] = m_new

    @pl.when(j < i)
    def _():
        step(None)

    @pl.when(j == i)
    def _():
        row = lax.broadcasted_iota(jnp.int32, (tq, tk), 0)
        col = lax.broadcasted_iota(jnp.int32, (tq, tk), 1)
        step(col < ((row // CHUNK + 1) * CHUNK))
        for h in range(0, N_MLA_HEADS, 2):
            a = acc_ref[h]
            b = acc_ref[h + 1]
            a = a[:, :D_V] / a[:, D_V:D_V + 1]
            b = b[:, :D_V] / b[:, D_V:D_V + 1]
            o_ref[0, :, h * D_V:(h + 2) * D_V] = jnp.concatenate([a, b], axis=1)


def _attn_call(q, k, v, *, batch, seq, tq):
    hp = N_MLA_HEADS * HEAD_PAD
    q = q.reshape(batch, seq, hp)
    k = k.reshape(batch, seq, hp)
    v = v.reshape(batch, seq, hp)
    assert seq % tq == 0 and tq % CHUNK == 0, (seq, tq)
    nq = seq // tq
    pairs = [(i, j) for i in range(nq) for j in range(i + 1)]
    qi = jnp.asarray([p[0] for p in pairs], jnp.int32)
    kj = jnp.asarray([p[1] for p in pairs], jnp.int32)
    kv_spec = pl.BlockSpec((1, tq, hp), lambda b, t, qi, kj: (b, kj[t], 0))
    body = functools.partial(_attn_body, tq=tq, tk=tq)
    out = pl.pallas_call(
        body,
        grid_spec=pltpu.PrefetchScalarGridSpec(
            num_scalar_prefetch=2,
            grid=(batch, len(pairs)),
            in_specs=[pl.BlockSpec((1, tq, hp), lambda b, t, qi, kj: (b, qi[t], 0)), kv_spec, kv_spec],
            out_specs=pl.BlockSpec((1, tq, D_MLA), lambda b, t, qi, kj: (b, qi[t], 0)),
            scratch_shapes=[
                pltpu.VMEM((N_MLA_HEADS, tq, LANES), F32),
                pltpu.VMEM((N_MLA_HEADS, tq, LANES), F32),
            ],
        ),
        out_shape=jax.ShapeDtypeStruct((batch, seq, D_MLA), F32),
        compiler_params=_cparams(("parallel", "arbitrary")),
        name="mla_attn",
    )(qi, kj, q, k, v)
    return out.reshape(batch * seq, D_MLA)


def _mla_prep_s_body(qa_ref, kva_ref, ma_ref, mb_ref, qnw_ref, kvnw_ref, wqn_ref, wqr_ref, wqr2_ref, wukt_ref,
                     cq_ref, sq_ref, ck_ref, sk_ref, qlat_ref, qr_ref, ckv_ref, kr_ref):
    qa16 = _rms_norm(qa_ref[...], qnw_ref[...]).astype(BF16)
    qn16 = _dot(qa16, wqn_ref[...]).astype(BF16)
    for h in range(N_MLA_HEADS):
        qh = qn16[:, h * D_NOPE:(h + 1) * D_NOPE]
        qlat_ref[:, h * KV_LORA:(h + 1) * KV_LORA] = (_dot(qh, wukt_ref[h]) * MLA_SCALE).astype(BF16)
    qr = _dot(qa16, wqr_ref[...]) * cq_ref[...] + _dot(qa16, wqr2_ref[...]) * sq_ref[...]
    qr_ref[...] = qr.astype(BF16)
    ckv_ref[...] = _rms_norm(kva_ref[...], kvnw_ref[...])
    kr_ref[...] = ma_ref[...] * ck_ref[...] + mb_ref[...] * sk_ref[...]


def _mla_prep_s_call(proj, w, cq, sq, ck, sk, layer):
    t = proj.shape[0]
    nr = N_MLA_HEADS * D_ROPE
    return pl.pallas_call(
        _mla_prep_s_body,
        grid=(1,),
        in_specs=[
            pl.BlockSpec((t, Q_LORA), lambda i: (0, COL_QA // Q_LORA)),
            pl.BlockSpec((t, KV_LORA), lambda i: (0, COL_KVA // KV_LORA)),
            pl.BlockSpec((t, LANES), lambda i: (0, COL_MISC_A // LANES)),
            pl.BlockSpec((t, LANES), lambda i: (0, COL_MISC_B // LANES)),
            _layer_spec((1, Q_LORA), layer), _layer_spec((1, KV_LORA), layer),
            _layer_spec((Q_LORA, D_MLA), layer), _layer_spec((Q_LORA, nr), layer), _layer_spec((Q_LORA, nr), layer),
            _layer_spec((N_MLA_HEADS, D_NOPE, KV_LORA), layer),
            _const_spec((t, nr)), _const_spec((t, nr)), _const_spec((t, LANES)), _const_spec((t, LANES)),
        ],
        out_specs=[
            _const_spec((t, N_MLA_HEADS * KV_LORA)),
            _const_spec((t, nr)),
            _const_spec((t, KV_LORA)),
            _const_spec((t, LANES)),
        ],
        out_shape=[
            jax.ShapeDtypeStruct((t, N_MLA_HEADS * KV_LORA), BF16),
            jax.ShapeDtypeStruct((t, nr), BF16),
            jax.ShapeDtypeStruct((t, KV_LORA), F32),
            jax.ShapeDtypeStruct((t, LANES), F32),
        ],
        compiler_params=_cparams(("arbitrary",)),
        name="mla_prep_dec",
    )(proj, proj, proj, proj, w['qnw'], w['kvnw'], w['wqn'], w['wqr'], w['wqr2'], w['wukt'], cq, sq, ck, sk)


def _attn_s_body(ql_ref, qr_ref, cc_ref, ckr_ref, nc_ref, nkr_ref, wuv_ref, o_ref, m_ref, l_ref, acc_ref, *, lq):
    j = pl.program_id(1)
    nj = pl.num_programs(1)
    ql = ql_ref[0]
    qr = qr_ref[0]

    @pl.when(j == 0)
    def _():
        m_ref[...] = jnp.full(m_ref.shape, NEG_INF, F32)
        l_ref[...] = jnp.zeros(l_ref.shape, F32)
        acc_ref[...] = jnp.zeros(acc_ref.shape, F32)

    def update(c16, kr16):
        s = _dot_nt(ql, c16) + _dot_nt(qr, kr16)
        m_prev = m_ref[...]
        m_new = jnp.maximum(m_prev, jnp.max(s, axis=-1, keepdims=True))
        alpha = jnp.exp(m_prev - m_new)
        p = jnp.exp(s - m_new)
        l_ref[...] = alpha * l_ref[...] + jnp.sum(p, axis=-1, keepdims=True)
        acc_ref[...] = alpha * acc_ref[...] + _dot(p.astype(BF16), c16)
        m_ref[...] = m_new

    update(cc_ref[0].astype(BF16), ckr_ref[0].astype(BF16))

    @pl.when(j == nj - 1)
    def _():
        update(nc_ref[0].astype(BF16), nkr_ref[0].astype(BF16))
        olat = (acc_ref[...] / l_ref[...]).astype(BF16)
        for h in range(N_MLA_HEADS):
            o_ref[0, h * lq:(h + 1) * lq, :] = _dot(olat[h * lq:(h + 1) * lq, :], wuv_ref[h])


def _attn_s_call(qlat, qr, cache_ckv, cache_kr, new_ckv, new_kr, wuv, layer, *, tk):
    b, rows, _ = qlat.shape
    lq = rows // N_MLA_HEADS
    past = cache_ckv.shape[2]
    assert past % tk == 0, (past, tk)
    body = functools.partial(_attn_s_body, lq=lq)
    return pl.pallas_call(
        body,
        grid=(b, past // tk),
        in_specs=[
            pl.BlockSpec((1, rows, KV_LORA), lambda i, j: (i, 0, 0)),
            pl.BlockSpec((1, rows, D_ROPE), lambda i, j: (i, 0, 0)),
            pl.BlockSpec((None, 1, tk, KV_LORA), lambda i, j: (layer, i, j, 0)),
            pl.BlockSpec((None, 1, tk, D_ROPE), lambda i, j: (layer, i, j, 0)),
            pl.BlockSpec((1, lq, KV_LORA), lambda i, j: (i, 0, 0)),
            pl.BlockSpec((1, lq, D_ROPE), lambda i, j: (i, 0, 0)),
            _layer_spec((N_MLA_HEADS, KV_LORA, D_V), layer),
        ],
        out_specs=pl.BlockSpec((1, rows, D_V), lambda i, j: (i, 0, 0)),
        out_shape=jax.ShapeDtypeStruct((b, rows, D_V), F32),
        scratch_shapes=[
            pltpu.VMEM((rows, 1), F32),
            pltpu.VMEM((rows, 1), F32),
            pltpu.VMEM((rows, KV_LORA), F32),
        ],
        compiler_params=_cparams(("parallel", "arbitrary")),
        name="mla_attn_dec",
    )(qlat, qr, cache_ckv, cache_kr, new_ckv, new_kr, wuv)


def _mem_attn_body(q_ref, mk_ref, mv_ref, o_ref, *, head_split):
    scale = MEM_HEAD_DIM ** -0.5
    for h in range(MEM_HEADS):
        hs = slice(h * MEM_HEAD_DIM, (h + 1) * MEM_HEAD_DIM)
        if head_split:
            kh = mk_ref[0, :, h, :]
            vh = mv_ref[0, :, h, :]
        else:
            kh = mk_ref[0, :, hs]
            vh = mv_ref[0, :, hs]
        s = _dot_nt(q_ref[:, hs].astype(BF16), kh.astype(BF16)) * scale
        p = jnp.exp(s - jnp.max(s, axis=-1, keepdims=True))
        l = jnp.sum(p, axis=-1, keepdims=True)
        o_ref[:, hs] = (_dot(p.astype(BF16), vh.astype(BF16)) / l).astype(BF16)


def _mem_attn_call(proj, mk, mv, layer, *, batch, seq, tm):
    t = batch * seq
    assert seq % tm == 0, (seq, tm)
    nrb = seq // tm
    head_split = mk.ndim == 5
    if head_split:
        mem_spec = pl.BlockSpec((None, 1, N_MEM, MEM_HEADS, MEM_HEAD_DIM), lambda b, r: (layer, b, 0, 0, 0))
    else:
        mem_spec = pl.BlockSpec((1, N_MEM, D_MEM), lambda b, r: (b, 0, 0))
    return pl.pallas_call(
        functools.partial(_mem_attn_body, head_split=head_split),
        grid=(batch, nrb),
        in_specs=[
            pl.BlockSpec((tm, D_MEM), lambda b, r: (b * nrb + r, COL_QMEM // D_MEM)),
            mem_spec,
            mem_spec,
        ],
        out_specs=pl.BlockSpec((tm, D_MEM), lambda b, r: (b * nrb + r, 0)),
        out_shape=jax.ShapeDtypeStruct((t, D_MEM), BF16),
        compiler_params=_cparams(("parallel", "parallel")),
        name="mem_attn",
    )(proj, mk, mv)


def _merge_body(y_ref, o_ref, g_ref, om_ref, gates_ref, x_ref, wos_ref, woa_ref, wom_ref, wout_ref,
                lng_ref, lnb_ref, x32_ref, x16_ref):
    ys = _dot(y_ref[...], wos_ref[...])
    ya = _dot((o_ref[...] * _silu(g_ref[...])).astype(BF16), woa_ref[...])
    ym = _dot(om_ref[...], wom_ref[...])
    gs = jax.nn.sigmoid(gates_ref[...])
    h = gs[:, 0:D_MODEL] * ys + gs[:, D_MODEL:2 * D_MODEL] * ya + gs[:, 2 * D_MODEL:3 * D_MODEL] * ym
    out = _dot(h.astype(BF16), wout_ref[...])
    xn = _layer_norm(ALPHA * x_ref[...] + out, lng_ref[...], lnb_ref[...])
    x32_ref[...] = xn
    x16_ref[...] = xn.astype(BF16)


def _merge_call(y, o, proj, om, x, w, layer, *, tm):
    t = x.shape[0]
    assert t % tm == 0, (t, tm)
    rowspec = lambda width: pl.BlockSpec((tm, width), lambda i: (i, 0))
    return pl.pallas_call(
        _merge_body,
        grid=(t // tm,),
        in_specs=[
            rowspec(D_SSM), rowspec(D_MLA),
            pl.BlockSpec((tm, D_MLA), lambda i: (i, COL_GMLA // D_MLA)),
            rowspec(D_MEM),
            pl.BlockSpec((tm, 3 * D_MODEL), lambda i: (i, COL_GATES // (3 * D_MODEL))),
            rowspec(D_MODEL),
            _layer_spec((D_SSM, D_MODEL), layer), _layer_spec((D_MLA, D_MODEL), layer),
            _layer_spec((D_MEM, D_MODEL), layer), _layer_spec((D_MODEL, D_MODEL), layer),
            _layer_spec((1, D_MODEL), layer), _layer_spec((1, D_MODEL), layer),
        ],
        out_specs=[rowspec(D_MODEL), rowspec(D_MODEL)],
        out_shape=[jax.ShapeDtypeStruct((t, D_MODEL), F32), jax.ShapeDtypeStruct((t, D_MODEL), BF16)],
        compiler_params=_cparams(("parallel",)),
        name="merge_out",
    )(y, o, proj, om, proj, x, w['wos'], w['woa'], w['wom'], w['wout'], w['lng'], w['lnb'])


def _rot_cols(w):
    half = D_ROPE // 2
    return jnp.concatenate([-w[..., half:], w[..., :half]], axis=-1)


def _pack_weights(p):
    w_in = p['w_in']
    seg = {}
    off = 0
    for name, width in (('z', D_SSM), ('xbc', CONV_DIM), ('dt', N_SSM_HEADS), ('qa', Q_LORA), ('kva', KV_LORA),
                        ('kr', D_ROPE), ('gmla', D_MLA), ('qmem', D_MEM), ('gates', 3 * D_MODEL)):
        seg[name] = w_in[:, :, off:off + width].astype(BF16)
        off += width
    zc = lambda n: jnp.zeros((DEPTH, D_MODEL, n), BF16)
    w_proj = jnp.concatenate([
        seg['xbc'], seg['gates'], seg['z'], seg['gmla'], seg['qmem'], seg['qa'], seg['kva'],
        seg['dt'], zc(32), seg['kr'], zc(32),
        zc(64), _rot_cols(seg['kr']), zc(32),
    ], axis=2)

    wqb = p['w_q_b'].astype(BF16).reshape(DEPTH, Q_LORA, N_MLA_HEADS, D_NOPE + D_ROPE)
    wq_n = wqb[..., :D_NOPE]
    wq_r = wqb[..., D_NOPE:]
    zq = jnp.zeros((DEPTH, Q_LORA, N_MLA_HEADS, HEAD_PAD - D_NOPE - D_ROPE), BF16)
    flat = lambda a: a.reshape(DEPTH, a.shape[1], -1)
    wq1 = flat(jnp.concatenate([wq_n, wq_r, zq], axis=-1))
    wq2 = flat(jnp.concatenate([jnp.zeros_like(wq_n), _rot_cols(wq_r), zq], axis=-1))
    wuk = p['w_uk'].astype(BF16)
    wuv = p['w_uv'].astype(BF16)
    zk = jnp.zeros((DEPTH, KV_LORA, N_MLA_HEADS, HEAD_PAD - D_NOPE), BF16)
    wk = flat(jnp.concatenate([wuk, zk], axis=-1))
    wv = flat(jnp.concatenate([wuv, zk], axis=-1))

    pad128 = lambda v: jnp.pad(v, ((0, 0), (0, LANES - v.shape[1]))).reshape(DEPTH, 1, LANES)
    row = lambda v: v.reshape(DEPTH, 1, v.shape[-1])
    return dict(
        w_proj=w_proj,
        cw=p['conv_w'], cb=row(p['conv_b']),
        dtb=pad128(p['dt_bias']), alog=pad128(p['a_log']),
        dskip=row(jnp.repeat(p['d_skip'], SSM_HEAD_DIM, axis=1)),
        nw=row(p['ssm_norm_w']),
        qnw=row(p['q_a_norm_w']), kvnw=row(p['kv_norm_w']),
        wq1=wq1, wq2=wq2, wk=wk, wv=wv,
        wqn=flat(wq_n), wqr=flat(wq_r), wqr2=flat(_rot_cols(wq_r)),
        wukt=jnp.transpose(wuk, (0, 2, 3, 1)),
        wuv=jnp.transpose(wuv, (0, 2, 1, 3)),
        wmk=p['w_mem_k'].astype(BF16), wmv=p['w_mem_v'].astype(BF16),
        wos=p['w_o_ssm'].astype(BF16), woa=p['w_o_mla'].astype(BF16),
        wom=p['w_o_mem'].astype(BF16), wout=p['w_out'].astype(BF16),
        lng=row(p['ln_g']), lnb=row(p['ln_b']),
    )


def _rope_tables(pos):
    half = D_ROPE // 2
    inv = ROPE_THETA ** (-jnp.arange(half, dtype=F32) / half)
    ang = pos.astype(F32)[:, None] * inv[None, :]
    cos = jnp.cos(ang)
    sin = jnp.sin(ang)
    return jnp.concatenate([cos, cos], axis=-1), jnp.concatenate([sin, sin], axis=-1)


def _expansion_matrix():
    e = np.zeros((LANES, D_SSM), np.float32)
    for h in range(N_SSM_HEADS):
        e[h, h * SSM_HEAD_DIM:(h + 1) * SSM_HEAD_DIM] = 1.0
    return jnp.asarray(e, dtype=BF16)


def kernel(x_prompt, x_sample, mem_prompt, state_ssm, state_conv, cache_ckv, cache_krope, cache_mem_k, cache_mem_v,
           ln_in_g, ln_in_b, w_in, conv_w, conv_b, dt_bias, a_log, d_skip, ssm_norm_w, w_o_ssm, q_a_norm_w, w_q_b,
           kv_norm_w, w_uk, w_uv, w_o_mla, w_mem_k, w_mem_v, w_o_mem, w_out, ln_g, ln_b):
    params = dict(w_in=w_in, conv_w=conv_w, conv_b=conv_b, dt_bias=dt_bias, a_log=a_log, d_skip=d_skip,
                  ssm_norm_w=ssm_norm_w, w_o_ssm=w_o_ssm, q_a_norm_w=q_a_norm_w, w_q_b=w_q_b, kv_norm_w=kv_norm_w,
                  w_uk=w_uk, w_uv=w_uv, w_o_mla=w_o_mla, w_mem_k=w_mem_k, w_mem_v=w_mem_v, w_o_mem=w_o_mem,
                  w_out=w_out, ln_g=ln_g, ln_b=ln_b)
    bp, lp, _ = x_prompt.shape
    bs, ls, _ = x_sample.shape
    past = cache_ckv.shape[2]
    tp = bp * lp
    ts = bs * ls
    assert w_in.shape == (DEPTH, D_MODEL, N_PROJ - 192), w_in.shape
    assert (past // CHUNK + 1) * CHUNK >= past + ls, "decode block would need a causal mask"
    assert ls >= CONV_W - 1 and lp >= CONV_W - 1

    w = _pack_weights(params)
    emat = _expansion_matrix()
    tm_p = min(512, tp)
    xp32, xp16 = _ln_call(x_prompt.reshape(tp, D_MODEL), ln_in_g, ln_in_b, tm_p)
    xs32, xs16 = _ln_call(x_sample.reshape(ts, D_MODEL), ln_in_g, ln_in_b, ts)
    mem16 = mem_prompt.reshape(bp * N_MEM, D_MODEL).astype(BF16)

    cos_p, sin_p = _rope_tables(jnp.arange(lp, dtype=jnp.int32))
    cos_s, sin_s = _rope_tables(past + jnp.arange(ls, dtype=jnp.int32))
    z64 = lambda c: jnp.zeros((c.shape[0], 64), F32)
    z32 = lambda c: jnp.zeros((c.shape[0], 32), F32)
    one64 = jnp.ones((lp, 64), F32)
    cq_p = jnp.concatenate([one64, cos_p, z32(cos_p)], axis=1) * (MLA_SCALE * LOG2E)
    sq_p = jnp.concatenate([z64(sin_p), sin_p, z32(sin_p)], axis=1) * (MLA_SCALE * LOG2E)
    ck_p = jnp.concatenate([z64(cos_p), cos_p, z32(cos_p)], axis=1)
    sk_p = jnp.concatenate([z64(sin_p), sin_p, z32(sin_p)], axis=1)
    cq_s = jnp.tile(jnp.tile(cos_s, (1, N_MLA_HEADS)), (bs, 1)) * MLA_SCALE
    sq_s = jnp.tile(jnp.tile(sin_s, (1, N_MLA_HEADS)), (bs, 1)) * MLA_SCALE
    ck_s = jnp.tile(jnp.concatenate([z64(cos_s), cos_s, z32(cos_s)], axis=1), (bs, 1))
    sk_s = jnp.tile(jnp.concatenate([z64(sin_s), sin_s, z32(sin_s)], axis=1), (bs, 1))

    tail_rows = 8 - (CONV_W - 1)
    rb_p = min(256, lp)
    tc_p = min(128, lp)
    tq = min(512, lp)
    tmm = min(1024, tp)
    tk_s = min(1024, past)

    outs = {k: [] for k in ('p_ssm', 'p_conv', 'p_ckv', 'p_kr', 'p_mk', 'p_mv', 's_ssm', 's_conv', 's_ckv', 's_kr')}
    for i in range(DEPTH):
        proj = _mm_call(xp16, w['w_proj'], i, tmm, 1024, "in_proj")
        y, ssm_new, tail = _ssm_call(proj, None, w, emat, i, batch=bp, seq=lp, rb=rb_p, tc=tc_p)
        q, k, v, ckv, kr = _mla_prep_call(proj, w, cq_p, sq_p, ck_p, sk_p, i, seq=lp, tm=tq)
        o = _attn_call(q, k, v, batch=bp, seq=lp, tq=tq)
        mk = _mm_call(mem16, w['wmk'], i, bp * N_MEM, 1024, "mem_k_proj")
        mv = _mm_call(mem16, w['wmv'], i, bp * N_MEM, 1024, "mem_v_proj")
        om = _mem_attn_call(proj, mk.reshape(bp, N_MEM, D_MEM), mv.reshape(bp, N_MEM, D_MEM), i,
                            batch=bp, seq=lp, tm=tq)
        xp32, xp16 = _merge_call(y, o, proj, om, xp32, w, i, tm=min(512, tp))
        outs['p_ssm'].append(ssm_new)
        outs['p_conv'].append(tail[:, tail_rows:, :])
        outs['p_ckv'].append(ckv.reshape(bp, lp, KV_LORA))
        outs['p_kr'].append(kr[:, 64:64 + D_ROPE].reshape(bp, lp, D_ROPE))
        outs['p_mk'].append(mk.reshape(bp, N_MEM, MEM_HEADS, MEM_HEAD_DIM))
        outs['p_mv'].append(mv.reshape(bp, N_MEM, MEM_HEADS, MEM_HEAD_DIM))

        proj_s = _mm_call(xs16, w['w_proj'], i, ts, 1024, "in_proj_dec")
        tail0 = jnp.pad(state_conv[i], ((0, 0), (tail_rows, 0), (0, 0)))
        y_s, ssm_new_s, tail_s = _ssm_call(proj_s, (tail0, state_ssm), w, emat, i, batch=bs, seq=ls, rb=ls, tc=ls)
        qlat, qr, ckv_s, kr_s = _mla_prep_s_call(proj_s, w, cq_s, sq_s, ck_s, sk_s, i)
        kr_s = kr_s[:, 64:64 + D_ROPE]
        to_hq = lambda a, d: jnp.transpose(a.reshape(bs, ls, N_MLA_HEADS, d), (0, 2, 1, 3)).reshape(bs, N_MLA_HEADS * ls, d)
        o_s = _attn_s_call(to_hq(qlat, KV_LORA), to_hq(qr, D_ROPE), cache_ckv, cache_krope,
                           ckv_s.reshape(bs, ls, KV_LORA), kr_s.reshape(bs, ls, D_ROPE), w['wuv'], i, tk=tk_s)
        o_s = jnp.transpose(o_s.reshape(bs, N_MLA_HEADS, ls, D_V), (0, 2, 1, 3)).reshape(ts, D_MLA)
        om_s = _mem_attn_call(proj_s, cache_mem_k, cache_mem_v, i, batch=bs, seq=ls, tm=ls)
        xs32, xs16 = _merge_call(y_s, o_s, proj_s, om_s, xs32, w, i, tm=min(256, ts))
        outs['s_ssm'].append(ssm_new_s)
        outs['s_conv'].append(tail_s[:, tail_rows:, :])
        outs['s_ckv'].append(ckv_s.reshape(bs, ls, KV_LORA))
        outs['s_kr'].append(kr_s.reshape(bs, ls, D_ROPE))

    st = lambda k: jnp.stack(outs[k])
    return (xp32.reshape(bp, lp, D_MODEL), xs32.reshape(bs, ls, D_MODEL), st('p_ssm'), st('p_conv'), st('p_ckv'),
            st('p_kr'), st('p_mk'), st('p_mv'), st('s_ssm'), st('s_conv'), st('s_ckv'), st('s_kr'))
```

```python
import functools
import math

import numpy as np
import jax
import jax.numpy as jnp
from jax import lax
from jax.experimental import pallas as pl
from jax.experimental.pallas import tpu as pltpu

F32 = jnp.float32
BF16 = jnp.bfloat16

D_MODEL = 1024
DEPTH = 4
CHUNK = 64
D_SSM = 2048
SSM_HEAD_DIM = 64
N_SSM_HEADS = 32
SSM_GROUPS = 4
HEADS_PER_GROUP = N_SSM_HEADS // SSM_GROUPS
D_STATE = 128
CONV_W = 4
CONV_DIM = D_SSM + 2 * SSM_GROUPS * D_STATE
N_MLA_HEADS = 16
D_NOPE = 64
D_ROPE = 32
D_V = 64
Q_LORA = 512
KV_LORA = 256
D_MLA = N_MLA_HEADS * D_V
MLA_SCALE = (D_NOPE + D_ROPE) ** -0.5
ROPE_THETA = 10000.0
N_MEM = 256
MEM_HEADS = 4
MEM_HEAD_DIM = 256
D_MEM = MEM_HEADS * MEM_HEAD_DIM
ALPHA = (2 * DEPTH) ** 0.25
EPS = 1e-5
NEG_INF = -1e30
LOG2E = math.log2(math.e)

LANES = 128
HEAD_PAD = 128
VMEM_LIMIT = 56 * 1024 * 1024

COL_XBC = 0
COL_GATES = 3072
COL_Z = 6144
COL_GMLA = 8192
COL_QMEM = 9216
COL_QA = 10240
COL_KVA = 10752
COL_MISC_A = 11008
COL_MISC_B = 11136
N_PROJ = 11264


def _cparams(sem):
    return pltpu.CompilerParams(dimension_semantics=sem, vmem_limit_bytes=VMEM_LIMIT)


def _const_spec(shape):
    nd = len(shape)
    return pl.BlockSpec(shape, lambda *_: (0,) * nd)


def _layer_spec(shape, layer):
    nd = len(shape)
    return pl.BlockSpec((None,) + tuple(shape), lambda *_: (layer,) + (0,) * nd, pipeline_mode=pl.Buffered(1))


def _silu(x):
    return x * jax.nn.sigmoid(x)


def _softplus(x):
    return jnp.maximum(x, 0.0) + jnp.log1p(jnp.exp(-jnp.abs(x)))


def _dot(a, b):
    return jnp.dot(a, b, preferred_element_type=F32)


def _dot_nt(a, b):
    return lax.dot_general(a, b, (((1,), (1,)), ((), ())), preferred_element_type=F32)


def _layer_norm(x, g, b):
    mu = jnp.mean(x, axis=-1, keepdims=True)
    xc = x - mu
    var = jnp.mean(xc * xc, axis=-1, keepdims=True)
    return xc * lax.rsqrt(var + EPS) * g + b


def _rms_norm(x, w):
    return x * lax.rsqrt(jnp.mean(x * x, axis=-1, keepdims=True) + EPS) * w


def _ln_body(x_ref, g_ref, b_ref, o32_ref, o16_ref):
    y = _layer_norm(x_ref[...], g_ref[...], b_ref[...])
    o32_ref[...] = y
    o16_ref[...] = y.astype(BF16)


def _ln_call(x, g, b, tm):
    t, d = x.shape
    assert t % tm == 0, (t, tm)
    return pl.pallas_call(
        _ln_body,
        grid=(t // tm,),
        in_specs=[pl.BlockSpec((tm, d), lambda i: (i, 0)), _const_spec((1, d)), _const_spec((1, d))],
        out_specs=[pl.BlockSpec((tm, d), lambda i: (i, 0)), pl.BlockSpec((tm, d), lambda i: (i, 0))],
        out_shape=[jax.ShapeDtypeStruct((t, d), F32), jax.ShapeDtypeStruct((t, d), BF16)],
        compiler_params=_cparams(("parallel",)),
        name="ln_in",
    )(x, g.reshape(1, d), b.reshape(1, d))


def _mm_body(x_ref, w_ref, o_ref):
    o_ref[...] = _dot(x_ref[...], w_ref[...]).astype(o_ref.dtype)


def _mm_call(x, w, layer, tm, tn, name):
    m, k = x.shape
    n = w.shape[2]
    assert m % tm == 0 and n % tn == 0, (m, n, tm, tn)
    return pl.pallas_call(
        _mm_body,
        grid=(m // tm, n // tn),
        in_specs=[pl.BlockSpec((tm, k), lambda i, j: (i, 0)),
                  pl.BlockSpec((None, k, tn), lambda i, j: (layer, 0, j))],
        out_specs=pl.BlockSpec((tm, tn), lambda i, j: (i, j)),
        out_shape=jax.ShapeDtypeStruct((m, n), F32),
        compiler_params=_cparams(("parallel", "parallel")),
        name=name,
    )(x, w)


def _cumsum_rows(x, t):
    row = lax.broadcasted_iota(jnp.int32, x.shape, 0)
    s = x
    k = 1
    while k < t:
        s = s + jnp.where(row >= k, pltpu.roll(s, k, 0), 0.0)
        k *= 2
    return s


def _expand_heads(w, e_ref):
    hi = w.astype(BF16)
    lo = (w - hi.astype(F32)).astype(BF16)
    e = e_ref[...]
    return _dot(hi, e) + _dot(lo, e)


def _ssm_body(*refs, rb, tc, has_prev):
    if has_prev:
        (xbc_ref, z_ref, misc_ref, tail0_ref, s0_ref, cw_ref, cb_ref, dtb_ref, alog_ref, dskip_ref, nw_ref, e_ref,
         y_ref, sfin_ref, tail_ref, cbuf, ybuf, dabuf, dtbuf, state, tail) = refs
    else:
        (xbc_ref, z_ref, misc_ref, cw_ref, cb_ref, dtb_ref, alog_ref, dskip_ref, nw_ref, e_ref,
         y_ref, sfin_ref, tail_ref, cbuf, ybuf, dabuf, dtbuf, state, tail) = refs
    r = pl.program_id(1)
    nr = pl.num_programs(1)
    pair_w = 2 * SSM_HEAD_DIM

    @pl.when(r == 0)
    def _():
        if has_prev:
            for hp in range(N_SSM_HEADS // 2):
                blk = s0_ref[0, 2 * hp:2 * hp + 2].reshape(pair_w, D_STATE)
                state[:, hp * pair_w:(hp + 1) * pair_w] = blk.T
            tail[...] = tail0_ref[0]
        else:
            state[...] = jnp.zeros(state.shape, F32)
            tail[...] = jnp.zeros(tail.shape, F32)

    x = xbc_ref[...]
    cw = cw_ref[...]
    cb = cb_ref[...]
    acc = cb + cw[3:4, :] * x
    for j in range(1, CONV_W):
        acc = acc + cw[3 - j:4 - j, :] * pltpu.roll(x, j, 0)
    cbuf[...] = _silu(acc)
    x8 = x[0:8, :]
    t8 = tail[...]
    row8 = lax.broadcasted_iota(jnp.int32, x8.shape, 0)
    acc8 = cb + cw[3:4, :] * x8
    for j in range(1, CONV_W):
        xj = jnp.where(row8 < j, pltpu.roll(t8, j, 0), pltpu.roll(x8, j, 0))
        acc8 = acc8 + cw[3 - j:4 - j, :] * xj
    cbuf[0:8, :] = _silu(acc8)
    tail[...] = x[rb - 8:rb, :]

    dt = _softplus(misc_ref[...] + dtb_ref[...])
    dtbuf[...] = dt
    dabuf[...] = dt * (-jnp.exp(alog_ref[...]))

    ri = lax.broadcasted_iota(jnp.int32, (tc, tc), 0)
    ci = lax.broadcasted_iota(jnp.int32, (tc, tc), 1)
    lower = ri >= ci
    pad_rows = LANES - tc

    def to_rows(a):
        if pad_rows:
            a = jnp.concatenate([a, jnp.zeros((pad_rows, LANES), F32)], axis=0)
        return a.T[:, :tc]

    def chunk(c, carry):
        r0 = pl.multiple_of(c * tc, tc)
        rows = pl.ds(r0, tc)
        da = dabuf[rows, :]
        dtc = dtbuf[rows, :]
        acs = _cumsum_rows(da, tc)
        acs_t = to_rows(acs)
        dt_t = to_rows(dtc)
        total = acs[tc - 1:tc, :]
        wexp = _expand_heads(jnp.exp(total - acs) * dtc, e_ref)
        etot = _expand_heads(jnp.broadcast_to(jnp.exp(total), (8, LANES)), e_ref)[0:1, :]
        xs = cbuf[rows, 0:D_SSM]
        xs16 = xs.astype(BF16)
        xw16 = (xs * wexp).astype(BF16)
        for g in range(SSM_GROUPS):
            bg = cbuf[rows, D_SSM + g * D_STATE:D_SSM + (g + 1) * D_STATE]
            cg = cbuf[rows, D_SSM + SSM_GROUPS * D_STATE + g * D_STATE:D_SSM + SSM_GROUPS * D_STATE + (g + 1) * D_STATE]
            bg16 = bg.astype(BF16)
            cg16 = cg.astype(BF16)
            cbm = _dot_nt(cg16, bg16)
            gcols = slice(g * HEADS_PER_GROUP * SSM_HEAD_DIM, (g + 1) * HEADS_PER_GROUP * SSM_HEAD_DIM)
            s_old = state[:, gcols]
            s16 = s_old.astype(BF16)
            for hh in range(0, HEADS_PER_GROUP, 2):
                pair = []
                for h in (g * HEADS_PER_GROUP + hh, g * HEADS_PER_GROUP + hh + 1):
                    colb = jnp.broadcast_to(acs[:, h:h + 1], (tc, LANES))
                    rowb = jnp.broadcast_to(acs_t[h:h + 1, :], (tc, tc))
                    dtb = jnp.broadcast_to(dt_t[h:h + 1, :], (tc, tc))
                    lmat = jnp.where(lower, jnp.exp(colb[:, :tc] - rowb), 0.0)
                    m16 = (cbm * lmat * dtb).astype(BF16)
                    cexp16 = (cg * jnp.exp(colb)).astype(BF16)
                    hl = h - g * HEADS_PER_GROUP
                    xh = xs16[:, h * SSM_HEAD_DIM:(h + 1) * SSM_HEAD_DIM]
                    sh = s16[:, hl * SSM_HEAD_DIM:(hl + 1) * SSM_HEAD_DIM]
                    if tc % LANES == 0:
                        yh = _dot(jnp.concatenate([m16, cexp16], axis=1), jnp.concatenate([xh, sh], axis=0))
                    else:
                        yh = _dot(m16, xh) + _dot(cexp16, sh)
                    pair.append(yh)
                h0 = g * HEADS_PER_GROUP + hh
                ybuf[rows, h0 * SSM_HEAD_DIM:(h0 + 2) * SSM_HEAD_DIM] = jnp.concatenate(pair, axis=1)
            bg_t = to_rows(bg).astype(BF16)
            state[:, gcols] = s_old * etot[:, gcols] + _dot(bg_t, xw16[:, gcols])
        y = ybuf[rows, :] + dskip_ref[...] * xs
        zc = z_ref[rows, :]
        gt = y * _silu(zc)
        gw = D_SSM // SSM_GROUPS
        for g in range(SSM_GROUPS):
            gg = gt[:, g * gw:(g + 1) * gw]
            ms = jnp.mean(gg * gg, axis=-1, keepdims=True)
            y_ref[rows, g * gw:(g + 1) * gw] = (gg * lax.rsqrt(ms + EPS) * nw_ref[:, g * gw:(g + 1) * gw]).astype(BF16)
        return carry

    lax.fori_loop(0, rb // tc, chunk, 0)

    @pl.when(r == nr - 1)
    def _():
        for hp in range(N_SSM_HEADS // 2):
            blk = state[:, hp * pair_w:(hp + 1) * pair_w].T
            sfin_ref[0, 2 * hp:2 * hp + 2] = blk.reshape(2, SSM_HEAD_DIM, D_STATE)
        tail_ref[0] = tail[...]


def _ssm_call(proj, prev, w, emat, layer, *, batch, seq, rb, tc):
    t = batch * seq
    assert seq % rb == 0 and rb % tc == 0 and rb % 8 == 0, (seq, rb, tc)
    nrb = seq // rb
    row = lambda b, r: b * nrb + r
    has_prev = prev is not None
    body = functools.partial(_ssm_body, rb=rb, tc=tc, has_prev=has_prev)
    state_block = (1, N_SSM_HEADS, SSM_HEAD_DIM, D_STATE)
    prev_specs = []
    prev_args = []
    if has_prev:
        prev_specs = [pl.BlockSpec((1, 8, CONV_DIM), lambda b, r: (b, 0, 0)),
                      pl.BlockSpec((None,) + state_block, lambda b, r: (layer, b, 0, 0, 0))]
        prev_args = list(prev)
    return pl.pallas_call(
        body,
        grid=(batch, nrb),
        in_specs=[
            pl.BlockSpec((rb, CONV_DIM), lambda b, r: (row(b, r), COL_XBC // CONV_DIM)),
            pl.BlockSpec((rb, D_SSM), lambda b, r: (row(b, r), COL_Z // D_SSM)),
            pl.BlockSpec((rb, LANES), lambda b, r: (row(b, r), COL_MISC_A // LANES)),
            *prev_specs,
            _layer_spec((CONV_W, CONV_DIM), layer),
            _layer_spec((1, CONV_DIM), layer),
            _layer_spec((1, LANES), layer),
            _layer_spec((1, LANES), layer),
            _layer_spec((1, D_SSM), layer),
            _layer_spec((1, D_SSM), layer),
            _const_spec((LANES, D_SSM)),
        ],
        out_specs=[
            pl.BlockSpec((rb, D_SSM), lambda b, r: (row(b, r), 0)),
            pl.BlockSpec(state_block, lambda b, r: (b, 0, 0, 0)),
            pl.BlockSpec((1, 8, CONV_DIM), lambda b, r: (b, 0, 0)),
        ],
        out_shape=[
            jax.ShapeDtypeStruct((t, D_SSM), BF16),
            jax.ShapeDtypeStruct((batch,) + state_block[1:], F32),
            jax.ShapeDtypeStruct((batch, 8, CONV_DIM), F32),
        ],
        scratch_shapes=[
            pltpu.VMEM((rb, CONV_DIM), F32),
            pltpu.VMEM((rb, D_SSM), F32),
            pltpu.VMEM((rb, LANES), F32),
            pltpu.VMEM((rb, LANES), F32),
            pltpu.VMEM((D_STATE, D_SSM), F32),
            pltpu.VMEM((8, CONV_DIM), F32),
        ],
        compiler_params=_cparams(("parallel", "arbitrary")),
        name="ssd_mixer",
    )(proj, proj, proj, *prev_args, w['cw'], w['cb'], w['dtb'], w['alog'], w['dskip'], w['nw'], emat)


def _mla_prep_body(qa_ref, kva_ref, ma_ref, mb_ref, qnw_ref, kvnw_ref, wq1_ref, wq2_ref, wk_ref, wv_ref,
                   cq_ref, sq_ref, ck_ref, sk_ref, q_ref, k_ref, v_ref, ckv_ref, kr_ref):
    qa16 = _rms_norm(qa_ref[...], qnw_ref[...]).astype(BF16)
    q1 = _dot(qa16, wq1_ref[...])
    q2 = _dot(qa16, wq2_ref[...])
    cq = jnp.tile(cq_ref[...], (1, N_MLA_HEADS))
    sq = jnp.tile(sq_ref[...], (1, N_MLA_HEADS))
    q_ref[...] = (q1 * cq + q2 * sq).astype(BF16)
    ckv = _rms_norm(kva_ref[...], kvnw_ref[...])
    ckv_ref[...] = ckv
    ckv16 = ckv.astype(BF16)
    kr = ma_ref[...] * ck_ref[...] + mb_ref[...] * sk_ref[...]
    kr_ref[...] = kr
    k_ref[...] = (_dot(ckv16, wk_ref[...]) + jnp.tile(kr, (1, N_MLA_HEADS))).astype(BF16)
    lane = lax.broadcasted_iota(jnp.int32, (1, N_MLA_HEADS * HEAD_PAD), 1)
    ones_col = jnp.where(lane % HEAD_PAD == D_V, 1.0, 0.0)
    v_ref[...] = (_dot(ckv16, wv_ref[...]) + ones_col).astype(BF16)


def _mla_prep_call(proj, w, cq, sq, ck, sk, layer, *, seq, tm):
    t = proj.shape[0]
    assert seq % tm == 0 and t % seq == 0, (t, seq, tm)
    npos = seq // tm
    hp = N_MLA_HEADS * HEAD_PAD
    tab = pl.BlockSpec((tm, LANES), lambda i: (i % npos, 0))
    return pl.pallas_call(
        _mla_prep_body,
        grid=(t // tm,),
        in_specs=[
            pl.BlockSpec((tm, Q_LORA), lambda i: (i, COL_QA // Q_LORA)),
            pl.BlockSpec((tm, KV_LORA), lambda i: (i, COL_KVA // KV_LORA)),
            pl.BlockSpec((tm, LANES), lambda i: (i, COL_MISC_A // LANES)),
            pl.BlockSpec((tm, LANES), lambda i: (i, COL_MISC_B // LANES)),
            _layer_spec((1, Q_LORA), layer), _layer_spec((1, KV_LORA), layer),
            _layer_spec((Q_LORA, hp), layer), _layer_spec((Q_LORA, hp), layer),
            _layer_spec((KV_LORA, hp), layer), _layer_spec((KV_LORA, hp), layer),
            tab, tab, tab, tab,
        ],
        out_specs=[
            pl.BlockSpec((tm, hp), lambda i: (i, 0)),
            pl.BlockSpec((tm, hp), lambda i: (i, 0)),
            pl.BlockSpec((tm, hp), lambda i: (i, 0)),
            pl.BlockSpec((tm, KV_LORA), lambda i: (i, 0)),
            pl.BlockSpec((tm, LANES), lambda i: (i, 0)),
        ],
        out_shape=[
            jax.ShapeDtypeStruct((t, hp), BF16),
            jax.ShapeDtypeStruct((t, hp), BF16),
            jax.ShapeDtypeStruct((t, hp), BF16),
            jax.ShapeDtypeStruct((t, KV_LORA), F32),
            jax.ShapeDtypeStruct((t, LANES), F32),
        ],
        compiler_params=_cparams(("parallel",)),
        name="mla_prep",
    )(proj, proj, proj, proj, w['qnw'], w['kvnw'], w['wq1'], w['wq2'], w['wk'], w['wv'], cq, sq, ck, sk)


def _attn_body(qi_ref, kj_ref, q_ref, k_ref, v_ref, o_ref, m_ref, acc_ref, s_ref, *, tq, tk):
    t = pl.program_id(1)
    i = qi_ref[t]
    j = kj_ref[t]

    @pl.when(j == 0)
    def _():
        m_ref[...] = jnp.full(m_ref.shape, NEG_INF, F32)
        acc_ref[...] = jnp.zeros(acc_ref.shape, F32)

    def scores(h):
        hs = slice(h * HEAD_PAD, (h + 1) * HEAD_PAD)
        s_ref[h % 2] = _dot_nt(q_ref[0, :, hs], k_ref[0, :, hs])

    def step(bias):
        scores(0)
        for h in range(N_MLA_HEADS):
            hs = slice(h * HEAD_PAD, (h + 1) * HEAD_PAD)
            if h + 1 < N_MLA_HEADS:
                scores(h + 1)
            s = s_ref[h % 2]
            if bias is not None:
                s = s + bias
            m_prev = m_ref[h]
            m_new = jnp.maximum(m_prev, jnp.max(s, axis=-1, keepdims=True))
            p = jnp.exp2(s - jnp.tile(m_new, (1, tk // LANES)))
            acc_ref[h] = jnp.exp2(m_prev - m_new) * acc_ref[h] + _dot(p.astype(BF16), v_ref[0, :, hs])
            m_ref[h] = m_new

    @pl.when(j < i)
    def _():
        step(None)

    @pl.when(j == i)
    def _():
        row = lax.broadcasted_iota(jnp.int32, (tq, tk), 0)
        col = lax.broadcasted_iota(jnp.int32, (tq, tk), 1)
        step(jnp.where(col < ((row // CHUNK + 1) * CHUNK), 0.0, NEG_INF))
        for h in range(0, N_MLA_HEADS, 2):
            a = acc_ref[h]
            b = acc_ref[h + 1]
            a = a[:, :D_V] / a[:, D_V:D_V + 1]
            b = b[:, :D_V] / b[:, D_V:D_V + 1]
            o_ref[0, :, h * D_V:(h + 2) * D_V] = jnp.concatenate([a, b], axis=1)


def _attn_call(q, k, v, *, batch, seq, tq):
    hp = N_MLA_HEADS * HEAD_PAD
    q = q.reshape(batch, seq, hp)
    k = k.reshape(batch, seq, hp)
    v = v.reshape(batch, seq, hp)
    assert seq % tq == 0 and tq % CHUNK == 0, (seq, tq)
    nq = seq // tq
    pairs = [(i, j) for i in range(nq) for j in range(i + 1)]
    qi = jnp.asarray([p[0] for p in pairs], jnp.int32)
    kj = jnp.asarray([p[1] for p in pairs], jnp.int32)
    kv_spec = pl.BlockSpec((1, tq, hp), lambda b, t, qi, kj: (b, kj[t], 0))
    body = functools.partial(_attn_body, tq=tq, tk=tq)
    out = pl.pallas_call(
        body,
        grid_spec=pltpu.PrefetchScalarGridSpec(
            num_scalar_prefetch=2,
            grid=(batch, len(pairs)),
            in_specs=[pl.BlockSpec((1, tq, hp), lambda b, t, qi, kj: (b, qi[t], 0)), kv_spec, kv_spec],
            out_specs=pl.BlockSpec((1, tq, D_MLA), lambda b, t, qi, kj: (b, qi[t], 0)),
            scratch_shapes=[
                pltpu.VMEM((N_MLA_HEADS, tq, LANES), F32),
                pltpu.VMEM((N_MLA_HEADS, tq, LANES), F32),
                pltpu.VMEM((2, tq, tq), F32),
            ],
        ),
        out_shape=jax.ShapeDtypeStruct((batch, seq, D_MLA), F32),
        compiler_params=_cparams(("parallel", "arbitrary")),
        name="mla_attn",
    )(qi, kj, q, k, v)
    return out.reshape(batch * seq, D_MLA)


def _mla_prep_s_body(qa_ref, kva_ref, ma_ref, mb_ref, qnw_ref, kvnw_ref, wqn_ref, wqr_ref, wqr2_ref, wukt_ref,
                     cq_ref, sq_ref, ck_ref, sk_ref, qlat_ref, qr_ref, ckv_ref, kr_ref):
    qa16 = _rms_norm(qa_ref[...], qnw_ref[...]).astype(BF16)
    qn16 = _dot(qa16, wqn_ref[...]).astype(BF16)
    for h in range(N_MLA_HEADS):
        qh = qn16[:, h * D_NOPE:(h + 1) * D_NOPE]
        qlat_ref[:, h * KV_LORA:(h + 1) * KV_LORA] = (_dot(qh, wukt_ref[h]) * MLA_SCALE).astype(BF16)
    qr = _dot(qa16, wqr_ref[...]) * cq_ref[...] + _dot(qa16, wqr2_ref[...]) * sq_ref[...]
    qr_ref[...] = qr.astype(BF16)
    ckv_ref[...] = _rms_norm(kva_ref[...], kvnw_ref[...])
    kr_ref[...] = ma_ref[...] * ck_ref[...] + mb_ref[...] * sk_ref[...]


def _mla_prep_s_call(proj, w, cq, sq, ck, sk, layer):
    t = proj.shape[0]
    nr = N_MLA_HEADS * D_ROPE
    return pl.pallas_call(
        _mla_prep_s_body,
        grid=(1,),
        in_specs=[
            pl.BlockSpec((t, Q_LORA), lambda i: (0, COL_QA // Q_LORA)),
            pl.BlockSpec((t, KV_LORA), lambda i: (0, COL_KVA // KV_LORA)),
            pl.BlockSpec((t, LANES), lambda i: (0, COL_MISC_A // LANES)),
            pl.BlockSpec((t, LANES), lambda i: (0, COL_MISC_B // LANES)),
            _layer_spec((1, Q_LORA), layer), _layer_spec((1, KV_LORA), layer),
            _layer_spec((Q_LORA, D_MLA), layer), _layer_spec((Q_LORA, nr), layer), _layer_spec((Q_LORA, nr), layer),
            _layer_spec((N_MLA_HEADS, D_NOPE, KV_LORA), layer),
            _const_spec((t, nr)), _const_spec((t, nr)), _const_spec((t, LANES)), _const_spec((t, LANES)),
        ],
        out_specs=[
            _const_spec((t, N_MLA_HEADS * KV_LORA)),
            _const_spec((t, nr)),
            _const_spec((t, KV_LORA)),
            _const_spec((t, LANES)),
        ],
        out_shape=[
            jax.ShapeDtypeStruct((t, N_MLA_HEADS * KV_LORA), BF16),
            jax.ShapeDtypeStruct((t, nr), BF16),
            jax.ShapeDtypeStruct((t, KV_LORA), F32),
            jax.ShapeDtypeStruct((t, LANES), F32),
        ],
        compiler_params=_cparams(("arbitrary",)),
        name="mla_prep_dec",
    )(proj, proj, proj, proj, w['qnw'], w['kvnw'], w['wqn'], w['wqr'], w['wqr2'], w['wukt'], cq, sq, ck, sk)


def _attn_s_body(ql_ref, qr_ref, cc_ref, ckr_ref, nc_ref, nkr_ref, wuv_ref, o_ref, m_ref, l_ref, acc_ref, *, lq):
    j = pl.program_id(1)
    nj = pl.num_programs(1)
    ql = ql_ref[0]
    qr = qr_ref[0]

    @pl.when(j == 0)
    def _():
        m_ref[...] = jnp.full(m_ref.shape, NEG_INF, F32)
        l_ref[...] = jnp.zeros(l_ref.shape, F32)
        acc_ref[...] = jnp.zeros(acc_ref.shape, F32)

    def update(c16, kr16):
        s = _dot_nt(ql, c16) + _dot_nt(qr, kr16)
        m_prev = m_ref[...]
        m_new = jnp.maximum(m_prev, jnp.max(s, axis=-1, keepdims=True))
        alpha = jnp.exp(m_prev - m_new)
        p = jnp.exp(s - m_new)
        l_ref[...] = alpha * l_ref[...] + jnp.sum(p, axis=-1, keepdims=True)
        acc_ref[...] = alpha * acc_ref[...] + _dot(p.astype(BF16), c16)
        m_ref[...] = m_new

    update(cc_ref[0].astype(BF16), ckr_ref[0].astype(BF16))

    @pl.when(j == nj - 1)
    def _():
        update(nc_ref[0].astype(BF16), nkr_ref[0].astype(BF16))
        olat = (acc_ref[...] / l_ref[...]).astype(BF16)
        for h in range(N_MLA_HEADS):
            o_ref[0, h * lq:(h + 1) * lq, :] = _dot(olat[h * lq:(h + 1) * lq, :], wuv_ref[h])


def _attn_s_call(qlat, qr, cache_ckv, cache_kr, new_ckv, new_kr, wuv, layer, *, tk):
    b, rows, _ = qlat.shape
    lq = rows // N_MLA_HEADS
    past = cache_ckv.shape[2]
    assert past % tk == 0, (past, tk)
    body = functools.partial(_attn_s_body, lq=lq)
    return pl.pallas_call(
        body,
        grid=(b, past // tk),
        in_specs=[
            pl.BlockSpec((1, rows, KV_LORA), lambda i, j: (i, 0, 0)),
            pl.BlockSpec((1, rows, D_ROPE), lambda i, j: (i, 0, 0)),
            pl.BlockSpec((None, 1, tk, KV_LORA), lambda i, j: (layer, i, j, 0)),
            pl.BlockSpec((None, 1, tk, D_ROPE), lambda i, j: (layer, i, j, 0)),
            pl.BlockSpec((1, lq, KV_LORA), lambda i, j: (i, 0, 0)),
            pl.BlockSpec((1, lq, D_ROPE), lambda i, j: (i, 0, 0)),
            _layer_spec((N_MLA_HEADS, KV_LORA, D_V), layer),
        ],
        out_specs=pl.BlockSpec((1, rows, D_V), lambda i, j: (i, 0, 0)),
        out_shape=jax.ShapeDtypeStruct((b, rows, D_V), F32),
        scratch_shapes=[
            pltpu.VMEM((rows, 1), F32),
            pltpu.VMEM((rows, 1), F32),
            pltpu.VMEM((rows, KV_LORA), F32),
        ],
        compiler_params=_cparams(("parallel", "arbitrary")),
        name="mla_attn_dec",
    )(qlat, qr, cache_ckv, cache_kr, new_ckv, new_kr, wuv)


def _mem_attn_body(q_ref, mk_ref, mv_ref, o_ref, *, head_split):
    scale = MEM_HEAD_DIM ** -0.5
    for h in range(MEM_HEADS):
        hs = slice(h * MEM_HEAD_DIM, (h + 1) * MEM_HEAD_DIM)
        if head_split:
            kh = mk_ref[0, :, h, :]
            vh = mv_ref[0, :, h, :]
        else:
            kh = mk_ref[0, :, hs]
            vh = mv_ref[0, :, hs]
        s = _dot_nt(q_ref[:, hs].astype(BF16), kh.astype(BF16)) * scale
        p = jnp.exp(s - jnp.max(s, axis=-1, keepdims=True))
        l = jnp.sum(p, axis=-1, keepdims=True)
        o_ref[:, hs] = (_dot(p.astype(BF16), vh.astype(BF16)) / l).astype(BF16)


def _mem_attn_call(proj, mk, mv, layer, *, batch, seq, tm):
    t = batch * seq
    assert seq % tm == 0, (seq, tm)
    nrb = seq // tm
    head_split = mk.ndim == 5
    if head_split:
        mem_spec = pl.BlockSpec((None, 1, N_MEM, MEM_HEADS, MEM_HEAD_DIM), lambda b, r: (layer, b, 0, 0, 0))
    else:
        mem_spec = pl.BlockSpec((1, N_MEM, D_MEM), lambda b, r: (b, 0, 0))
    return pl.pallas_call(
        functools.partial(_mem_attn_body, head_split=head_split),
        grid=(batch, nrb),
        in_specs=[
            pl.BlockSpec((tm, D_MEM), lambda b, r: (b * nrb + r, COL_QMEM // D_MEM)),
            mem_spec,
            mem_spec,
        ],
        out_specs=pl.BlockSpec((tm, D_MEM), lambda b, r: (b * nrb + r, 0)),
        out_shape=jax.ShapeDtypeStruct((t, D_MEM), BF16),
        compiler_params=_cparams(("parallel", "parallel")),
        name="mem_attn",
    )(proj, mk, mv)


def _merge_body(y_ref, o_ref, g_ref, om_ref, gates_ref, x_ref, wos_ref, woa_ref, wom_ref, wout_ref,
                lng_ref, lnb_ref, x32_ref, x16_ref):
    ys = _dot(y_ref[...], wos_ref[...])
    ya = _dot((o_ref[...] * _silu(g_ref[...])).astype(BF16), woa_ref[...])
    ym = _dot(om_ref[...], wom_ref[...])
    gs = jax.nn.sigmoid(gates_ref[...])
    h = gs[:, 0:D_MODEL] * ys + gs[:, D_MODEL:2 * D_MODEL] * ya + gs[:, 2 * D_MODEL:3 * D_MODEL] * ym
    out = _dot(h.astype(BF16), wout_ref[...])
    xn = _layer_norm(ALPHA * x_ref[...] + out, lng_ref[...], lnb_ref[...])
    x32_ref[...] = xn
    x16_ref[...] = xn.astype(BF16)


def _merge_call(y, o, proj, om, x, w, layer, *, tm):
    t = x.shape[0]
    assert t % tm == 0, (t, tm)
    rowspec = lambda width: pl.BlockSpec((tm, width), lambda i: (i, 0))
    return pl.pallas_call(
        _merge_body,
        grid=(t // tm,),
        in_specs=[
            rowspec(D_SSM), rowspec(D_MLA),
            pl.BlockSpec((tm, D_MLA), lambda i: (i, COL_GMLA // D_MLA)),
            rowspec(D_MEM),
            pl.BlockSpec((tm, 3 * D_MODEL), lambda i: (i, COL_GATES // (3 * D_MODEL))),
            rowspec(D_MODEL),
            _layer_spec((D_SSM, D_MODEL), layer), _layer_spec((D_MLA, D_MODEL), layer),
            _layer_spec((D_MEM, D_MODEL), layer), _layer_spec((D_MODEL, D_MODEL), layer),
            _layer_spec((1, D_MODEL), layer), _layer_spec((1, D_MODEL), layer),
        ],
        out_specs=[rowspec(D_MODEL), rowspec(D_MODEL)],
        out_shape=[jax.ShapeDtypeStruct((t, D_MODEL), F32), jax.ShapeDtypeStruct((t, D_MODEL), BF16)],
        compiler_params=_cparams(("parallel",)),
        name="merge_out",
    )(y, o, proj, om, proj, x, w['wos'], w['woa'], w['wom'], w['wout'], w['lng'], w['lnb'])


def _rot_cols(w):
    half = D_ROPE // 2
    return jnp.concatenate([-w[..., half:], w[..., :half]], axis=-1)


def _pack_weights(p):
    w_in = p['w_in']
    seg = {}
    off = 0
    for name, width in (('z', D_SSM), ('xbc', CONV_DIM), ('dt', N_SSM_HEADS), ('qa', Q_LORA), ('kva', KV_LORA),
                        ('kr', D_ROPE), ('gmla', D_MLA), ('qmem', D_MEM), ('gates', 3 * D_MODEL)):
        seg[name] = w_in[:, :, off:off + width].astype(BF16)
        off += width
    zc = lambda n: jnp.zeros((DEPTH, D_MODEL, n), BF16)
    w_proj = jnp.concatenate([
        seg['xbc'], seg['gates'], seg['z'], seg['gmla'], seg['qmem'], seg['qa'], seg['kva'],
        seg['dt'], zc(32), seg['kr'], zc(32),
        zc(64), _rot_cols(seg['kr']), zc(32),
    ], axis=2)

    wqb = p['w_q_b'].astype(BF16).reshape(DEPTH, Q_LORA, N_MLA_HEADS, D_NOPE + D_ROPE)
    wq_n = wqb[..., :D_NOPE]
    wq_r = wqb[..., D_NOPE:]
    zq = jnp.zeros((DEPTH, Q_LORA, N_MLA_HEADS, HEAD_PAD - D_NOPE - D_ROPE), BF16)
    flat = lambda a: a.reshape(DEPTH, a.shape[1], -1)
    wq1 = flat(jnp.concatenate([wq_n, wq_r, zq], axis=-1))
    wq2 = flat(jnp.concatenate([jnp.zeros_like(wq_n), _rot_cols(wq_r), zq], axis=-1))
    wuk = p['w_uk'].astype(BF16)
    wuv = p['w_uv'].astype(BF16)
    zk = jnp.zeros((DEPTH, KV_LORA, N_MLA_HEADS, HEAD_PAD - D_NOPE), BF16)
    wk = flat(jnp.concatenate([wuk, zk], axis=-1))
    wv = flat(jnp.concatenate([wuv, zk], axis=-1))

    pad128 = lambda v: jnp.pad(v, ((0, 0), (0, LANES - v.shape[1]))).reshape(DEPTH, 1, LANES)
    row = lambda v: v.reshape(DEPTH, 1, v.shape[-1])
    return dict(
        w_proj=w_proj,
        cw=p['conv_w'], cb=row(p['conv_b']),
        dtb=pad128(p['dt_bias']), alog=pad128(p['a_log']),
        dskip=row(jnp.repeat(p['d_skip'], SSM_HEAD_DIM, axis=1)),
        nw=row(p['ssm_norm_w']),
        qnw=row(p['q_a_norm_w']), kvnw=row(p['kv_norm_w']),
        wq1=wq1, wq2=wq2, wk=wk, wv=wv,
        wqn=flat(wq_n), wqr=flat(wq_r), wqr2=flat(_rot_cols(wq_r)),
        wukt=jnp.transpose(wuk, (0, 2, 3, 1)),
        wuv=jnp.transpose(wuv, (0, 2, 1, 3)),
        wmk=p['w_mem_k'].astype(BF16), wmv=p['w_mem_v'].astype(BF16),
        wos=p['w_o_ssm'].astype(BF16), woa=p['w_o_mla'].astype(BF16),
        wom=p['w_o_mem'].astype(BF16), wout=p['w_out'].astype(BF16),
        lng=row(p['ln_g']), lnb=row(p['ln_b']),
    )


def _rope_tables(pos):
    half = D_ROPE // 2
    inv = ROPE_THETA ** (-jnp.arange(half, dtype=F32) / half)
    ang = pos.astype(F32)[:, None] * inv[None, :]
    cos = jnp.cos(ang)
    sin = jnp.sin(ang)
    return jnp.concatenate([cos, cos], axis=-1), jnp.concatenate([sin, sin], axis=-1)


def _expansion_matrix():
    e = np.zeros((LANES, D_SSM), np.float32)
    for h in range(N_SSM_HEADS):
        e[h, h * SSM_HEAD_DIM:(h + 1) * SSM_HEAD_DIM] = 1.0
    return jnp.asarray(e, dtype=BF16)


def kernel(x_prompt, x_sample, mem_prompt, state_ssm, state_conv, cache_ckv, cache_krope, cache_mem_k, cache_mem_v,
           ln_in_g, ln_in_b, w_in, conv_w, conv_b, dt_bias, a_log, d_skip, ssm_norm_w, w_o_ssm, q_a_norm_w, w_q_b,
           kv_norm_w, w_uk, w_uv, w_o_mla, w_mem_k, w_mem_v, w_o_mem, w_out, ln_g, ln_b):
    params = dict(w_in=w_in, conv_w=conv_w, conv_b=conv_b, dt_bias=dt_bias, a_log=a_log, d_skip=d_skip,
                  ssm_norm_w=ssm_norm_w, w_o_ssm=w_o_ssm, q_a_norm_w=q_a_norm_w, w_q_b=w_q_b, kv_norm_w=kv_norm_w,
                  w_uk=w_uk, w_uv=w_uv, w_o_mla=w_o_mla, w_mem_k=w_mem_k, w_mem_v=w_mem_v, w_o_mem=w_o_mem,
                  w_out=w_out, ln_g=ln_g, ln_b=ln_b)
    bp, lp, _ = x_prompt.shape
    bs, ls, _ = x_sample.shape
    past = cache_ckv.shape[2]
    tp = bp * lp
    ts = bs * ls
    assert w_in.shape == (DEPTH, D_MODEL, N_PROJ - 192), w_in.shape
    assert (past // CHUNK + 1) * CHUNK >= past + ls, "decode block would need a causal mask"
    assert ls >= CONV_W - 1 and lp >= CONV_W - 1

    w = _pack_weights(params)
    emat = _expansion_matrix()
    tm_p = min(512, tp)
    xp32, xp16 = _ln_call(x_prompt.reshape(tp, D_MODEL), ln_in_g, ln_in_b, tm_p)
    xs32, xs16 = _ln_call(x_sample.reshape(ts, D_MODEL), ln_in_g, ln_in_b, ts)
    mem16 = mem_prompt.reshape(bp * N_MEM, D_MODEL).astype(BF16)

    cos_p, sin_p = _rope_tables(jnp.arange(lp, dtype=jnp.int32))
    cos_s, sin_s = _rope_tables(past + jnp.arange(ls, dtype=jnp.int32))
    z64 = lambda c: jnp.zeros((c.shape[0], 64), F32)
    z32 = lambda c: jnp.zeros((c.shape[0], 32), F32)
    one64 = jnp.ones((lp, 64), F32)
    cq_p = jnp.concatenate([one64, cos_p, z32(cos_p)], axis=1) * (MLA_SCALE * LOG2E)
    sq_p = jnp.concatenate([z64(sin_p), sin_p, z32(sin_p)], axis=1) * (MLA_SCALE * LOG2E)
    ck_p = jnp.concatenate([z64(cos_p), cos_p, z32(cos_p)], axis=1)
    sk_p = jnp.concatenate([z64(sin_p), sin_p, z32(sin_p)], axis=1)
    cq_s = jnp.tile(jnp.tile(cos_s, (1, N_MLA_HEADS)), (bs, 1)) * MLA_SCALE
    sq_s = jnp.tile(jnp.tile(sin_s, (1, N_MLA_HEADS)), (bs, 1)) * MLA_SCALE
    ck_s = jnp.tile(jnp.concatenate([z64(cos_s), cos_s, z32(cos_s)], axis=1), (bs, 1))
    sk_s = jnp.tile(jnp.concatenate([z64(sin_s), sin_s, z32(sin_s)], axis=1), (bs, 1))

    tail_rows = 8 - (CONV_W - 1)
    rb_p = min(256, lp)
    tc_p = min(128, lp)
    tq = min(512, lp)
    tmm = min(1024, tp)
    tk_s = min(1024, past)

    outs = {k: [] for k in ('p_ssm', 'p_conv', 'p_ckv', 'p_kr', 'p_mk', 'p_mv', 's_ssm', 's_conv', 's_ckv', 's_kr')}
    for i in range(DEPTH):
        proj = _mm_call(xp16, w['w_proj'], i, tmm, 1024, "in_proj")
        y, ssm_new, tail = _ssm_call(proj, None, w, emat, i, batch=bp, seq=lp, rb=rb_p, tc=tc_p)
        q, k, v, ckv, kr = _mla_prep_call(proj, w, cq_p, sq_p, ck_p, sk_p, i, seq=lp, tm=tq)
        o = _attn_call(q, k, v, batch=bp, seq=lp, tq=tq)
        mk = _mm_call(mem16, w['wmk'], i, bp * N_MEM, 1024, "mem_k_proj")
        mv = _mm_call(mem16, w['wmv'], i, bp * N_MEM, 1024, "mem_v_proj")
        om = _mem_attn_call(proj, mk.reshape(bp, N_MEM, D_MEM), mv.reshape(bp, N_MEM, D_MEM), i,
                            batch=bp, seq=lp, tm=tq)
        xp32, xp16 = _merge_call(y, o, proj, om, xp32, w, i, tm=min(512, tp))
        outs['p_ssm'].append(ssm_new)
        outs['p_conv'].append(tail[:, tail_rows:, :])
        outs['p_ckv'].append(ckv.reshape(bp, lp, KV_LORA))
        outs['p_kr'].append(kr[:, 64:64 + D_ROPE].reshape(bp, lp, D_ROPE))
        outs['p_mk'].append(mk.reshape(bp, N_MEM, MEM_HEADS, MEM_HEAD_DIM))
        outs['p_mv'].append(mv.reshape(bp, N_MEM, MEM_HEADS, MEM_HEAD_DIM))

        proj_s = _mm_call(xs16, w['w_proj'], i, ts, 1024, "in_proj_dec")
        tail0 = jnp.pad(state_conv[i], ((0, 0), (tail_rows, 0), (0, 0)))
        y_s, ssm_new_s, tail_s = _ssm_call(proj_s, (tail0, state_ssm), w, emat, i, batch=bs, seq=ls, rb=ls, tc=ls)
        qlat, qr, ckv_s, kr_s = _mla_prep_s_call(proj_s, w, cq_s, sq_s, ck_s, sk_s, i)
        kr_s = kr_s[:, 64:64 + D_ROPE]
        to_hq = lambda a, d: jnp.transpose(a.reshape(bs, ls, N_MLA_HEADS, d), (0, 2, 1, 3)).reshape(bs, N_MLA_HEADS * ls, d)
        o_s = _attn_s_call(to_hq(qlat, KV_LORA), to_hq(qr, D_ROPE), cache_ckv, cache_krope,
                           ckv_s.reshape(bs, ls, KV_LORA), kr_s.reshape(bs, ls, D_ROPE), w['wuv'], i, tk=tk_s)
        o_s = jnp.transpose(o_s.reshape(bs, N_MLA_HEADS, ls, D_V), (0, 2, 1, 3)).reshape(ts, D_MLA)
        om_s = _mem_attn_call(proj_s, cache_mem_k, cache_mem_v, i, batch=bs, seq=ls, tm=ls)
        xs32, xs16 = _merge_call(y_s, o_s, proj_s, om_s, xs32, w, i, tm=min(256, ts))
        outs['s_ssm'].append(ssm_new_s)
        outs['s_conv'].append(tail_s[:, tail_rows:, :])
        outs['s_ckv'].append(ckv_s.reshape(bs, ls, KV_LORA))
        outs['s_kr'].append(kr_s.reshape(bs, ls, D_ROPE))

    st = lambda k: jnp.stack(outs[k])
    return (xp32.reshape(bp, lp, D_MODEL), xs32.reshape(bs, ls, D_MODEL), st('p_ssm'), st('p_conv'), st('p_ckv'),
            st('p_kr'), st('p_mk'), st('p_mv'), st('s_ssm'), st('s_conv'), st('s_ckv'), st('s_kr'))
```

```python
import functools
import math

import numpy as np
import jax
import jax.numpy as jnp
from jax import lax
from jax.experimental import pallas as pl
from jax.experimental.pallas import tpu as pltpu

F32 = jnp.float32
BF16 = jnp.bfloat16

D_MODEL = 1024
DEPTH = 4
CHUNK = 64
D_SSM = 2048
SSM_HEAD_DIM = 64
N_SSM_HEADS = 32
SSM_GROUPS = 4
HEADS_PER_GROUP = N_SSM_HEADS // SSM_GROUPS
D_STATE = 128
CONV_W = 4
CONV_DIM = D_SSM + 2 * SSM_GROUPS * D_STATE
N_MLA_HEADS = 16
D_NOPE = 64
D_ROPE = 32
D_V = 64
Q_LORA = 512
KV_LORA = 256
D_MLA = N_MLA_HEADS * D_V
MLA_SCALE = (D_NOPE + D_ROPE) ** -0.5
ROPE_THETA = 10000.0
N_MEM = 256
MEM_HEADS = 4
MEM_HEAD_DIM = 256
D_MEM = MEM_HEADS * MEM_HEAD_DIM
ALPHA = (2 * DEPTH) ** 0.25
EPS = 1e-5
NEG_INF = -1e30
LOG2E = math.log2(math.e)

LANES = 128
HEAD_PAD = 128
VMEM_LIMIT = 56 * 1024 * 1024

COL_GATES = 0
COL_GMLA = 3072
COL_Z = 4096
COL_QMEM = 6144
N_WIDE = 7168
COL_QA = 0
COL_KVA = 512
COL_MISC_A = 768
COL_MISC_B = 896
N_NARROW = 1024
N_PROJ = CONV_DIM + N_WIDE + N_NARROW


def _cparams(sem):
    return pltpu.CompilerParams(dimension_semantics=sem, vmem_limit_bytes=VMEM_LIMIT)


def _const_spec(shape):
    nd = len(shape)
    return pl.BlockSpec(shape, lambda *_: (0,) * nd)


def _layer_spec(shape, layer):
    nd = len(shape)
    return pl.BlockSpec((None,) + tuple(shape), lambda *_: (layer,) + (0,) * nd, pipeline_mode=pl.Buffered(1))


def _silu(x):
    return x * jax.nn.sigmoid(x)


def _softplus(x):
    return jnp.maximum(x, 0.0) + jnp.log1p(jnp.exp(-jnp.abs(x)))


def _dot(a, b):
    return jnp.dot(a, b, preferred_element_type=F32)


def _dot_nt(a, b):
    return lax.dot_general(a, b, (((1,), (1,)), ((), ())), preferred_element_type=F32)


def _layer_norm(x, g, b):
    mu = jnp.mean(x, axis=-1, keepdims=True)
    xc = x - mu
    var = jnp.mean(xc * xc, axis=-1, keepdims=True)
    return xc * lax.rsqrt(var + EPS) * g + b


def _rms_norm(x, w):
    return x * lax.rsqrt(jnp.mean(x * x, axis=-1, keepdims=True) + EPS) * w


def _ln_body(x_ref, g_ref, b_ref, o32_ref, o16_ref):
    y = _layer_norm(x_ref[...], g_ref[...], b_ref[...])
    o32_ref[...] = y
    o16_ref[...] = y.astype(BF16)


def _ln_call(x, g, b, tm):
    t, d = x.shape
    assert t % tm == 0, (t, tm)
    return pl.pallas_call(
        _ln_body,
        grid=(t // tm,),
        in_specs=[pl.BlockSpec((tm, d), lambda i: (i, 0)), _const_spec((1, d)), _const_spec((1, d))],
        out_specs=[pl.BlockSpec((tm, d), lambda i: (i, 0)), pl.BlockSpec((tm, d), lambda i: (i, 0))],
        out_shape=[jax.ShapeDtypeStruct((t, d), F32), jax.ShapeDtypeStruct((t, d), BF16)],
        compiler_params=_cparams(("parallel",)),
        name="ln_in",
    )(x, g.reshape(1, d), b.reshape(1, d))


def _mm_body(x_ref, w_ref, o_ref):
    o_ref[...] = _dot(x_ref[...], w_ref[...]).astype(o_ref.dtype)


def _mm_call(x, w, layer, tm, tn, name):
    m, k = x.shape
    n = w.shape[2]
    assert m % tm == 0 and n % tn == 0, (m, n, tm, tn)
    return pl.pallas_call(
        _mm_body,
        grid=(m // tm, n // tn),
        in_specs=[pl.BlockSpec((tm, k), lambda i, j: (i, 0)),
                  pl.BlockSpec((None, k, tn), lambda i, j: (layer, 0, j))],
        out_specs=pl.BlockSpec((tm, tn), lambda i, j: (i, j)),
        out_shape=jax.ShapeDtypeStruct((m, n), F32),
        compiler_params=_cparams(("parallel", "parallel")),
        name=name,
    )(x, w)


def _causal_conv(x, t8, cw, cb):
    acc = cb + cw[3:4, :] * x
    for j in range(1, CONV_W):
        acc = acc + cw[3 - j:4 - j, :] * pltpu.roll(x, j, 0)
    x8 = x[0:8, :]
    row8 = lax.broadcasted_iota(jnp.int32, x8.shape, 0)
    acc8 = cb + cw[3:4, :] * x8
    for j in range(1, CONV_W):
        xj = jnp.where(row8 < j, pltpu.roll(t8, j, 0), pltpu.roll(x8, j, 0))
        acc8 = acc8 + cw[3 - j:4 - j, :] * xj
    return jnp.concatenate([acc8, acc[8:, :]], axis=0)


def _inproj_body(x_ref, w_ref, xo_ref, pa_ref, pb_ref, *, n_conv, n_a):
    j = pl.program_id(1)

    @pl.when(j < n_conv)
    def _():
        xo_ref[...] = _dot(x_ref[...], w_ref[...]).astype(xo_ref.dtype)

    @pl.when(jnp.logical_and(j >= n_conv, j < n_conv + n_a))
    def _():
        pa_ref[...] = _dot(x_ref[...], w_ref[...]).astype(pa_ref.dtype)

    @pl.when(j >= n_conv + n_a)
    def _():
        pb_ref[...] = _dot(x_ref[...], w_ref[...])


def _inproj_call(x, w, layer, *, tm, tn, wide_dtype, name):
    t, k = x.shape
    assert t % tm == 0 and CONV_DIM % tn == 0 and N_WIDE % tn == 0 and N_NARROW % tn == 0, (t, tm, tn)
    n_conv = CONV_DIM // tn
    n_a = N_WIDE // tn
    n_b = N_NARROW // tn
    body = functools.partial(_inproj_body, n_conv=n_conv, n_a=n_a)
    return pl.pallas_call(
        body,
        grid=(t // tm, n_conv + n_a + n_b),
        in_specs=[pl.BlockSpec((tm, k), lambda i, j: (i, 0)),
                  pl.BlockSpec((None, k, tn), lambda i, j: (layer, 0, j))],
        out_specs=[pl.BlockSpec((tm, tn), lambda i, j: (i, jnp.minimum(j, n_conv - 1))),
                   pl.BlockSpec((tm, tn), lambda i, j: (i, jnp.clip(j - n_conv, 0, n_a - 1))),
                   pl.BlockSpec((tm, tn), lambda i, j: (i, jnp.maximum(j - n_conv - n_a, 0)))],
        out_shape=[jax.ShapeDtypeStruct((t, CONV_DIM), wide_dtype),
                   jax.ShapeDtypeStruct((t, N_WIDE), wide_dtype),
                   jax.ShapeDtypeStruct((t, N_NARROW), F32)],
        compiler_params=_cparams(("parallel", "arbitrary")),
        name=name,
    )(x, w['w_proj'])


def _cumsum_rows(x, t):
    row = lax.broadcasted_iota(jnp.int32, x.shape, 0)
    s = x
    k = 1
    while k < t:
        s = s + jnp.where(row >= k, pltpu.roll(s, k, 0), 0.0)
        k *= 2
    return s


def _expand_heads(w, e_ref):
    hi = w.astype(BF16)
    lo = (w - hi.astype(F32)).astype(BF16)
    e = e_ref[...]
    return _dot(hi, e) + _dot(lo, e)


def _ssm_body(*refs, rb, tc, has_prev):
    if has_prev:
        (xbc_ref, z_ref, misc_ref, tail0_ref, s0_ref, cw_ref, cb_ref, dtb_ref, alog_ref, dskip_ref, nw_ref, e_ref,
         y_ref, sfin_ref, tail_ref, cbuf, ybuf, dabuf, dtbuf, state, tail) = refs
    else:
        (xbc_ref, z_ref, misc_ref, cw_ref, cb_ref, dtb_ref, alog_ref, dskip_ref, nw_ref, e_ref,
         y_ref, sfin_ref, tail_ref, cbuf, ybuf, dabuf, dtbuf, state, tail) = refs
    r = pl.program_id(1)
    nr = pl.num_programs(1)
    pair_w = 2 * SSM_HEAD_DIM

    @pl.when(r == 0)
    def _():
        if has_prev:
            for hp in range(N_SSM_HEADS // 2):
                blk = s0_ref[0, 2 * hp:2 * hp + 2].reshape(pair_w, D_STATE)
                state[:, hp * pair_w:(hp + 1) * pair_w] = blk.T
            tail[...] = tail0_ref[0]
        else:
            state[...] = jnp.zeros(state.shape, F32)
            tail[...] = jnp.zeros(tail.shape, F32)

    x = xbc_ref[...].astype(F32)
    cbuf[...] = _silu(_causal_conv(x, tail[...], cw_ref[...], cb_ref[...]))
    tail[...] = x[rb - 8:rb, :]

    dt = _softplus(misc_ref[...] + dtb_ref[...])
    dtbuf[...] = dt
    dabuf[...] = dt * (-jnp.exp(alog_ref[...]) * LOG2E)

    ri = lax.broadcasted_iota(jnp.int32, (tc, tc), 0)
    ci = lax.broadcasted_iota(jnp.int32, (tc, tc), 1)
    lower = ri >= ci
    pad_rows = LANES - tc

    def to_rows(a):
        if pad_rows:
            a = jnp.concatenate([a, jnp.zeros((pad_rows, LANES), F32)], axis=0)
        return a.T[:, :tc]

    def chunk(c, carry):
        r0 = pl.multiple_of(c * tc, tc)
        rows = pl.ds(r0, tc)
        da = dabuf[rows, :]
        dtc = dtbuf[rows, :]
        acs = _cumsum_rows(da, tc)
        acs_t = to_rows(acs)
        dt_t = to_rows(dtc)
        total = acs[tc - 1:tc, :]
        wexp = _expand_heads(jnp.exp2(total - acs) * dtc, e_ref)
        etot = _expand_heads(jnp.broadcast_to(jnp.exp2(total), (8, LANES)), e_ref)[0:1, :]
        xs = cbuf[rows, 0:D_SSM]
        xs16 = xs.astype(BF16)
        xw16 = (xs * wexp).astype(BF16)
        for g in range(SSM_GROUPS):
            bg = cbuf[rows, D_SSM + g * D_STATE:D_SSM + (g + 1) * D_STATE]
            cg = cbuf[rows, D_SSM + SSM_GROUPS * D_STATE + g * D_STATE:D_SSM + SSM_GROUPS * D_STATE + (g + 1) * D_STATE]
            bg16 = bg.astype(BF16)
            cg16 = cg.astype(BF16)
            cbm = _dot_nt(cg16, bg16)
            gcols = slice(g * HEADS_PER_GROUP * SSM_HEAD_DIM, (g + 1) * HEADS_PER_GROUP * SSM_HEAD_DIM)
            s_old = state[:, gcols]
            s16 = s_old.astype(BF16)
            for hh in range(0, HEADS_PER_GROUP, 2):
                pair = []
                for h in (g * HEADS_PER_GROUP + hh, g * HEADS_PER_GROUP + hh + 1):
                    colb = jnp.broadcast_to(acs[:, h:h + 1], (tc, LANES))
                    rowb = jnp.broadcast_to(acs_t[h:h + 1, :], (tc, tc))
                    dtb = jnp.broadcast_to(dt_t[h:h + 1, :], (tc, tc))
                    lmat = jnp.where(lower, jnp.exp2(colb[:, :tc] - rowb), 0.0)
                    m16 = (cbm * lmat * dtb).astype(BF16)
                    cexp16 = (cg * jnp.exp2(colb)).astype(BF16)
                    hl = h - g * HEADS_PER_GROUP
                    xh = xs16[:, h * SSM_HEAD_DIM:(h + 1) * SSM_HEAD_DIM]
                    sh = s16[:, hl * SSM_HEAD_DIM:(hl + 1) * SSM_HEAD_DIM]
                    if tc % LANES == 0:
                        yh = _dot(jnp.concatenate([m16, cexp16], axis=1), jnp.concatenate([xh, sh], axis=0))
                    else:
                        yh = _dot(m16, xh) + _dot(cexp16, sh)
                    pair.append(yh)
                h0 = g * HEADS_PER_GROUP + hh
                ybuf[rows, h0 * SSM_HEAD_DIM:(h0 + 2) * SSM_HEAD_DIM] = jnp.concatenate(pair, axis=1)
            bg_t = to_rows(bg).astype(BF16)
            state[:, gcols] = s_old * etot[:, gcols] + _dot(bg_t, xw16[:, gcols])
        y = ybuf[rows, :] + dskip_ref[...] * xs
        zc = z_ref[rows, :].astype(F32)
        gt = y * _silu(zc)
        gw = D_SSM // SSM_GROUPS
        for g in range(SSM_GROUPS):
            gg = gt[:, g * gw:(g + 1) * gw]
            ms = jnp.mean(gg * gg, axis=-1, keepdims=True)
            y_ref[rows, g * gw:(g + 1) * gw] = (gg * lax.rsqrt(ms + EPS) * nw_ref[:, g * gw:(g + 1) * gw]).astype(BF16)
        return carry

    lax.fori_loop(0, rb // tc, chunk, 0)

    @pl.when(r == nr - 1)
    def _():
        for hp in range(N_SSM_HEADS // 2):
            blk = state[:, hp * pair_w:(hp + 1) * pair_w].T
            sfin_ref[0, 2 * hp:2 * hp + 2] = blk.reshape(2, SSM_HEAD_DIM, D_STATE)
        tail_ref[0] = tail[...]


def _ssm_call(xc, pa, pb, prev, w, emat, layer, *, batch, seq, rb, tc):
    t = batch * seq
    assert seq % rb == 0 and rb % tc == 0 and rb % 8 == 0, (seq, rb, tc)
    nrb = seq // rb
    row = lambda b, r: b * nrb + r
    has_prev = prev is not None
    body = functools.partial(_ssm_body, rb=rb, tc=tc, has_prev=has_prev)
    state_block = (1, N_SSM_HEADS, SSM_HEAD_DIM, D_STATE)
    prev_specs = []
    prev_args = []
    if has_prev:
        prev_specs = [pl.BlockSpec((1, 8, CONV_DIM), lambda b, r: (b, 0, 0)),
                      pl.BlockSpec((None,) + state_block, lambda b, r: (layer, b, 0, 0, 0))]
        prev_args = list(prev)
    return pl.pallas_call(
        body,
        grid=(batch, nrb),
        in_specs=[
            pl.BlockSpec((rb, CONV_DIM), lambda b, r: (row(b, r), 0)),
            pl.BlockSpec((rb, D_SSM), lambda b, r: (row(b, r), COL_Z // D_SSM)),
            pl.BlockSpec((rb, LANES), lambda b, r: (row(b, r), COL_MISC_A // LANES)),
            *prev_specs,
            _layer_spec((CONV_W, CONV_DIM), layer),
            _layer_spec((1, CONV_DIM), layer),
            _layer_spec((1, LANES), layer),
            _layer_spec((1, LANES), layer),
            _layer_spec((1, D_SSM), layer),
            _layer_spec((1, D_SSM), layer),
            _const_spec((LANES, D_SSM)),
        ],
        out_specs=[
            pl.BlockSpec((rb, D_SSM), lambda b, r: (row(b, r), 0)),
            pl.BlockSpec(state_block, lambda b, r: (b, 0, 0, 0)),
            pl.BlockSpec((1, 8, CONV_DIM), lambda b, r: (b, 0, 0)),
        ],
        out_shape=[
            jax.ShapeDtypeStruct((t, D_SSM), BF16),
            jax.ShapeDtypeStruct((batch,) + state_block[1:], F32),
            jax.ShapeDtypeStruct((batch, 8, CONV_DIM), F32),
        ],
        scratch_shapes=[
            pltpu.VMEM((rb, CONV_DIM), F32),
            pltpu.VMEM((rb, D_SSM), F32),
            pltpu.VMEM((rb, LANES), F32),
            pltpu.VMEM((rb, LANES), F32),
            pltpu.VMEM((D_STATE, D_SSM), F32),
            pltpu.VMEM((8, CONV_DIM), F32),
        ],
        compiler_params=_cparams(("parallel", "arbitrary")),
        name="ssd_mixer",
    )(xc, pa, pb, *prev_args, w['cw'], w['cb'], w['dtb'], w['alog'], w['dskip'], w['nw'], emat)


def _mla_prep_body(qa_ref, kva_ref, ma_ref, mb_ref, qnw_ref, kvnw_ref, wq1_ref, wq2_ref, wk_ref, wv_ref,
                   cq_ref, sq_ref, ck_ref, sk_ref, q_ref, k_ref, v_ref, ckv_ref, kr_ref):
    qa16 = _rms_norm(qa_ref[...], qnw_ref[...]).astype(BF16)
    q1 = _dot(qa16, wq1_ref[...])
    q2 = _dot(qa16, wq2_ref[...])
    cq = jnp.tile(cq_ref[...], (1, N_MLA_HEADS))
    sq = jnp.tile(sq_ref[...], (1, N_MLA_HEADS))
    q_ref[...] = (q1 * cq + q2 * sq).astype(BF16)
    ckv = _rms_norm(kva_ref[...], kvnw_ref[...])
    ckv_ref[...] = ckv
    ckv16 = ckv.astype(BF16)
    kr = ma_ref[...] * ck_ref[...] + mb_ref[...] * sk_ref[...]
    kr_ref[...] = kr
    k_ref[...] = (_dot(ckv16, wk_ref[...]) + jnp.tile(kr, (1, N_MLA_HEADS))).astype(BF16)
    lane = lax.broadcasted_iota(jnp.int32, (1, N_MLA_HEADS * HEAD_PAD), 1)
    ones_col = jnp.where(lane % HEAD_PAD == D_V, 1.0, 0.0)
    v_ref[...] = (_dot(ckv16, wv_ref[...]) + ones_col).astype(BF16)


def _mla_prep_call(proj, w, cq, sq, ck, sk, layer, *, seq, tm):
    t = proj.shape[0]
    assert seq % tm == 0 and t % seq == 0, (t, seq, tm)
    npos = seq // tm
    hp = N_MLA_HEADS * HEAD_PAD
    tab = pl.BlockSpec((tm, LANES), lambda i: (i % npos, 0))
    return pl.pallas_call(
        _mla_prep_body,
        grid=(t // tm,),
        in_specs=[
            pl.BlockSpec((tm, Q_LORA), lambda i: (i, COL_QA // Q_LORA)),
            pl.BlockSpec((tm, KV_LORA), lambda i: (i, COL_KVA // KV_LORA)),
            pl.BlockSpec((tm, LANES), lambda i: (i, COL_MISC_A // LANES)),
            pl.BlockSpec((tm, LANES), lambda i: (i, COL_MISC_B // LANES)),
            _layer_spec((1, Q_LORA), layer), _layer_spec((1, KV_LORA), layer),
            _layer_spec((Q_LORA, hp), layer), _layer_spec((Q_LORA, hp), layer),
            _layer_spec((KV_LORA, hp), layer), _layer_spec((KV_LORA, hp), layer),
            tab, tab, tab, tab,
        ],
        out_specs=[
            pl.BlockSpec((tm, hp), lambda i: (i, 0)),
            pl.BlockSpec((tm, hp), lambda i: (i, 0)),
            pl.BlockSpec((tm, hp), lambda i: (i, 0)),
            pl.BlockSpec((tm, KV_LORA), lambda i: (i, 0)),
            pl.BlockSpec((tm, LANES), lambda i: (i, 0)),
        ],
        out_shape=[
            jax.ShapeDtypeStruct((t, hp), BF16),
            jax.ShapeDtypeStruct((t, hp), BF16),
            jax.ShapeDtypeStruct((t, hp), BF16),
            jax.ShapeDtypeStruct((t, KV_LORA), F32),
            jax.ShapeDtypeStruct((t, LANES), F32),
        ],
        compiler_params=_cparams(("parallel",)),
        name="mla_prep",
    )(proj, proj, proj, proj, w['qnw'], w['kvnw'], w['wq1'], w['wq2'], w['wk'], w['wv'], cq, sq, ck, sk)


def _attn_body(qi_ref, kj_ref, q_ref, k_ref, v_ref, o_ref, m_ref, acc_ref, s_ref, *, tq, tk):
    t = pl.program_id(1)
    i = qi_ref[t]
    j = kj_ref[t]

    @pl.when(j == 0)
    def _():
        m_ref[...] = jnp.full(m_ref.shape, NEG_INF, F32)
        acc_ref[...] = jnp.zeros(acc_ref.shape, F32)

    def scores(h):
        hs = slice(h * HEAD_PAD, (h + 1) * HEAD_PAD)
        s_ref[h % 2] = _dot_nt(q_ref[0, :, hs], k_ref[0, :, hs])

    def step(bias):
        scores(0)
        for h in range(N_MLA_HEADS):
            hs = slice(h * HEAD_PAD, (h + 1) * HEAD_PAD)
            if h + 1 < N_MLA_HEADS:
                scores(h + 1)
            s = s_ref[h % 2]
            if bias is not None:
                s = s + bias
            m_prev = m_ref[h]
            m_new = jnp.maximum(m_prev, jnp.max(s, axis=-1, keepdims=True))
            p = jnp.exp2(s - jnp.tile(m_new, (1, tk // LANES)))
            acc_ref[h] = jnp.exp2(m_prev - m_new) * acc_ref[h] + _dot(p.astype(BF16), v_ref[0, :, hs])
            m_ref[h] = m_new

    @pl.when(j < i)
    def _():
        step(None)

    @pl.when(j == i)
    def _():
        row = lax.broadcasted_iota(jnp.int32, (tq, tk), 0)
        col = lax.broadcasted_iota(jnp.int32, (tq, tk), 1)
        step(jnp.where(col < ((row // CHUNK + 1) * CHUNK), 0.0, NEG_INF))
        for h in range(0, N_MLA_HEADS, 2):
            a = acc_ref[h]
            b = acc_ref[h + 1]
            a = a[:, :D_V] / a[:, D_V:D_V + 1]
            b = b[:, :D_V] / b[:, D_V:D_V + 1]
            o_ref[0, :, h * D_V:(h + 2) * D_V] = jnp.concatenate([a, b], axis=1)


def _attn_call(q, k, v, *, batch, seq, tq):
    hp = N_MLA_HEADS * HEAD_PAD
    q = q.reshape(batch, seq, hp)
    k = k.reshape(batch, seq, hp)
    v = v.reshape(batch, seq, hp)
    assert seq % tq == 0 and tq % CHUNK == 0 and tq % LANES == 0, (seq, tq)
    nq = seq // tq
    pairs = [(i, j) for i in range(nq) for j in range(i + 1)]
    qi = jnp.asarray([p[0] for p in pairs], jnp.int32)
    kj = jnp.asarray([p[1] for p in pairs], jnp.int32)
    kv_spec = pl.BlockSpec((1, tq, hp), lambda b, t, qi, kj: (b, kj[t], 0))
    body = functools.partial(_attn_body, tq=tq, tk=tq)
    out = pl.pallas_call(
        body,
        grid_spec=pltpu.PrefetchScalarGridSpec(
            num_scalar_prefetch=2,
            grid=(batch, len(pairs)),
            in_specs=[pl.BlockSpec((1, tq, hp), lambda b, t, qi, kj: (b, qi[t], 0)), kv_spec, kv_spec],
            out_specs=pl.BlockSpec((1, tq, D_MLA), lambda b, t, qi, kj: (b, qi[t], 0)),
            scratch_shapes=[
                pltpu.VMEM((N_MLA_HEADS, tq, LANES), F32),
                pltpu.VMEM((N_MLA_HEADS, tq, LANES), F32),
                pltpu.VMEM((2, tq, tq), F32),
            ],
        ),
        out_shape=jax.ShapeDtypeStruct((batch, seq, D_MLA), F32),
        compiler_params=_cparams(("parallel", "arbitrary")),
        name="mla_attn",
    )(qi, kj, q, k, v)
    return out.reshape(batch * seq, D_MLA)


def _mla_prep_s_body(qa_ref, kva_ref, ma_ref, mb_ref, qnw_ref, kvnw_ref, wqn_ref, wqr_ref, wqr2_ref, wukt_ref,
                     cq_ref, sq_ref, ck_ref, sk_ref, qlat_ref, qr_ref, ckv_ref, kr_ref):
    qa16 = _rms_norm(qa_ref[...], qnw_ref[...]).astype(BF16)
    qn16 = _dot(qa16, wqn_ref[...]).astype(BF16)
    for h in range(N_MLA_HEADS):
        qh = qn16[:, h * D_NOPE:(h + 1) * D_NOPE]
        qlat_ref[:, h * KV_LORA:(h + 1) * KV_LORA] = (_dot(qh, wukt_ref[h]) * MLA_SCALE).astype(BF16)
    qr = _dot(qa16, wqr_ref[...]) * cq_ref[...] + _dot(qa16, wqr2_ref[...]) * sq_ref[...]
    qr_ref[...] = qr.astype(BF16)
    ckv_ref[...] = _rms_norm(kva_ref[...], kvnw_ref[...])
    kr_ref[...] = ma_ref[...] * ck_ref[...] + mb_ref[...] * sk_ref[...]


def _mla_prep_s_call(proj, w, cq, sq, ck, sk, layer):
    t = proj.shape[0]
    nr = N_MLA_HEADS * D_ROPE
    return pl.pallas_call(
        _mla_prep_s_body,
        grid=(1,),
        in_specs=[
            pl.BlockSpec((t, Q_LORA), lambda i: (0, COL_QA // Q_LORA)),
            pl.BlockSpec((t, KV_LORA), lambda i: (0, COL_KVA // KV_LORA)),
            pl.BlockSpec((t, LANES), lambda i: (0, COL_MISC_A // LANES)),
            pl.BlockSpec((t, LANES), lambda i: (0, COL_MISC_B // LANES)),
            _layer_spec((1, Q_LORA), layer), _layer_spec((1, KV_LORA), layer),
            _layer_spec((Q_LORA, D_MLA), layer), _layer_spec((Q_LORA, nr), layer), _layer_spec((Q_LORA, nr), layer),
            _layer_spec((N_MLA_HEADS, D_NOPE, KV_LORA), layer),
            _const_spec((t, nr)), _const_spec((t, nr)), _const_spec((t, LANES)), _const_spec((t, LANES)),
        ],
        out_specs=[
            _const_spec((t, N_MLA_HEADS * KV_LORA)),
            _const_spec((t, nr)),
            _const_spec((t, KV_LORA)),
            _const_spec((t, LANES)),
        ],
        out_shape=[
            jax.ShapeDtypeStruct((t, N_MLA_HEADS * KV_LORA), BF16),
            jax.ShapeDtypeStruct((t, nr), BF16),
            jax.ShapeDtypeStruct((t, KV_LORA), F32),
            jax.ShapeDtypeStruct((t, LANES), F32),
        ],
        compiler_params=_cparams(("arbitrary",)),
        name="mla_prep_dec",
    )(proj, proj, proj, proj, w['qnw'], w['kvnw'], w['wqn'], w['wqr'], w['wqr2'], w['wukt'], cq, sq, ck, sk)


def _attn_s_body(ql_ref, qr_ref, cc_ref, ckr_ref, nc_ref, nkr_ref, wuv_ref, o_ref, m_ref, l_ref, acc_ref, *, lq):
    j = pl.program_id(1)
    nj = pl.num_programs(1)
    ql = ql_ref[0]
    qr = qr_ref[0]

    @pl.when(j == 0)
    def _():
        m_ref[...] = jnp.full(m_ref.shape, NEG_INF, F32)
        l_ref[...] = jnp.zeros(l_ref.shape, F32)
        acc_ref[...] = jnp.zeros(acc_ref.shape, F32)

    def update(c16, kr16):
        s = _dot_nt(ql, c16) + _dot_nt(qr, kr16)
        m_prev = m_ref[...]
        m_new = jnp.maximum(m_prev, jnp.max(s, axis=-1, keepdims=True))
        alpha = jnp.exp(m_prev - m_new)
        p = jnp.exp(s - m_new)
        l_ref[...] = alpha * l_ref[...] + jnp.sum(p, axis=-1, keepdims=True)
        acc_ref[...] = alpha * acc_ref[...] + _dot(p.astype(BF16), c16)
        m_ref[...] = m_new

    update(cc_ref[0].astype(BF16), ckr_ref[0].astype(BF16))

    @pl.when(j == nj - 1)
    def _():
        update(nc_ref[0].astype(BF16), nkr_ref[0].astype(BF16))
        olat = (acc_ref[...] / l_ref[...]).astype(BF16)
        for h in range(N_MLA_HEADS):
            o_ref[0, h * lq:(h + 1) * lq, :] = _dot(olat[h * lq:(h + 1) * lq, :], wuv_ref[h])


def _attn_s_call(qlat, qr, cache_ckv, cache_kr, new_ckv, new_kr, wuv, layer, *, tk):
    b, rows, _ = qlat.shape
    lq = rows // N_MLA_HEADS
    past = cache_ckv.shape[2]
    assert past % tk == 0, (past, tk)
    body = functools.partial(_attn_s_body, lq=lq)
    return pl.pallas_call(
        body,
        grid=(b, past // tk),
        in_specs=[
            pl.BlockSpec((1, rows, KV_LORA), lambda i, j: (i, 0, 0)),
            pl.BlockSpec((1, rows, D_ROPE), lambda i, j: (i, 0, 0)),
            pl.BlockSpec((None, 1, tk, KV_LORA), lambda i, j: (layer, i, j, 0)),
            pl.BlockSpec((None, 1, tk, D_ROPE), lambda i, j: (layer, i, j, 0)),
            pl.BlockSpec((1, lq, KV_LORA), lambda i, j: (i, 0, 0)),
            pl.BlockSpec((1, lq, D_ROPE), lambda i, j: (i, 0, 0)),
            _layer_spec((N_MLA_HEADS, KV_LORA, D_V), layer),
        ],
        out_specs=pl.BlockSpec((1, rows, D_V), lambda i, j: (i, 0, 0)),
        out_shape=jax.ShapeDtypeStruct((b, rows, D_V), F32),
        scratch_shapes=[
            pltpu.VMEM((rows, 1), F32),
            pltpu.VMEM((rows, 1), F32),
            pltpu.VMEM((rows, KV_LORA), F32),
        ],
        compiler_params=_cparams(("parallel", "arbitrary")),
        name="mla_attn_dec",
    )(qlat, qr, cache_ckv, cache_kr, new_ckv, new_kr, wuv)


def _mem_attn_body(q_ref, mk_ref, mv_ref, o_ref, *, head_split):
    scale = MEM_HEAD_DIM ** -0.5
    for h in range(MEM_HEADS):
        hs = slice(h * MEM_HEAD_DIM, (h + 1) * MEM_HEAD_DIM)
        if head_split:
            kh = mk_ref[0, :, h, :]
            vh = mv_ref[0, :, h, :]
        else:
            kh = mk_ref[0, :, hs]
            vh = mv_ref[0, :, hs]
        s = _dot_nt(q_ref[:, hs].astype(BF16), kh.astype(BF16)) * scale
        p = jnp.exp(s - jnp.max(s, axis=-1, keepdims=True))
        l = jnp.sum(p, axis=-1, keepdims=True)
        o_ref[:, hs] = (_dot(p.astype(BF16), vh.astype(BF16)) / l).astype(BF16)


def _mem_attn_call(proj, mk, mv, layer, *, batch, seq, tm):
    t = batch * seq
    assert seq % tm == 0, (seq, tm)
    nrb = seq // tm
    head_split = mk.ndim == 5
    if head_split:
        mem_spec = pl.BlockSpec((None, 1, N_MEM, MEM_HEADS, MEM_HEAD_DIM), lambda b, r: (layer, b, 0, 0, 0))
    else:
        mem_spec = pl.BlockSpec((1, N_MEM, D_MEM), lambda b, r: (b, 0, 0))
    return pl.pallas_call(
        functools.partial(_mem_attn_body, head_split=head_split),
        grid=(batch, nrb),
        in_specs=[
            pl.BlockSpec((tm, D_MEM), lambda b, r: (b * nrb + r, COL_QMEM // D_MEM)),
            mem_spec,
            mem_spec,
        ],
        out_specs=pl.BlockSpec((tm, D_MEM), lambda b, r: (b * nrb + r, 0)),
        out_shape=jax.ShapeDtypeStruct((t, D_MEM), BF16),
        compiler_params=_cparams(("parallel", "parallel")),
        name="mem_attn",
    )(proj, mk, mv)


def _merge_body(y_ref, o_ref, g_ref, om_ref, gates_ref, x_ref, wos_ref, woa_ref, wom_ref, wout_ref,
                lng_ref, lnb_ref, x32_ref, x16_ref):
    ys = _dot(y_ref[...], wos_ref[...])
    ya = _dot((o_ref[...] * _silu(g_ref[...].astype(F32))).astype(BF16), woa_ref[...])
    ym = _dot(om_ref[...], wom_ref[...])
    gs = jax.nn.sigmoid(gates_ref[...].astype(F32))
    h = gs[:, 0:D_MODEL] * ys + gs[:, D_MODEL:2 * D_MODEL] * ya + gs[:, 2 * D_MODEL:3 * D_MODEL] * ym
    out = _dot(h.astype(BF16), wout_ref[...])
    xn = _layer_norm(ALPHA * x_ref[...] + out, lng_ref[...], lnb_ref[...])
    x32_ref[...] = xn
    x16_ref[...] = xn.astype(BF16)


def _merge_call(y, o, proj, om, x, w, layer, *, tm):
    t = x.shape[0]
    assert t % tm == 0, (t, tm)
    rowspec = lambda width: pl.BlockSpec((tm, width), lambda i: (i, 0))
    return pl.pallas_call(
        _merge_body,
        grid=(t // tm,),
        in_specs=[
            rowspec(D_SSM), rowspec(D_MLA),
            pl.BlockSpec((tm, D_MLA), lambda i: (i, COL_GMLA // D_MLA)),
            rowspec(D_MEM),
            pl.BlockSpec((tm, 3 * D_MODEL), lambda i: (i, COL_GATES // (3 * D_MODEL))),
            rowspec(D_MODEL),
            _layer_spec((D_SSM, D_MODEL), layer), _layer_spec((D_MLA, D_MODEL), layer),
            _layer_spec((D_MEM, D_MODEL), layer), _layer_spec((D_MODEL, D_MODEL), layer),
            _layer_spec((1, D_MODEL), layer), _layer_spec((1, D_MODEL), layer),
        ],
        out_specs=[rowspec(D_MODEL), rowspec(D_MODEL)],
        out_shape=[jax.ShapeDtypeStruct((t, D_MODEL), F32), jax.ShapeDtypeStruct((t, D_MODEL), BF16)],
        compiler_params=_cparams(("parallel",)),
        name="merge_out",
    )(y, o, proj, om, proj, x, w['wos'], w['woa'], w['wom'], w['wout'], w['lng'], w['lnb'])


def _rot_cols(w):
    half = D_ROPE // 2
    return jnp.concatenate([-w[..., half:], w[..., :half]], axis=-1)


def _pack_weights(p):
    w_in = p['w_in']
    seg = {}
    off = 0
    for name, width in (('z', D_SSM), ('xbc', CONV_DIM), ('dt', N_SSM_HEADS), ('qa', Q_LORA), ('kva', KV_LORA),
                        ('kr', D_ROPE), ('gmla', D_MLA), ('qmem', D_MEM), ('gates', 3 * D_MODEL)):
        seg[name] = w_in[:, :, off:off + width].astype(BF16)
        off += width
    zc = lambda n: jnp.zeros((DEPTH, D_MODEL, n), BF16)
    w_proj = jnp.concatenate([
        seg['xbc'], seg['gates'], seg['gmla'], seg['z'], seg['qmem'], seg['qa'], seg['kva'],
        seg['dt'], zc(32), seg['kr'], zc(32),
        zc(64), _rot_cols(seg['kr']), zc(32),
    ], axis=2)

    wqb = p['w_q_b'].astype(BF16).reshape(DEPTH, Q_LORA, N_MLA_HEADS, D_NOPE + D_ROPE)
    wq_n = wqb[..., :D_NOPE]
    wq_r = wqb[..., D_NOPE:]
    zq = jnp.zeros((DEPTH, Q_LORA, N_MLA_HEADS, HEAD_PAD - D_NOPE - D_ROPE), BF16)
    flat = lambda a: a.reshape(DEPTH, a.shape[1], -1)
    wq1 = flat(jnp.concatenate([wq_n, wq_r, zq], axis=-1))
    wq2 = flat(jnp.concatenate([jnp.zeros_like(wq_n), _rot_cols(wq_r), zq], axis=-1))
    wuk = p['w_uk'].astype(BF16)
    wuv = p['w_uv'].astype(BF16)
    zk = jnp.zeros((DEPTH, KV_LORA, N_MLA_HEADS, HEAD_PAD - D_NOPE), BF16)
    wk = flat(jnp.concatenate([wuk, zk], axis=-1))
    wv = flat(jnp.concatenate([wuv, zk], axis=-1))

    pad128 = lambda v: jnp.pad(v, ((0, 0), (0, LANES - v.shape[1]))).reshape(DEPTH, 1, LANES)
    row = lambda v: v.reshape(DEPTH, 1, v.shape[-1])
    return dict(
        w_proj=w_proj,
        cw=p['conv_w'], cb=row(p['conv_b']),
        dtb=pad128(p['dt_bias']), alog=pad128(p['a_log']),
        dskip=row(jnp.repeat(p['d_skip'], SSM_HEAD_DIM, axis=1)),
        nw=row(p['ssm_norm_w']),
        qnw=row(p['q_a_norm_w']), kvnw=row(p['kv_norm_w']),
        wq1=wq1, wq2=wq2, wk=wk, wv=wv,
        wqn=flat(wq_n), wqr=flat(wq_r), wqr2=flat(_rot_cols(wq_r)),
        wukt=jnp.transpose(wuk, (0, 2, 3, 1)),
        wuv=jnp.transpose(wuv, (0, 2, 1, 3)),
        wmk=p['w_mem_k'].astype(BF16), wmv=p['w_mem_v'].astype(BF16),
        wos=p['w_o_ssm'].astype(BF16), woa=p['w_o_mla'].astype(BF16),
        wom=p['w_o_mem'].astype(BF16), wout=p['w_out'].astype(BF16),
        lng=row(p['ln_g']), lnb=row(p['ln_b']),
    )


def _rope_tables(pos):
    half = D_ROPE // 2
    inv = ROPE_THETA ** (-jnp.arange(half, dtype=F32) / half)
    ang = pos.astype(F32)[:, None] * inv[None, :]
    cos = jnp.cos(ang)
    sin = jnp.sin(ang)
    return jnp.concatenate([cos, cos], axis=-1), jnp.concatenate([sin, sin], axis=-1)


def _expansion_matrix():
    e = np.zeros((LANES, D_SSM), np.float32)
    for h in range(N_SSM_HEADS):
        e[h, h * SSM_HEAD_DIM:(h + 1) * SSM_HEAD_DIM] = 1.0
    return jnp.asarray(e, dtype=BF16)


def kernel(x_prompt, x_sample, mem_prompt, state_ssm, state_conv, cache_ckv, cache_krope, cache_mem_k, cache_mem_v,
           ln_in_g, ln_in_b, w_in, conv_w, conv_b, dt_bias, a_log, d_skip, ssm_norm_w, w_o_ssm, q_a_norm_w, w_q_b,
           kv_norm_w, w_uk, w_uv, w_o_mla, w_mem_k, w_mem_v, w_o_mem, w_out, ln_g, ln_b):
    params = dict(w_in=w_in, conv_w=conv_w, conv_b=conv_b, dt_bias=dt_bias, a_log=a_log, d_skip=d_skip,
                  ssm_norm_w=ssm_norm_w, w_o_ssm=w_o_ssm, q_a_norm_w=q_a_norm_w, w_q_b=w_q_b, kv_norm_w=kv_norm_w,
                  w_uk=w_uk, w_uv=w_uv, w_o_mla=w_o_mla, w_mem_k=w_mem_k, w_mem_v=w_mem_v, w_o_mem=w_o_mem,
                  w_out=w_out, ln_g=ln_g, ln_b=ln_b)
    bp, lp, _ = x_prompt.shape
    bs, ls, _ = x_sample.shape
    past = cache_ckv.shape[2]
    tp = bp * lp
    ts = bs * ls
    assert w_in.shape == (DEPTH, D_MODEL, N_PROJ - 192), w_in.shape
    assert (past // CHUNK + 1) * CHUNK >= past + ls, "decode block would need a causal mask"
    assert ls >= CONV_W - 1 and lp >= CONV_W - 1

    w = _pack_weights(params)
    emat = _expansion_matrix()
    tm_p = min(512, tp)
    xp32, xp16 = _ln_call(x_prompt.reshape(tp, D_MODEL), ln_in_g, ln_in_b, tm_p)
    xs32, xs16 = _ln_call(x_sample.reshape(ts, D_MODEL), ln_in_g, ln_in_b, ts)
    mem16 = mem_prompt.reshape(bp * N_MEM, D_MODEL).astype(BF16)

    cos_p, sin_p = _rope_tables(jnp.arange(lp, dtype=jnp.int32))
    cos_s, sin_s = _rope_tables(past + jnp.arange(ls, dtype=jnp.int32))
    z64 = lambda c: jnp.zeros((c.shape[0], 64), F32)
    z32 = lambda c: jnp.zeros((c.shape[0], 32), F32)
    one64 = jnp.ones((lp, 64), F32)
    cq_p = jnp.concatenate([one64, cos_p, z32(cos_p)], axis=1) * (MLA_SCALE * LOG2E)
    sq_p = jnp.concatenate([z64(sin_p), sin_p, z32(sin_p)], axis=1) * (MLA_SCALE * LOG2E)
    ck_p = jnp.concatenate([z64(cos_p), cos_p, z32(cos_p)], axis=1)
    sk_p = jnp.concatenate([z64(sin_p), sin_p, z32(sin_p)], axis=1)
    cq_s = jnp.tile(jnp.tile(cos_s, (1, N_MLA_HEADS)), (bs, 1)) * MLA_SCALE
    sq_s = jnp.tile(jnp.tile(sin_s, (1, N_MLA_HEADS)), (bs, 1)) * MLA_SCALE
    ck_s = jnp.tile(jnp.concatenate([z64(cos_s), cos_s, z32(cos_s)], axis=1), (bs, 1))
    sk_s = jnp.tile(jnp.concatenate([z64(sin_s), sin_s, z32(sin_s)], axis=1), (bs, 1))

    tail_rows = 8 - (CONV_W - 1)
    rb_p = min(512, lp)
    tc_p = min(128, lp)
    tq = min(512, lp)
    tmm = min(2048, tp)
    tk_s = min(1024, past)

    outs = {k: [] for k in ('p_ssm', 'p_conv', 'p_ckv', 'p_kr', 'p_mk', 'p_mv', 's_ssm', 's_conv', 's_ckv', 's_kr')}
    for i in range(DEPTH):
        xc, pa, pb = _inproj_call(xp16, w, i, tm=tmm, tn=1024, wide_dtype=BF16, name="in_proj")
        y, ssm_new, tail = _ssm_call(xc, pa, pb, None, w, emat, i, batch=bp, seq=lp, rb=rb_p, tc=tc_p)
        q, k, v, ckv, kr = _mla_prep_call(pb, w, cq_p, sq_p, ck_p, sk_p, i, seq=lp, tm=tq)
        o = _attn_call(q, k, v, batch=bp, seq=lp, tq=tq)
        mk = _mm_call(mem16, w['wmk'], i, bp * N_MEM, 1024, "mem_k_proj")
        mv = _mm_call(mem16, w['wmv'], i, bp * N_MEM, 1024, "mem_v_proj")
        om = _mem_attn_call(pa, mk.reshape(bp, N_MEM, D_MEM), mv.reshape(bp, N_MEM, D_MEM), i,
                            batch=bp, seq=lp, tm=tq)
        xp32, xp16 = _merge_call(y, o, pa, om, xp32, w, i, tm=min(512, tp))
        outs['p_ssm'].append(ssm_new)
        outs['p_conv'].append(tail[:, tail_rows:, :])
        outs['p_ckv'].append(ckv.reshape(bp, lp, KV_LORA))
        outs['p_kr'].append(kr[:, 64:64 + D_ROPE].reshape(bp, lp, D_ROPE))
        outs['p_mk'].append(mk.reshape(bp, N_MEM, MEM_HEADS, MEM_HEAD_DIM))
        outs['p_mv'].append(mv.reshape(bp, N_MEM, MEM_HEADS, MEM_HEAD_DIM))

        xc_s, pa_s, pb_s = _inproj_call(xs16, w, i, tm=ts, tn=1024, wide_dtype=F32, name="in_proj_dec")
        tail0 = jnp.pad(state_conv[i], ((0, 0), (tail_rows, 0), (0, 0)))
        y_s, ssm_new_s, tail_s = _ssm_call(xc_s, pa_s, pb_s, (tail0, state_ssm), w, emat, i,
                                           batch=bs, seq=ls, rb=ls, tc=ls)
        qlat, qr, ckv_s, kr_s = _mla_prep_s_call(pb_s, w, cq_s, sq_s, ck_s, sk_s, i)
        kr_s = kr_s[:, 64:64 + D_ROPE]
        to_hq = lambda a, d: jnp.transpose(a.reshape(bs, ls, N_MLA_HEADS, d), (0, 2, 1, 3)).reshape(bs, N_MLA_HEADS * ls, d)
        o_s = _attn_s_call(to_hq(qlat, KV_LORA), to_hq(qr, D_ROPE), cache_ckv, cache_krope,
                           ckv_s.reshape(bs, ls, KV_LORA), kr_s.reshape(bs, ls, D_ROPE), w['wuv'], i, tk=tk_s)
        o_s = jnp.transpose(o_s.reshape(bs, N_MLA_HEADS, ls, D_V), (0, 2, 1, 3)).reshape(ts, D_MLA)
        om_s = _mem_attn_call(pa_s, cache_mem_k, cache_mem_v, i, batch=bs, seq=ls, tm=ls)
        xs32, xs16 = _merge_call(y_s, o_s, pa_s, om_s, xs32, w, i, tm=min(256, ts))
        outs['s_ssm'].append(ssm_new_s)
        outs['s_conv'].append(tail_s[:, tail_rows:, :])
        outs['s_ckv'].append(ckv_s.reshape(bs, ls, KV_LORA))
        outs['s_kr'].append(kr_s.reshape(bs, ls, D_ROPE))

    st = lambda k: jnp.stack(outs[k])
    return (xp32.reshape(bp, lp, D_MODEL), xs32.reshape(bs, ls, D_MODEL), st('p_ssm'), st('p_conv'), st('p_ckv'),
            st('p_kr'), st('p_mk'), st('p_mv'), st('s_ssm'), st('s_conv'), st('s_ckv'), st('s_kr'))
```

```python
import functools
import math

import numpy as np
import jax
import jax.numpy as jnp
from jax import lax
from jax.experimental import pallas as pl
from jax.experimental.pallas import tpu as pltpu

F32 = jnp.float32
BF16 = jnp.bfloat16

D_MODEL = 1024
DEPTH = 4
CHUNK = 64
D_SSM = 2048
SSM_HEAD_DIM = 64
N_SSM_HEADS = 32
SSM_GROUPS = 4
HEADS_PER_GROUP = N_SSM_HEADS // SSM_GROUPS
D_STATE = 128
CONV_W = 4
CONV_DIM = D_SSM + 2 * SSM_GROUPS * D_STATE
N_MLA_HEADS = 16
D_NOPE = 64
D_ROPE = 32
D_V = 64
Q_LORA = 512
KV_LORA = 256
D_MLA = N_MLA_HEADS * D_V
MLA_SCALE = (D_NOPE + D_ROPE) ** -0.5
ROPE_THETA = 10000.0
N_MEM = 256
MEM_HEADS = 4
MEM_HEAD_DIM = 256
D_MEM = MEM_HEADS * MEM_HEAD_DIM
ALPHA = (2 * DEPTH) ** 0.25
EPS = 1e-5
NEG_INF = -1e30
LOG2E = math.log2(math.e)

LANES = 128
HEAD_PAD = 128
VMEM_LIMIT = 56 * 1024 * 1024

COL_GATES = 0
COL_GMLA = 3072
COL_Z = 4096
COL_QMEM = 6144
N_WIDE = 7168
COL_QA = 0
COL_KVA = 512
COL_MISC_A = 768
COL_MISC_B = 896
N_NARROW = 1024
N_PROJ = CONV_DIM + N_WIDE + N_NARROW


def _cparams(sem):
    return pltpu.CompilerParams(dimension_semantics=sem, vmem_limit_bytes=VMEM_LIMIT)


def _const_spec(shape):
    nd = len(shape)
    return pl.BlockSpec(shape, lambda *_: (0,) * nd)


def _layer_spec(shape, layer):
    nd = len(shape)
    return pl.BlockSpec((None,) + tuple(shape), lambda *_: (layer,) + (0,) * nd, pipeline_mode=pl.Buffered(1))


def _stack_io(body, n_in, stacks):
    stacks = [s for s in stacks if s is not None]
    if not stacks:
        return body, [], [], {}

    def wrapped(*refs):
        return body(*refs[:n_in], *refs[n_in + len(stacks):])
    specs = [pl.BlockSpec(memory_space=pl.ANY)] * len(stacks)
    return wrapped, specs, [a for a, _ in stacks], {n_in + k: out for k, (_, out) in enumerate(stacks)}


def _silu(x):
    return x * jax.nn.sigmoid(x)


def _softplus(x):
    return jnp.maximum(x, 0.0) + jnp.log1p(jnp.exp(-jnp.abs(x)))


def _dot(a, b):
    return jnp.dot(a, b, preferred_element_type=F32)


def _dot_nt(a, b):
    return lax.dot_general(a, b, (((1,), (1,)), ((), ())), preferred_element_type=F32)


def _layer_norm(x, g, b):
    mu = jnp.mean(x, axis=-1, keepdims=True)
    xc = x - mu
    var = jnp.mean(xc * xc, axis=-1, keepdims=True)
    return xc * lax.rsqrt(var + EPS) * g + b


def _rms_norm(x, w):
    return x * lax.rsqrt(jnp.mean(x * x, axis=-1, keepdims=True) + EPS) * w


def _ln_body(x_ref, g_ref, b_ref, o32_ref, o16_ref):
    y = _layer_norm(x_ref[...], g_ref[...], b_ref[...])
    o32_ref[...] = y
    o16_ref[...] = y.astype(BF16)


def _ln_call(x, g, b, tm):
    t, d = x.shape
    assert t % tm == 0, (t, tm)
    return pl.pallas_call(
        _ln_body,
        grid=(t // tm,),
        in_specs=[pl.BlockSpec((tm, d), lambda i: (i, 0)), _const_spec((1, d)), _const_spec((1, d))],
        out_specs=[pl.BlockSpec((tm, d), lambda i: (i, 0)), pl.BlockSpec((tm, d), lambda i: (i, 0))],
        out_shape=[jax.ShapeDtypeStruct((t, d), F32), jax.ShapeDtypeStruct((t, d), BF16)],
        compiler_params=_cparams(("parallel",)),
        name="ln_in",
    )(x, g.reshape(1, d), b.reshape(1, d))


def _mm_body(x_ref, w_ref, o_ref):
    o_ref[...] = _dot(x_ref[...], w_ref[...]).astype(o_ref.dtype)


def _mm_call(x, w, layer, tm, tn, name, prev=None):
    m, k = x.shape
    n = w.shape[2]
    assert m % tm == 0 and n % tn == 0, (m, n, tm, tn)
    body, st_specs, st_args, aliases = _stack_io(_mm_body, 2, [None if prev is None else (prev, 0)])
    return pl.pallas_call(
        body,
        grid=(m // tm, n // tn),
        in_specs=[pl.BlockSpec((tm, k), lambda i, j: (i, 0)),
                  pl.BlockSpec((None, k, tn), lambda i, j: (layer, 0, j))] + st_specs,
        out_specs=pl.BlockSpec((None, tm, tn), lambda i, j: (layer, i, j)),
        out_shape=jax.ShapeDtypeStruct((DEPTH, m, n), F32),
        input_output_aliases=aliases,
        compiler_params=_cparams(("parallel", "parallel")),
        name=name,
    )(x, w, *st_args)


def _causal_conv(x, t8, cw, cb):
    acc = cb + cw[3:4, :] * x
    for j in range(1, CONV_W):
        acc = acc + cw[3 - j:4 - j, :] * pltpu.roll(x, j, 0)
    x8 = x[0:8, :]
    row8 = lax.broadcasted_iota(jnp.int32, x8.shape, 0)
    acc8 = cb + cw[3:4, :] * x8
    for j in range(1, CONV_W):
        xj = jnp.where(row8 < j, pltpu.roll(t8, j, 0), pltpu.roll(x8, j, 0))
        acc8 = acc8 + cw[3 - j:4 - j, :] * xj
    return jnp.concatenate([acc8, acc[8:, :]], axis=0)


def _inproj_body(x_ref, w_ref, xo_ref, pa_ref, pb_ref, *, n_conv, n_a):
    j = pl.program_id(1)

    @pl.when(j < n_conv)
    def _():
        xo_ref[...] = _dot(x_ref[...], w_ref[...]).astype(xo_ref.dtype)

    @pl.when(jnp.logical_and(j >= n_conv, j < n_conv + n_a))
    def _():
        pa_ref[...] = _dot(x_ref[...], w_ref[...]).astype(pa_ref.dtype)

    @pl.when(j >= n_conv + n_a)
    def _():
        pb_ref[...] = _dot(x_ref[...], w_ref[...])


def _inproj_call(x, w, layer, *, tm, tn, wide_dtype, name):
    t, k = x.shape
    assert t % tm == 0 and CONV_DIM % tn == 0 and N_WIDE % tn == 0 and N_NARROW % tn == 0, (t, tm, tn)
    n_conv = CONV_DIM // tn
    n_a = N_WIDE // tn
    n_b = N_NARROW // tn
    body = functools.partial(_inproj_body, n_conv=n_conv, n_a=n_a)
    return pl.pallas_call(
        body,
        grid=(t // tm, n_conv + n_a + n_b),
        in_specs=[pl.BlockSpec((tm, k), lambda i, j: (i, 0)),
                  pl.BlockSpec((None, k, tn), lambda i, j: (layer, 0, j))],
        out_specs=[pl.BlockSpec((tm, tn), lambda i, j: (i, jnp.minimum(j, n_conv - 1))),
                   pl.BlockSpec((tm, tn), lambda i, j: (i, jnp.clip(j - n_conv, 0, n_a - 1))),
                   pl.BlockSpec((tm, tn), lambda i, j: (i, jnp.maximum(j - n_conv - n_a, 0)))],
        out_shape=[jax.ShapeDtypeStruct((t, CONV_DIM), wide_dtype),
                   jax.ShapeDtypeStruct((t, N_WIDE), wide_dtype),
                   jax.ShapeDtypeStruct((t, N_NARROW), F32)],
        compiler_params=_cparams(("parallel", "arbitrary")),
        name=name,
    )(x, w['w_proj'])


def _cumsum_rows(x, t):
    row = lax.broadcasted_iota(jnp.int32, x.shape, 0)
    s = x
    k = 1
    while k < t:
        s = s + jnp.where(row >= k, pltpu.roll(s, k, 0), 0.0)
        k *= 2
    return s


def _expand_heads(w, e_ref):
    hi = w.astype(BF16)
    lo = (w - hi.astype(F32)).astype(BF16)
    e = e_ref[...]
    return _dot(hi, e) + _dot(lo, e)


def _ssm_body(*refs, rb, tc, has_prev):
    if has_prev:
        (xbc_ref, z_ref, misc_ref, tail0_ref, s0_ref, cw_ref, cb_ref, dtb_ref, alog_ref, dskip_ref, nw_ref, e_ref,
         y_ref, sfin_ref, tail_ref, cbuf, ybuf, dabuf, dtbuf, state, tail) = refs
    else:
        (xbc_ref, z_ref, misc_ref, cw_ref, cb_ref, dtb_ref, alog_ref, dskip_ref, nw_ref, e_ref,
         y_ref, sfin_ref, tail_ref, cbuf, ybuf, dabuf, dtbuf, state, tail) = refs
    r = pl.program_id(1)
    nr = pl.num_programs(1)
    pair_w = 2 * SSM_HEAD_DIM

    @pl.when(r == 0)
    def _():
        if has_prev:
            for hp in range(N_SSM_HEADS // 2):
                blk = s0_ref[0, 2 * hp:2 * hp + 2].reshape(pair_w, D_STATE)
                state[:, hp * pair_w:(hp + 1) * pair_w] = blk.T
            tail[...] = tail0_ref[0]
        else:
            state[...] = jnp.zeros(state.shape, F32)
            tail[...] = jnp.zeros(tail.shape, F32)

    x = xbc_ref[...].astype(F32)
    cbuf[...] = _silu(_causal_conv(x, tail[...], cw_ref[...], cb_ref[...]))
    tail[...] = x[rb - 8:rb, :]

    dt = _softplus(misc_ref[...] + dtb_ref[...])
    dtbuf[...] = dt
    dabuf[...] = dt * (-jnp.exp(alog_ref[...]) * LOG2E)

    ri = lax.broadcasted_iota(jnp.int32, (tc, tc), 0)
    ci = lax.broadcasted_iota(jnp.int32, (tc, tc), 1)
    lower = ri >= ci
    pad_rows = LANES - tc

    def to_rows(a):
        if pad_rows:
            a = jnp.concatenate([a, jnp.zeros((pad_rows, LANES), F32)], axis=0)
        return a.T[:, :tc]

    def chunk(c, carry):
        r0 = pl.multiple_of(c * tc, tc)
        rows = pl.ds(r0, tc)
        da = dabuf[rows, :]
        dtc = dtbuf[rows, :]
        acs = _cumsum_rows(da, tc)
        acs_t = to_rows(acs)
        dt_t = to_rows(dtc)
        total = acs[tc - 1:tc, :]
        wexp = _expand_heads(jnp.exp2(total - acs) * dtc, e_ref)
        etot = _expand_heads(jnp.broadcast_to(jnp.exp2(total), (8, LANES)), e_ref)[0:1, :]
        xs = cbuf[rows, 0:D_SSM]
        xs16 = xs.astype(BF16)
        xw16 = (xs * wexp).astype(BF16)
        for g in range(SSM_GROUPS):
            bg = cbuf[rows, D_SSM + g * D_STATE:D_SSM + (g + 1) * D_STATE]
            cg = cbuf[rows, D_SSM + SSM_GROUPS * D_STATE + g * D_STATE:D_SSM + SSM_GROUPS * D_STATE + (g + 1) * D_STATE]
            bg16 = bg.astype(BF16)
            cg16 = cg.astype(BF16)
            cbm = _dot_nt(cg16, bg16)
            gcols = slice(g * HEADS_PER_GROUP * SSM_HEAD_DIM, (g + 1) * HEADS_PER_GROUP * SSM_HEAD_DIM)
            s_old = state[:, gcols]
            s16 = s_old.astype(BF16)
            for hh in range(0, HEADS_PER_GROUP, 2):
                pair = []
                for h in (g * HEADS_PER_GROUP + hh, g * HEADS_PER_GROUP + hh + 1):
                    colb = jnp.broadcast_to(acs[:, h:h + 1], (tc, LANES))
                    rowb = jnp.broadcast_to(acs_t[h:h + 1, :], (tc, tc))
                    dtb = jnp.broadcast_to(dt_t[h:h + 1, :], (tc, tc))
                    lmat = jnp.where(lower, jnp.exp2(colb[:, :tc] - rowb), 0.0)
                    m16 = (cbm * lmat * dtb).astype(BF16)
                    cexp16 = (cg * jnp.exp2(colb)).astype(BF16)
                    hl = h - g * HEADS_PER_GROUP
                    xh = xs16[:, h * SSM_HEAD_DIM:(h + 1) * SSM_HEAD_DIM]
                    sh = s16[:, hl * SSM_HEAD_DIM:(hl + 1) * SSM_HEAD_DIM]
                    if tc % LANES == 0:
                        yh = _dot(jnp.concatenate([m16, cexp16], axis=1), jnp.concatenate([xh, sh], axis=0))
                    else:
                        yh = _dot(m16, xh) + _dot(cexp16, sh)
                    pair.append(yh)
                h0 = g * HEADS_PER_GROUP + hh
                ybuf[rows, h0 * SSM_HEAD_DIM:(h0 + 2) * SSM_HEAD_DIM] = jnp.concatenate(pair, axis=1)
            bg_t = to_rows(bg).astype(BF16)
            state[:, gcols] = s_old * etot[:, gcols] + _dot(bg_t, xw16[:, gcols])
        y = ybuf[rows, :] + dskip_ref[...] * xs
        zc = z_ref[rows, :].astype(F32)
        gt = y * _silu(zc)
        gw = D_SSM // SSM_GROUPS
        for g in range(SSM_GROUPS):
            gg = gt[:, g * gw:(g + 1) * gw]
            ms = jnp.mean(gg * gg, axis=-1, keepdims=True)
            y_ref[rows, g * gw:(g + 1) * gw] = (gg * lax.rsqrt(ms + EPS) * nw_ref[:, g * gw:(g + 1) * gw]).astype(BF16)
        return carry

    lax.fori_loop(0, rb // tc, chunk, 0)

    @pl.when(r == nr - 1)
    def _():
        for hp in range(N_SSM_HEADS // 2):
            blk = state[:, hp * pair_w:(hp + 1) * pair_w].T
            sfin_ref[0, 2 * hp:2 * hp + 2] = blk.reshape(2, SSM_HEAD_DIM, D_STATE)
        tail_ref[0] = tail[...]


def _ssm_call(xc, pa, pb, prev, w, emat, layer, *, batch, seq, rb, tc, prev_out=None):
    t = batch * seq
    assert seq % rb == 0 and rb % tc == 0 and rb % 8 == 0, (seq, rb, tc)
    nrb = seq // rb
    row = lambda b, r: b * nrb + r
    has_prev = prev is not None
    body = functools.partial(_ssm_body, rb=rb, tc=tc, has_prev=has_prev)
    state_block = (1, N_SSM_HEADS, SSM_HEAD_DIM, D_STATE)
    prev_specs = []
    prev_args = []
    if has_prev:
        prev_specs = [pl.BlockSpec((1, 8, CONV_DIM), lambda b, r: (b, 0, 0)),
                      pl.BlockSpec((None,) + state_block, lambda b, r: (layer, b, 0, 0, 0))]
        prev_args = list(prev)
    n_in = 10 + len(prev_args)
    body, st_specs, st_args, aliases = _stack_io(body, n_in, [None if prev_out is None else (prev_out, 1)])
    return pl.pallas_call(
        body,
        grid=(batch, nrb),
        in_specs=[
            pl.BlockSpec((rb, CONV_DIM), lambda b, r: (row(b, r), 0)),
            pl.BlockSpec((rb, D_SSM), lambda b, r: (row(b, r), COL_Z // D_SSM)),
            pl.BlockSpec((rb, LANES), lambda b, r: (row(b, r), COL_MISC_A // LANES)),
            *prev_specs,
            _layer_spec((CONV_W, CONV_DIM), layer),
            _layer_spec((1, CONV_DIM), layer),
            _layer_spec((1, LANES), layer),
            _layer_spec((1, LANES), layer),
            _layer_spec((1, D_SSM), layer),
            _layer_spec((1, D_SSM), layer),
            _const_spec((LANES, D_SSM)),
        ] + st_specs,
        out_specs=[
            pl.BlockSpec((rb, D_SSM), lambda b, r: (row(b, r), 0)),
            pl.BlockSpec((None,) + state_block, lambda b, r: (layer, b, 0, 0, 0)),
            pl.BlockSpec((1, 8, CONV_DIM), lambda b, r: (b, 0, 0)),
        ],
        out_shape=[
            jax.ShapeDtypeStruct((t, D_SSM), BF16),
            jax.ShapeDtypeStruct((DEPTH, batch) + state_block[1:], F32),
            jax.ShapeDtypeStruct((batch, 8, CONV_DIM), F32),
        ],
        input_output_aliases=aliases,
        scratch_shapes=[
            pltpu.VMEM((rb, CONV_DIM), F32),
            pltpu.VMEM((rb, D_SSM), F32),
            pltpu.VMEM((rb, LANES), F32),
            pltpu.VMEM((rb, LANES), F32),
            pltpu.VMEM((D_STATE, D_SSM), F32),
            pltpu.VMEM((8, CONV_DIM), F32),
        ],
        compiler_params=_cparams(("parallel", "arbitrary")),
        name="ssd_mixer",
    )(xc, pa, pb, *prev_args, w['cw'], w['cb'], w['dtb'], w['alog'], w['dskip'], w['nw'], emat, *st_args)


def _mla_prep_body(qa_ref, kva_ref, ma_ref, mb_ref, qnw_ref, kvnw_ref, wq1_ref, wq2_ref, wk_ref, wv_ref,
                   cq_ref, sq_ref, ck_ref, sk_ref, q_ref, k_ref, v_ref, ckv_ref, kr_ref):
    qa16 = _rms_norm(qa_ref[...], qnw_ref[...]).astype(BF16)
    q1 = _dot(qa16, wq1_ref[...])
    q2 = _dot(qa16, wq2_ref[...])
    cq = jnp.tile(cq_ref[...], (1, N_MLA_HEADS))
    sq = jnp.tile(sq_ref[...], (1, N_MLA_HEADS))
    q_ref[...] = (q1 * cq + q2 * sq).astype(BF16)
    ckv = _rms_norm(kva_ref[...], kvnw_ref[...])
    ckv_ref[...] = ckv
    ckv16 = ckv.astype(BF16)
    kr = ma_ref[...] * ck_ref[...] + mb_ref[...] * sk_ref[...]
    kr_ref[...] = kr
    k_ref[...] = (_dot(ckv16, wk_ref[...]) + jnp.tile(kr, (1, N_MLA_HEADS))).astype(BF16)
    lane = lax.broadcasted_iota(jnp.int32, (1, N_MLA_HEADS * HEAD_PAD), 1)
    ones_col = jnp.where(lane % HEAD_PAD == D_V, 1.0, 0.0)
    v_ref[...] = (_dot(ckv16, wv_ref[...]) + ones_col).astype(BF16)


def _mla_prep_call(proj, w, cq, sq, ck, sk, layer, *, seq, tm, prev=None):
    t = proj.shape[0]
    assert seq % tm == 0 and t % seq == 0, (t, seq, tm)
    npos = seq // tm
    hp = N_MLA_HEADS * HEAD_PAD
    tab = pl.BlockSpec((tm, LANES), lambda i: (i % npos, 0))
    stacks = [None, None] if prev is None else [(prev[0], 3), (prev[1], 4)]
    body, st_specs, st_args, aliases = _stack_io(_mla_prep_body, 14, stacks)
    return pl.pallas_call(
        body,
        grid=(t // tm,),
        in_specs=[
            pl.BlockSpec((tm, Q_LORA), lambda i: (i, COL_QA // Q_LORA)),
            pl.BlockSpec((tm, KV_LORA), lambda i: (i, COL_KVA // KV_LORA)),
            pl.BlockSpec((tm, LANES), lambda i: (i, COL_MISC_A // LANES)),
            pl.BlockSpec((tm, LANES), lambda i: (i, COL_MISC_B // LANES)),
            _layer_spec((1, Q_LORA), layer), _layer_spec((1, KV_LORA), layer),
            _layer_spec((Q_LORA, hp), layer), _layer_spec((Q_LORA, hp), layer),
            _layer_spec((KV_LORA, hp), layer), _layer_spec((KV_LORA, hp), layer),
            tab, tab, tab, tab,
        ] + st_specs,
        out_specs=[
            pl.BlockSpec((tm, hp), lambda i: (i, 0)),
            pl.BlockSpec((tm, hp), lambda i: (i, 0)),
            pl.BlockSpec((tm, hp), lambda i: (i, 0)),
            pl.BlockSpec((None, tm, KV_LORA), lambda i: (layer, i, 0)),
            pl.BlockSpec((None, tm, LANES), lambda i: (layer, i, 0)),
        ],
        out_shape=[
            jax.ShapeDtypeStruct((t, hp), BF16),
            jax.ShapeDtypeStruct((t, hp), BF16),
            jax.ShapeDtypeStruct((t, hp), BF16),
            jax.ShapeDtypeStruct((DEPTH, t, KV_LORA), F32),
            jax.ShapeDtypeStruct((DEPTH, t, LANES), F32),
        ],
        input_output_aliases=aliases,
        compiler_params=_cparams(("parallel",)),
        name="mla_prep",
    )(proj, proj, proj, proj, w['qnw'], w['kvnw'], w['wq1'], w['wq2'], w['wk'], w['wv'], cq, sq, ck, sk, *st_args)


STEP_PAIR, STEP_PAIR_DIAG, STEP_SINGLE_DIAG = 0, 1, 2


def _attn_body(qi_ref, ja_ref, jb_ref, kind_ref, q_ref, ka_ref, kb_ref, va_ref, vb_ref, o_ref, m_ref, acc_ref, s_ref,
               *, tq):
    t = pl.program_id(1)
    kind = kind_ref[t]

    @pl.when(ja_ref[t] == 0)
    def _():
        m_ref[...] = jnp.full(m_ref.shape, NEG_INF, F32)
        acc_ref[...] = jnp.zeros(acc_ref.shape, F32)

    def step(two_tiles, bias):
        nk = 2 * tq if two_tiles else tq

        def scores(h):
            hs = slice(h * HEAD_PAD, (h + 1) * HEAD_PAD)
            s_ref[h % 2, :, 0:tq] = _dot_nt(q_ref[0, :, hs], ka_ref[0, :, hs])
            if two_tiles:
                s_ref[h % 2, :, tq:nk] = _dot_nt(q_ref[0, :, hs], kb_ref[0, :, hs])

        scores(0)
        for h in range(N_MLA_HEADS):
            hs = slice(h * HEAD_PAD, (h + 1) * HEAD_PAD)
            if h + 1 < N_MLA_HEADS:
                scores(h + 1)
            s = s_ref[h % 2, :, 0:nk]
            if bias is not None:
                s = jnp.concatenate([s[:, :nk - tq], s[:, nk - tq:] + bias], axis=1) if two_tiles else s + bias
            m_prev = m_ref[h]
            m_new = jnp.maximum(m_prev, jnp.max(s, axis=-1, keepdims=True))
            p = jnp.exp2(s - jnp.tile(m_new, (1, nk // LANES))).astype(BF16)
            pv = _dot(p[:, 0:tq], va_ref[0, :, hs])
            if two_tiles:
                pv = pv + _dot(p[:, tq:nk], vb_ref[0, :, hs])
            acc_ref[h] = jnp.exp2(m_prev - m_new) * acc_ref[h] + pv
            m_ref[h] = m_new

    def diag_bias():
        row = lax.broadcasted_iota(jnp.int32, (tq, tq), 0)
        col = lax.broadcasted_iota(jnp.int32, (tq, tq), 1)
        return jnp.where(col < ((row // CHUNK + 1) * CHUNK), 0.0, NEG_INF)

    def emit():
        for h in range(0, N_MLA_HEADS, 2):
            a = acc_ref[h]
            b = acc_ref[h + 1]
            a = a[:, :D_V] / a[:, D_V:D_V + 1]
            b = b[:, :D_V] / b[:, D_V:D_V + 1]
            o_ref[0, :, h * D_V:(h + 2) * D_V] = jnp.concatenate([a, b], axis=1)

    @pl.when(kind == STEP_PAIR)
    def _():
        step(True, None)

    @pl.when(kind == STEP_PAIR_DIAG)
    def _():
        step(True, diag_bias())
        emit()

    @pl.when(kind == STEP_SINGLE_DIAG)
    def _():
        step(False, diag_bias())
        emit()


def _attn_call(q, k, v, *, batch, seq, tq):
    hp = N_MLA_HEADS * HEAD_PAD
    q = q.reshape(batch, seq, hp)
    k = k.reshape(batch, seq, hp)
    v = v.reshape(batch, seq, hp)
    assert seq % tq == 0 and tq % CHUNK == 0 and tq % LANES == 0, (seq, tq)
    nq = seq // tq
    steps = []
    for i in range(nq):
        tiles = list(range(i + 1))
        while tiles:
            if len(tiles) >= 2:
                ja, jb = tiles[0], tiles[1]
                tiles = tiles[2:]
                steps.append((i, ja, jb, STEP_PAIR if tiles else STEP_PAIR_DIAG))
            else:
                ja = tiles.pop()
                steps.append((i, ja, steps[-1][2] if steps else 0, STEP_SINGLE_DIAG))
    tab = [jnp.asarray([st[c] for st in steps], jnp.int32) for c in range(4)]
    q_spec = pl.BlockSpec((1, tq, hp), lambda b, t, qi, ja, jb, kd: (b, qi[t], 0))
    a_spec = pl.BlockSpec((1, tq, hp), lambda b, t, qi, ja, jb, kd: (b, ja[t], 0))
    b_spec = pl.BlockSpec((1, tq, hp), lambda b, t, qi, ja, jb, kd: (b, jb[t], 0))
    body = functools.partial(_attn_body, tq=tq)
    out = pl.pallas_call(
        body,
        grid_spec=pltpu.PrefetchScalarGridSpec(
            num_scalar_prefetch=4,
            grid=(batch, len(steps)),
            in_specs=[q_spec, a_spec, b_spec, a_spec, b_spec],
            out_specs=pl.BlockSpec((1, tq, D_MLA), lambda b, t, qi, ja, jb, kd: (b, qi[t], 0)),
            scratch_shapes=[
                pltpu.VMEM((N_MLA_HEADS, tq, LANES), F32),
                pltpu.VMEM((N_MLA_HEADS, tq, LANES), F32),
                pltpu.VMEM((2, tq, 2 * tq), F32),
            ],
        ),
        out_shape=jax.ShapeDtypeStruct((batch, seq, D_MLA), F32),
        compiler_params=_cparams(("parallel", "arbitrary")),
        name="mla_attn",
    )(*tab, q, k, k, v, v)
    return out.reshape(batch * seq, D_MLA)


def _mla_prep_s_body(qa_ref, kva_ref, ma_ref, mb_ref, qnw_ref, kvnw_ref, wqn_ref, wqr_ref, wqr2_ref, wukt_ref,
                     cq_ref, sq_ref, ck_ref, sk_ref, qlat_ref, qr_ref, ckv_ref, kr_ref):
    qa16 = _rms_norm(qa_ref[...], qnw_ref[...]).astype(BF16)
    qn16 = _dot(qa16, wqn_ref[...]).astype(BF16)
    for h in range(N_MLA_HEADS):
        qh = qn16[:, h * D_NOPE:(h + 1) * D_NOPE]
        qlat_ref[:, h * KV_LORA:(h + 1) * KV_LORA] = (_dot(qh, wukt_ref[h]) * (MLA_SCALE * LOG2E)).astype(BF16)
    qr = _dot(qa16, wqr_ref[...]) * cq_ref[...] + _dot(qa16, wqr2_ref[...]) * sq_ref[...]
    qr_ref[...] = qr.astype(BF16)
    ckv_ref[...] = _rms_norm(kva_ref[...], kvnw_ref[...])
    kr_ref[...] = ma_ref[...] * ck_ref[...] + mb_ref[...] * sk_ref[...]


def _mla_prep_s_call(proj, w, cq, sq, ck, sk, layer):
    t = proj.shape[0]
    nr = N_MLA_HEADS * D_ROPE
    return pl.pallas_call(
        _mla_prep_s_body,
        grid=(1,),
        in_specs=[
            pl.BlockSpec((t, Q_LORA), lambda i: (0, COL_QA // Q_LORA)),
            pl.BlockSpec((t, KV_LORA), lambda i: (0, COL_KVA // KV_LORA)),
            pl.BlockSpec((t, LANES), lambda i: (0, COL_MISC_A // LANES)),
            pl.BlockSpec((t, LANES), lambda i: (0, COL_MISC_B // LANES)),
            _layer_spec((1, Q_LORA), layer), _layer_spec((1, KV_LORA), layer),
            _layer_spec((Q_LORA, D_MLA), layer), _layer_spec((Q_LORA, nr), layer), _layer_spec((Q_LORA, nr), layer),
            _layer_spec((N_MLA_HEADS, D_NOPE, KV_LORA), layer),
            _const_spec((t, nr)), _const_spec((t, nr)), _const_spec((t, LANES)), _const_spec((t, LANES)),
        ],
        out_specs=[
            _const_spec((t, N_MLA_HEADS * KV_LORA)),
            _const_spec((t, nr)),
            _const_spec((t, KV_LORA)),
            _const_spec((t, LANES)),
        ],
        out_shape=[
            jax.ShapeDtypeStruct((t, N_MLA_HEADS * KV_LORA), BF16),
            jax.ShapeDtypeStruct((t, nr), BF16),
            jax.ShapeDtypeStruct((t, KV_LORA), F32),
            jax.ShapeDtypeStruct((t, LANES), F32),
        ],
        compiler_params=_cparams(("arbitrary",)),
        name="mla_prep_dec",
    )(proj, proj, proj, proj, w['qnw'], w['kvnw'], w['wqn'], w['wqr'], w['wqr2'], w['wukt'], cq, sq, ck, sk)


def _attn_s_body(ql_ref, qr_ref, cc_ref, ckr_ref, nc_ref, nkr_ref, wuv_ref, o_ref, m_ref, l_ref, acc_ref, s_ref,
                 *, lq):
    j = pl.program_id(1)
    nj = pl.num_programs(1)
    ql = ql_ref[0]
    qr = qr_ref[0]

    @pl.when(j == 0)
    def _():
        m_ref[...] = jnp.full(m_ref.shape, NEG_INF, F32)
        l_ref[...] = jnp.zeros(l_ref.shape, F32)
        acc_ref[...] = jnp.zeros(acc_ref.shape, F32)

    def scores(c16, kr16):
        return _dot_nt(ql, c16) + _dot_nt(qr, kr16)

    def update(s, c16):
        m_prev = m_ref[...]
        m_new = jnp.maximum(m_prev, jnp.max(s, axis=-1, keepdims=True))
        alpha = jnp.exp2(m_prev - m_new)
        p = jnp.exp2(s - m_new)
        l_ref[...] = alpha * l_ref[...] + jnp.sum(p, axis=-1, keepdims=True)
        acc_ref[...] = alpha * acc_ref[...] + _dot(p.astype(BF16), c16)
        m_ref[...] = m_new

    half = cc_ref.shape[1] // 2
    c_a = cc_ref[0, :half].astype(BF16)
    c_b = cc_ref[0, half:].astype(BF16)
    s_ref[0] = scores(c_a, ckr_ref[0, :half].astype(BF16))
    s_ref[1] = scores(c_b, ckr_ref[0, half:].astype(BF16))
    update(s_ref[0], c_a)
    update(s_ref[1], c_b)

    @pl.when(j == nj - 1)
    def _():
        c_n = nc_ref[0].astype(BF16)
        update(scores(c_n, nkr_ref[0].astype(BF16)), c_n)
        olat = (acc_ref[...] / l_ref[...]).astype(BF16)
        for h in range(N_MLA_HEADS):
            o_ref[0, h * lq:(h + 1) * lq, :] = _dot(olat[h * lq:(h + 1) * lq, :], wuv_ref[h])


def _attn_s_call(qlat, qr, cache_ckv, cache_kr, new_ckv, new_kr, wuv, layer, *, tk):
    b, rows, _ = qlat.shape
    lq = rows // N_MLA_HEADS
    past = cache_ckv.shape[2]
    assert past % tk == 0 and tk % 16 == 0, (past, tk)
    body = functools.partial(_attn_s_body, lq=lq)
    return pl.pallas_call(
        body,
        grid=(b, past // tk),
        in_specs=[
            pl.BlockSpec((1, rows, KV_LORA), lambda i, j: (i, 0, 0)),
            pl.BlockSpec((1, rows, D_ROPE), lambda i, j: (i, 0, 0)),
            pl.BlockSpec((None, 1, tk, KV_LORA), lambda i, j: (layer, i, j, 0)),
            pl.BlockSpec((None, 1, tk, D_ROPE), lambda i, j: (layer, i, j, 0)),
            pl.BlockSpec((1, lq, KV_LORA), lambda i, j: (i, 0, 0)),
            pl.BlockSpec((1, lq, D_ROPE), lambda i, j: (i, 0, 0)),
            _layer_spec((N_MLA_HEADS, KV_LORA, D_V), layer),
        ],
        out_specs=pl.BlockSpec((1, rows, D_V), lambda i, j: (i, 0, 0)),
        out_shape=jax.ShapeDtypeStruct((b, rows, D_V), F32),
        scratch_shapes=[
            pltpu.VMEM((rows, 1), F32),
            pltpu.VMEM((rows, 1), F32),
            pltpu.VMEM((rows, KV_LORA), F32),
            pltpu.VMEM((2, rows, tk // 2), F32),
        ],
        compiler_params=_cparams(("parallel", "arbitrary")),
        name="mla_attn_dec",
    )(qlat, qr, cache_ckv, cache_kr, new_ckv, new_kr, wuv)


def _mem_attn_body(q_ref, mk_ref, mv_ref, o_ref, s_ref, *, head_split):
    scale = MEM_HEAD_DIM ** -0.5

    def head(ref, h):
        return ref[0, :, h, :] if head_split else ref[0, :, h * MEM_HEAD_DIM:(h + 1) * MEM_HEAD_DIM]

    def scores(h):
        hs = slice(h * MEM_HEAD_DIM, (h + 1) * MEM_HEAD_DIM)
        s_ref[h % 2] = _dot_nt(q_ref[:, hs].astype(BF16), head(mk_ref, h).astype(BF16)) * scale

    lookahead = not head_split
    if lookahead:
        scores(0)
    for h in range(MEM_HEADS):
        hs = slice(h * MEM_HEAD_DIM, (h + 1) * MEM_HEAD_DIM)
        if not lookahead:
            scores(h)
        elif h + 1 < MEM_HEADS:
            scores(h + 1)
        s = s_ref[h % 2]
        p = jnp.exp(s - jnp.max(s, axis=-1, keepdims=True))
        l = jnp.sum(p, axis=-1, keepdims=True)
        o_ref[:, hs] = (_dot(p.astype(BF16), head(mv_ref, h).astype(BF16)) / l).astype(BF16)


def _mem_attn_call(proj, mk, mv, layer, *, batch, seq, tm):
    t = batch * seq
    assert seq % tm == 0, (seq, tm)
    nrb = seq // tm
    head_split = mk.ndim == 5
    if head_split:
        mem_spec = pl.BlockSpec((None, 1, N_MEM, MEM_HEADS, MEM_HEAD_DIM), lambda b, r: (layer, b, 0, 0, 0))
    else:
        mem_spec = pl.BlockSpec((None, 1, N_MEM, D_MEM), lambda b, r: (layer, b, 0, 0))
    return pl.pallas_call(
        functools.partial(_mem_attn_body, head_split=head_split),
        grid=(batch, nrb),
        in_specs=[
            pl.BlockSpec((tm, D_MEM), lambda b, r: (b * nrb + r, COL_QMEM // D_MEM)),
            mem_spec,
            mem_spec,
        ],
        out_specs=pl.BlockSpec((tm, D_MEM), lambda b, r: (b * nrb + r, 0)),
        out_shape=jax.ShapeDtypeStruct((t, D_MEM), BF16),
        scratch_shapes=[pltpu.VMEM((2, tm, N_MEM), F32)],
        compiler_params=_cparams(("parallel", "parallel")),
        name="mem_attn",
    )(proj, mk, mv)


def _merge_body(y_ref, o_ref, g_ref, om_ref, gates_ref, x_ref, wos_ref, woa_ref, wom_ref, wout_ref,
                lng_ref, lnb_ref, x32_ref, x16_ref):
    ys = _dot(y_ref[...], wos_ref[...])
    ya = _dot((o_ref[...] * _silu(g_ref[...].astype(F32))).astype(BF16), woa_ref[...])
    ym = _dot(om_ref[...], wom_ref[...])
    gs = jax.nn.sigmoid(gates_ref[...].astype(F32))
    h = gs[:, 0:D_MODEL] * ys + gs[:, D_MODEL:2 * D_MODEL] * ya + gs[:, 2 * D_MODEL:3 * D_MODEL] * ym
    out = _dot(h.astype(BF16), wout_ref[...])
    xn = _layer_norm(ALPHA * x_ref[...] + out, lng_ref[...], lnb_ref[...])
    x32_ref[...] = xn
    x16_ref[...] = xn.astype(BF16)


def _merge_call(y, o, proj, om, x, w, layer, *, tm):
    t = x.shape[0]
    assert t % tm == 0, (t, tm)
    rowspec = lambda width: pl.BlockSpec((tm, width), lambda i: (i, 0))
    return pl.pallas_call(
        _merge_body,
        grid=(t // tm,),
        in_specs=[
            rowspec(D_SSM), rowspec(D_MLA),
            pl.BlockSpec((tm, D_MLA), lambda i: (i, COL_GMLA // D_MLA)),
            rowspec(D_MEM),
            pl.BlockSpec((tm, 3 * D_MODEL), lambda i: (i, COL_GATES // (3 * D_MODEL))),
            rowspec(D_MODEL),
            _layer_spec((D_SSM, D_MODEL), layer), _layer_spec((D_MLA, D_MODEL), layer),
            _layer_spec((D_MEM, D_MODEL), layer), _layer_spec((D_MODEL, D_MODEL), layer),
            _layer_spec((1, D_MODEL), layer), _layer_spec((1, D_MODEL), layer),
        ],
        out_specs=[rowspec(D_MODEL), rowspec(D_MODEL)],
        out_shape=[jax.ShapeDtypeStruct((t, D_MODEL), F32), jax.ShapeDtypeStruct((t, D_MODEL), BF16)],
        compiler_params=_cparams(("parallel",)),
        name="merge_out",
    )(y, o, proj, om, proj, x, w['wos'], w['woa'], w['wom'], w['wout'], w['lng'], w['lnb'])


def _rot_cols(w):
    half = D_ROPE // 2
    return jnp.concatenate([-w[..., half:], w[..., :half]], axis=-1)


def _pack_weights(p):
    w_in = p['w_in']
    seg = {}
    off = 0
    for name, width in (('z', D_SSM), ('xbc', CONV_DIM), ('dt', N_SSM_HEADS), ('qa', Q_LORA), ('kva', KV_LORA),
                        ('kr', D_ROPE), ('gmla', D_MLA), ('qmem', D_MEM), ('gates', 3 * D_MODEL)):
        seg[name] = w_in[:, :, off:off + width].astype(BF16)
        off += width
    zc = lambda n: jnp.zeros((DEPTH, D_MODEL, n), BF16)
    w_proj = jnp.concatenate([
        seg['xbc'], seg['gates'], seg['gmla'], seg['z'], seg['qmem'], seg['qa'], seg['kva'],
        seg['dt'], zc(32), seg['kr'], zc(32),
        zc(64), _rot_cols(seg['kr']), zc(32),
    ], axis=2)

    wqb = p['w_q_b'].astype(BF16).reshape(DEPTH, Q_LORA, N_MLA_HEADS, D_NOPE + D_ROPE)
    wq_n = wqb[..., :D_NOPE]
    wq_r = wqb[..., D_NOPE:]
    zq = jnp.zeros((DEPTH, Q_LORA, N_MLA_HEADS, HEAD_PAD - D_NOPE - D_ROPE), BF16)
    flat = lambda a: a.reshape(DEPTH, a.shape[1], -1)
    wq1 = flat(jnp.concatenate([wq_n, wq_r, zq], axis=-1))
    wq2 = flat(jnp.concatenate([jnp.zeros_like(wq_n), _rot_cols(wq_r), zq], axis=-1))
    wuk = p['w_uk'].astype(BF16)
    wuv = p['w_uv'].astype(BF16)
    zk = jnp.zeros((DEPTH, KV_LORA, N_MLA_HEADS, HEAD_PAD - D_NOPE), BF16)
    wk = flat(jnp.concatenate([wuk, zk], axis=-1))
    wv = flat(jnp.concatenate([wuv, zk], axis=-1))

    pad128 = lambda v: jnp.pad(v, ((0, 0), (0, LANES - v.shape[1]))).reshape(DEPTH, 1, LANES)
    row = lambda v: v.reshape(DEPTH, 1, v.shape[-1])
    return dict(
        w_proj=w_proj,
        cw=p['conv_w'], cb=row(p['conv_b']),
        dtb=pad128(p['dt_bias']), alog=pad128(p['a_log']),
        dskip=row(jnp.repeat(p['d_skip'], SSM_HEAD_DIM, axis=1)),
        nw=row(p['ssm_norm_w']),
        qnw=row(p['q_a_norm_w']), kvnw=row(p['kv_norm_w']),
        wq1=wq1, wq2=wq2, wk=wk, wv=wv,
        wqn=flat(wq_n), wqr=flat(wq_r), wqr2=flat(_rot_cols(wq_r)),
        wukt=jnp.transpose(wuk, (0, 2, 3, 1)),
        wuv=jnp.transpose(wuv, (0, 2, 1, 3)),
        wmk=p['w_mem_k'].astype(BF16), wmv=p['w_mem_v'].astype(BF16),
        wos=p['w_o_ssm'].astype(BF16), woa=p['w_o_mla'].astype(BF16),
        wom=p['w_o_mem'].astype(BF16), wout=p['w_out'].astype(BF16),
        lng=row(p['ln_g']), lnb=row(p['ln_b']),
    )


def _rope_tables(pos):
    half = D_ROPE // 2
    inv = ROPE_THETA ** (-jnp.arange(half, dtype=F32) / half)
    ang = pos.astype(F32)[:, None] * inv[None, :]
    cos = jnp.cos(ang)
    sin = jnp.sin(ang)
    return jnp.concatenate([cos, cos], axis=-1), jnp.concatenate([sin, sin], axis=-1)


def _expansion_matrix():
    e = np.zeros((LANES, D_SSM), np.float32)
    for h in range(N_SSM_HEADS):
        e[h, h * SSM_HEAD_DIM:(h + 1) * SSM_HEAD_DIM] = 1.0
    return jnp.asarray(e, dtype=BF16)


def kernel(x_prompt, x_sample, mem_prompt, state_ssm, state_conv, cache_ckv, cache_krope, cache_mem_k, cache_mem_v,
           ln_in_g, ln_in_b, w_in, conv_w, conv_b, dt_bias, a_log, d_skip, ssm_norm_w, w_o_ssm, q_a_norm_w, w_q_b,
           kv_norm_w, w_uk, w_uv, w_o_mla, w_mem_k, w_mem_v, w_o_mem, w_out, ln_g, ln_b):
    params = dict(w_in=w_in, conv_w=conv_w, conv_b=conv_b, dt_bias=dt_bias, a_log=a_log, d_skip=d_skip,
                  ssm_norm_w=ssm_norm_w, w_o_ssm=w_o_ssm, q_a_norm_w=q_a_norm_w, w_q_b=w_q_b, kv_norm_w=kv_norm_w,
                  w_uk=w_uk, w_uv=w_uv, w_o_mla=w_o_mla, w_mem_k=w_mem_k, w_mem_v=w_mem_v, w_o_mem=w_o_mem,
                  w_out=w_out, ln_g=ln_g, ln_b=ln_b)
    bp, lp, _ = x_prompt.shape
    bs, ls, _ = x_sample.shape
    past = cache_ckv.shape[2]
    tp = bp * lp
    ts = bs * ls
    assert w_in.shape == (DEPTH, D_MODEL, N_PROJ - 192), w_in.shape
    assert (past // CHUNK + 1) * CHUNK >= past + ls, "decode block would need a causal mask"
    assert ls >= CONV_W - 1 and lp >= CONV_W - 1

    w = _pack_weights(params)
    emat = _expansion_matrix()
    tm_p = min(2048, tp)
    xp32, xp16 = _ln_call(x_prompt.reshape(tp, D_MODEL), ln_in_g, ln_in_b, tm_p)
    xs32, xs16 = _ln_call(x_sample.reshape(ts, D_MODEL), ln_in_g, ln_in_b, ts)
    mem16 = mem_prompt.reshape(bp * N_MEM, D_MODEL).astype(BF16)

    cos_p, sin_p = _rope_tables(jnp.arange(lp, dtype=jnp.int32))
    cos_s, sin_s = _rope_tables(past + jnp.arange(ls, dtype=jnp.int32))
    z64 = lambda c: jnp.zeros((c.shape[0], 64), F32)
    z32 = lambda c: jnp.zeros((c.shape[0], 32), F32)
    one64 = jnp.ones((lp, 64), F32)
    cq_p = jnp.concatenate([one64, cos_p, z32(cos_p)], axis=1) * (MLA_SCALE * LOG2E)
    sq_p = jnp.concatenate([z64(sin_p), sin_p, z32(sin_p)], axis=1) * (MLA_SCALE * LOG2E)
    ck_p = jnp.concatenate([z64(cos_p), cos_p, z32(cos_p)], axis=1)
    sk_p = jnp.concatenate([z64(sin_p), sin_p, z32(sin_p)], axis=1)
    cq_s = jnp.tile(jnp.tile(cos_s, (1, N_MLA_HEADS)), (bs, 1)) * (MLA_SCALE * LOG2E)
    sq_s = jnp.tile(jnp.tile(sin_s, (1, N_MLA_HEADS)), (bs, 1)) * (MLA_SCALE * LOG2E)
    ck_s = jnp.tile(jnp.concatenate([z64(cos_s), cos_s, z32(cos_s)], axis=1), (bs, 1))
    sk_s = jnp.tile(jnp.concatenate([z64(sin_s), sin_s, z32(sin_s)], axis=1), (bs, 1))

    tail_rows = 8 - (CONV_W - 1)
    rb_p = min(512, lp)
    tc_p = min(128, lp)
    tq = min(512, lp)
    tmm = min(2048, tp)
    tk_s = min(1024, past)

    outs = {k: [] for k in ('p_conv', 's_conv', 's_ckv', 's_kr')}
    p_ssm = p_ckv = p_kr = p_mk = p_mv = s_ssm = None
    for i in range(DEPTH):
        xc, pa, pb = _inproj_call(xp16, w, i, tm=tmm, tn=1024, wide_dtype=BF16, name="in_proj")
        y, p_ssm, tail = _ssm_call(xc, pa, pb, None, w, emat, i, batch=bp, seq=lp, rb=rb_p, tc=tc_p, prev_out=p_ssm)
        q, k, v, p_ckv, p_kr = _mla_prep_call(pb, w, cq_p, sq_p, ck_p, sk_p, i, seq=lp, tm=tq,
                                              prev=None if i == 0 else (p_ckv, p_kr))
        o = _attn_call(q, k, v, batch=bp, seq=lp, tq=tq)
        p_mk = _mm_call(mem16, w['wmk'], i, bp * N_MEM, 1024, "mem_k_proj", prev=p_mk)
        p_mv = _mm_call(mem16, w['wmv'], i, bp * N_MEM, 1024, "mem_v_proj", prev=p_mv)
        om = _mem_attn_call(pa, p_mk.reshape(DEPTH, bp, N_MEM, D_MEM), p_mv.reshape(DEPTH, bp, N_MEM, D_MEM), i,
                            batch=bp, seq=lp, tm=min(1024, lp))
        xp32, xp16 = _merge_call(y, o, pa, om, xp32, w, i, tm=min(512, tp))
        outs['p_conv'].append(tail[:, tail_rows:, :])

        xc_s, pa_s, pb_s = _inproj_call(xs16, w, i, tm=ts, tn=1024, wide_dtype=F32, name="in_proj_dec")
        tail0 = jnp.pad(state_conv[i], ((0, 0), (tail_rows, 0), (0, 0)))
        y_s, s_ssm, tail_s = _ssm_call(xc_s, pa_s, pb_s, (tail0, state_ssm), w, emat, i,
                                       batch=bs, seq=ls, rb=ls, tc=ls, prev_out=s_ssm)
        qlat, qr, ckv_s, kr_s = _mla_prep_s_call(pb_s, w, cq_s, sq_s, ck_s, sk_s, i)
        kr_s = kr_s[:, 64:64 + D_ROPE]
        to_hq = lambda a, d: jnp.transpose(a.reshape(bs, ls, N_MLA_HEADS, d), (0, 2, 1, 3)).reshape(bs, N_MLA_HEADS * ls, d)
        o_s = _attn_s_call(to_hq(qlat, KV_LORA), to_hq(qr, D_ROPE), cache_ckv, cache_krope,
                           ckv_s.reshape(bs, ls, KV_LORA), kr_s.reshape(bs, ls, D_ROPE), w['wuv'], i, tk=tk_s)
        o_s = jnp.transpose(o_s.reshape(bs, N_MLA_HEADS, ls, D_V), (0, 2, 1, 3)).reshape(ts, D_MLA)
        om_s = _mem_attn_call(pa_s, cache_mem_k, cache_mem_v, i, batch=bs, seq=ls, tm=ls)
        xs32, xs16 = _merge_call(y_s, o_s, pa_s, om_s, xs32, w, i, tm=min(256, ts))
        outs['s_conv'].append(tail_s[:, tail_rows:, :])
        outs['s_ckv'].append(ckv_s.reshape(bs, ls, KV_LORA))
        outs['s_kr'].append(kr_s.reshape(bs, ls, D_ROPE))

    st = lambda k: jnp.stack(outs[k])
    mem5 = (DEPTH, bp, N_MEM, MEM_HEADS, MEM_HEAD_DIM)
    return (xp32.reshape(bp, lp, D_MODEL), xs32.reshape(bs, ls, D_MODEL), p_ssm, st('p_conv'),
            p_ckv.reshape(DEPTH, bp, lp, KV_LORA), p_kr[:, :, 64:64 + D_ROPE].reshape(DEPTH, bp, lp, D_ROPE),
            p_mk.reshape(mem5), p_mv.reshape(mem5), s_ssm, st('s_conv'), st('s_ckv'), st('s_kr'))
```

```python
import functools
import math

import numpy as np
import jax
import jax.numpy as jnp
from jax import lax
from jax.experimental import pallas as pl
from jax.experimental.pallas import tpu as pltpu

F32 = jnp.float32
BF16 = jnp.bfloat16

D_MODEL = 1024
DEPTH = 4
CHUNK = 64
D_SSM = 2048
SSM_HEAD_DIM = 64
N_SSM_HEADS = 32
SSM_GROUPS = 4
HEADS_PER_GROUP = N_SSM_HEADS // SSM_GROUPS
D_STATE = 128
CONV_W = 4
CONV_DIM = D_SSM + 2 * SSM_GROUPS * D_STATE
N_MLA_HEADS = 16
D_NOPE = 64
D_ROPE = 32
D_V = 64
Q_LORA = 512
KV_LORA = 256
D_MLA = N_MLA_HEADS * D_V
MLA_SCALE = (D_NOPE + D_ROPE) ** -0.5
ROPE_THETA = 10000.0
N_MEM = 256
MEM_HEADS = 4
MEM_HEAD_DIM = 256
D_MEM = MEM_HEADS * MEM_HEAD_DIM
ALPHA = (2 * DEPTH) ** 0.25
EPS = 1e-5
NEG_INF = -1e30
LOG2E = math.log2(math.e)

LANES = 128
HEAD_PAD = 128
VMEM_LIMIT = 56 * 1024 * 1024

COL_GATES = 0
COL_GMLA = 3072
COL_Z = 4096
COL_QMEM = 6144
N_WIDE = 7168
COL_QA = 0
COL_KVA = 512
COL_MISC_A = 768
COL_MISC_B = 896
N_NARROW = 1024
N_PROJ = CONV_DIM + N_WIDE + N_NARROW


def _cparams(sem):
    return pltpu.CompilerParams(dimension_semantics=sem, vmem_limit_bytes=VMEM_LIMIT)


def _const_spec(shape):
    nd = len(shape)
    return pl.BlockSpec(shape, lambda *_: (0,) * nd)


def _layer_spec(shape, layer):
    nd = len(shape)
    return pl.BlockSpec((None,) + tuple(shape), lambda *_: (layer,) + (0,) * nd, pipeline_mode=pl.Buffered(1))


def _stack_io(body, n_in, stacks):
    stacks = [s for s in stacks if s is not None]
    if not stacks:
        return body, [], [], {}

    def wrapped(*refs):
        return body(*refs[:n_in], *refs[n_in + len(stacks):])
    specs = [pl.BlockSpec(memory_space=pl.ANY)] * len(stacks)
    return wrapped, specs, [a for a, _ in stacks], {n_in + k: out for k, (_, out) in enumerate(stacks)}


def _silu(x):
    return x * jax.nn.sigmoid(x)


def _softplus(x):
    return jnp.maximum(x, 0.0) + jnp.log1p(jnp.exp(-jnp.abs(x)))


def _dot(a, b):
    return jnp.dot(a, b, preferred_element_type=F32)


def _dot_nt(a, b):
    return lax.dot_general(a, b, (((1,), (1,)), ((), ())), preferred_element_type=F32)


def _layer_norm(x, g, b):
    mu = jnp.mean(x, axis=-1, keepdims=True)
    xc = x - mu
    var = jnp.mean(xc * xc, axis=-1, keepdims=True)
    return xc * lax.rsqrt(var + EPS) * g + b


def _rms_norm(x, w):
    return x * lax.rsqrt(jnp.mean(x * x, axis=-1, keepdims=True) + EPS) * w


def _ln_body(x_ref, g_ref, b_ref, o32_ref, o16_ref):
    y = _layer_norm(x_ref[...], g_ref[...], b_ref[...])
    o32_ref[...] = y
    o16_ref[...] = y.astype(BF16)


def _ln_call(x, g, b, tm):
    t, d = x.shape
    assert t % tm == 0, (t, tm)
    return pl.pallas_call(
        _ln_body,
        grid=(t // tm,),
        in_specs=[pl.BlockSpec((tm, d), lambda i: (i, 0)), _const_spec((1, d)), _const_spec((1, d))],
        out_specs=[pl.BlockSpec((tm, d), lambda i: (i, 0)), pl.BlockSpec((tm, d), lambda i: (i, 0))],
        out_shape=[jax.ShapeDtypeStruct((t, d), F32), jax.ShapeDtypeStruct((t, d), BF16)],
        compiler_params=_cparams(("parallel",)),
        name="ln_in",
    )(x, g.reshape(1, d), b.reshape(1, d))


def _mm_body(x_ref, w_ref, o_ref):
    o_ref[...] = _dot(x_ref[...], w_ref[...]).astype(o_ref.dtype)


def _mm_call(x, w, layer, tm, tn, name, prev=None):
    m, k = x.shape
    n = w.shape[2]
    assert m % tm == 0 and n % tn == 0, (m, n, tm, tn)
    body, st_specs, st_args, aliases = _stack_io(_mm_body, 2, [None if prev is None else (prev, 0)])
    return pl.pallas_call(
        body,
        grid=(m // tm, n // tn),
        in_specs=[pl.BlockSpec((tm, k), lambda i, j: (i, 0)),
                  pl.BlockSpec((None, k, tn), lambda i, j: (layer, 0, j))] + st_specs,
        out_specs=pl.BlockSpec((None, tm, tn), lambda i, j: (layer, i, j)),
        out_shape=jax.ShapeDtypeStruct((DEPTH, m, n), F32),
        input_output_aliases=aliases,
        compiler_params=_cparams(("parallel", "parallel")),
        name=name,
    )(x, w, *st_args)


def _causal_conv(x, t8, cw, cb):
    acc = cb + cw[3:4, :] * x
    for j in range(1, CONV_W):
        acc = acc + cw[3 - j:4 - j, :] * pltpu.roll(x, j, 0)
    x8 = x[0:8, :]
    row8 = lax.broadcasted_iota(jnp.int32, x8.shape, 0)
    acc8 = cb + cw[3:4, :] * x8
    for j in range(1, CONV_W):
        xj = jnp.where(row8 < j, pltpu.roll(t8, j, 0), pltpu.roll(x8, j, 0))
        acc8 = acc8 + cw[3 - j:4 - j, :] * xj
    return jnp.concatenate([acc8, acc[8:, :]], axis=0)


def _inproj_body(x_ref, w_ref, xo_ref, pa_ref, pb_ref, *, n_conv, n_a):
    j = pl.program_id(1)

    @pl.when(j < n_conv)
    def _():
        xo_ref[...] = _dot(x_ref[...], w_ref[...]).astype(xo_ref.dtype)

    @pl.when(jnp.logical_and(j >= n_conv, j < n_conv + n_a))
    def _():
        pa_ref[...] = _dot(x_ref[...], w_ref[...]).astype(pa_ref.dtype)

    @pl.when(j >= n_conv + n_a)
    def _():
        pb_ref[...] = _dot(x_ref[...], w_ref[...])


def _inproj_call(x, w, layer, *, tm, tn, wide_dtype, name):
    t, k = x.shape
    assert t % tm == 0 and CONV_DIM % tn == 0 and N_WIDE % tn == 0 and N_NARROW % tn == 0, (t, tm, tn)
    n_conv = CONV_DIM // tn
    n_a = N_WIDE // tn
    n_b = N_NARROW // tn
    body = functools.partial(_inproj_body, n_conv=n_conv, n_a=n_a)
    return pl.pallas_call(
        body,
        grid=(t // tm, n_conv + n_a + n_b),
        in_specs=[pl.BlockSpec((tm, k), lambda i, j: (i, 0)),
                  pl.BlockSpec((None, k, tn), lambda i, j: (layer, 0, j))],
        out_specs=[pl.BlockSpec((tm, tn), lambda i, j: (i, jnp.minimum(j, n_conv - 1))),
                   pl.BlockSpec((tm, tn), lambda i, j: (i, jnp.clip(j - n_conv, 0, n_a - 1))),
                   pl.BlockSpec((tm, tn), lambda i, j: (i, jnp.maximum(j - n_conv - n_a, 0)))],
        out_shape=[jax.ShapeDtypeStruct((t, CONV_DIM), wide_dtype),
                   jax.ShapeDtypeStruct((t, N_WIDE), wide_dtype),
                   jax.ShapeDtypeStruct((t, N_NARROW), F32)],
        compiler_params=_cparams(("parallel", "arbitrary")),
        name=name,
    )(x, w['w_proj'])


def _cumsum_rows(x, t):
    row = lax.broadcasted_iota(jnp.int32, x.shape, 0)
    s = x
    k = 1
    while k < t:
        s = s + jnp.where(row >= k, pltpu.roll(s, k, 0), 0.0)
        k *= 2
    return s


def _expand_heads(w, e_ref):
    hi = w.astype(BF16)
    lo = (w - hi.astype(F32)).astype(BF16)
    e = e_ref[...]
    return _dot(hi, e) + _dot(lo, e)


def _ssm_body(*refs, rb, tc, has_prev):
    if has_prev:
        (xbc_ref, z_ref, misc_ref, tail0_ref, s0_ref, cw_ref, cb_ref, dtb_ref, alog_ref, dskip_ref, nw_ref, e_ref,
         y_ref, sfin_ref, tail_ref, cbuf, ybuf, dabuf, dtbuf, state, tail) = refs
    else:
        (xbc_ref, z_ref, misc_ref, cw_ref, cb_ref, dtb_ref, alog_ref, dskip_ref, nw_ref, e_ref,
         y_ref, sfin_ref, tail_ref, cbuf, ybuf, dabuf, dtbuf, state, tail) = refs
    r = pl.program_id(1)
    nr = pl.num_programs(1)
    pair_w = 2 * SSM_HEAD_DIM

    @pl.when(r == 0)
    def _():
        if has_prev:
            for hp in range(N_SSM_HEADS // 2):
                blk = s0_ref[0, 2 * hp:2 * hp + 2].reshape(pair_w, D_STATE)
                state[:, hp * pair_w:(hp + 1) * pair_w] = blk.T
            tail[...] = tail0_ref[0]
        else:
            state[...] = jnp.zeros(state.shape, F32)
            tail[...] = jnp.zeros(tail.shape, F32)

    x = xbc_ref[...].astype(F32)
    cbuf[...] = _silu(_causal_conv(x, tail[...], cw_ref[...], cb_ref[...]))
    tail[...] = x[rb - 8:rb, :]

    dt = _softplus(misc_ref[...] + dtb_ref[...])
    dtbuf[...] = dt
    dabuf[...] = dt * (-jnp.exp(alog_ref[...]) * LOG2E)

    ri = lax.broadcasted_iota(jnp.int32, (tc, tc), 0)
    ci = lax.broadcasted_iota(jnp.int32, (tc, tc), 1)
    lower = ri >= ci
    pad_rows = LANES - tc

    def to_rows(a):
        if pad_rows:
            a = jnp.concatenate([a, jnp.zeros((pad_rows, LANES), F32)], axis=0)
        return a.T[:, :tc]

    def chunk(c, carry):
        r0 = pl.multiple_of(c * tc, tc)
        rows = pl.ds(r0, tc)
        da = dabuf[rows, :]
        dtc = dtbuf[rows, :]
        acs = _cumsum_rows(da, tc)
        acs_t = to_rows(acs)
        dt_t = to_rows(dtc)
        total = acs[tc - 1:tc, :]
        wexp = _expand_heads(jnp.exp2(total - acs) * dtc, e_ref)
        etot = _expand_heads(jnp.broadcast_to(jnp.exp2(total), (8, LANES)), e_ref)[0:1, :]
        xs = cbuf[rows, 0:D_SSM]
        xs16 = xs.astype(BF16)
        xw16 = (xs * wexp).astype(BF16)
        for g in range(SSM_GROUPS):
            bg = cbuf[rows, D_SSM + g * D_STATE:D_SSM + (g + 1) * D_STATE]
            cg = cbuf[rows, D_SSM + SSM_GROUPS * D_STATE + g * D_STATE:D_SSM + SSM_GROUPS * D_STATE + (g + 1) * D_STATE]
            bg16 = bg.astype(BF16)
            cg16 = cg.astype(BF16)
            cbm = _dot_nt(cg16, bg16)
            gcols = slice(g * HEADS_PER_GROUP * SSM_HEAD_DIM, (g + 1) * HEADS_PER_GROUP * SSM_HEAD_DIM)
            s_old = state[:, gcols]
            s16 = s_old.astype(BF16)
            for hh in range(0, HEADS_PER_GROUP, 2):
                pair = []
                for h in (g * HEADS_PER_GROUP + hh, g * HEADS_PER_GROUP + hh + 1):
                    colb = jnp.broadcast_to(acs[:, h:h + 1], (tc, LANES))
                    rowb = jnp.broadcast_to(acs_t[h:h + 1, :], (tc, tc))
                    dtb = jnp.broadcast_to(dt_t[h:h + 1, :], (tc, tc))
                    lmat = jnp.where(lower, jnp.exp2(colb[:, :tc] - rowb), 0.0)
                    m16 = (cbm * lmat * dtb).astype(BF16)
                    cexp16 = (cg * jnp.exp2(colb)).astype(BF16)
                    hl = h - g * HEADS_PER_GROUP
                    xh = xs16[:, h * SSM_HEAD_DIM:(h + 1) * SSM_HEAD_DIM]
                    sh = s16[:, hl * SSM_HEAD_DIM:(hl + 1) * SSM_HEAD_DIM]
                    if tc % LANES == 0:
                        yh = _dot(jnp.concatenate([m16, cexp16], axis=1), jnp.concatenate([xh, sh], axis=0))
                    else:
                        yh = _dot(m16, xh) + _dot(cexp16, sh)
                    pair.append(yh)
                h0 = g * HEADS_PER_GROUP + hh
                ybuf[rows, h0 * SSM_HEAD_DIM:(h0 + 2) * SSM_HEAD_DIM] = jnp.concatenate(pair, axis=1)
            bg_t = to_rows(bg).astype(BF16)
            state[:, gcols] = s_old * etot[:, gcols] + _dot(bg_t, xw16[:, gcols])
        y = ybuf[rows, :] + dskip_ref[...] * xs
        zc = z_ref[rows, :].astype(F32)
        gt = y * _silu(zc)
        gw = D_SSM // SSM_GROUPS
        for g in range(SSM_GROUPS):
            gg = gt[:, g * gw:(g + 1) * gw]
            ms = jnp.mean(gg * gg, axis=-1, keepdims=True)
            y_ref[rows, g * gw:(g + 1) * gw] = (gg * lax.rsqrt(ms + EPS) * nw_ref[:, g * gw:(g + 1) * gw]).astype(BF16)
        return carry

    lax.fori_loop(0, rb // tc, chunk, 0)

    @pl.when(r == nr - 1)
    def _():
        for hp in range(N_SSM_HEADS // 2):
            blk = state[:, hp * pair_w:(hp + 1) * pair_w].T
            sfin_ref[0, 2 * hp:2 * hp + 2] = blk.reshape(2, SSM_HEAD_DIM, D_STATE)
        tail_ref[0] = tail[...]


def _ssm_call(xc, pa, pb, prev, w, emat, layer, *, batch, seq, rb, tc, prev_out=None):
    t = batch * seq
    assert seq % rb == 0 and rb % tc == 0 and rb % 8 == 0, (seq, rb, tc)
    nrb = seq // rb
    row = lambda b, r: b * nrb + r
    has_prev = prev is not None
    body = functools.partial(_ssm_body, rb=rb, tc=tc, has_prev=has_prev)
    state_block = (1, N_SSM_HEADS, SSM_HEAD_DIM, D_STATE)
    prev_specs = []
    prev_args = []
    if has_prev:
        prev_specs = [pl.BlockSpec((1, 8, CONV_DIM), lambda b, r: (b, 0, 0)),
                      pl.BlockSpec((None,) + state_block, lambda b, r: (layer, b, 0, 0, 0))]
        prev_args = list(prev)
    n_in = 10 + len(prev_args)
    body, st_specs, st_args, aliases = _stack_io(body, n_in, [None if prev_out is None else (prev_out, 1)])
    return pl.pallas_call(
        body,
        grid=(batch, nrb),
        in_specs=[
            pl.BlockSpec((rb, CONV_DIM), lambda b, r: (row(b, r), 0)),
            pl.BlockSpec((rb, D_SSM), lambda b, r: (row(b, r), COL_Z // D_SSM)),
            pl.BlockSpec((rb, LANES), lambda b, r: (row(b, r), COL_MISC_A // LANES)),
            *prev_specs,
            _layer_spec((CONV_W, CONV_DIM), layer),
            _layer_spec((1, CONV_DIM), layer),
            _layer_spec((1, LANES), layer),
            _layer_spec((1, LANES), layer),
            _layer_spec((1, D_SSM), layer),
            _layer_spec((1, D_SSM), layer),
            _const_spec((LANES, D_SSM)),
        ] + st_specs,
        out_specs=[
            pl.BlockSpec((rb, D_SSM), lambda b, r: (row(b, r), 0)),
            pl.BlockSpec((None,) + state_block, lambda b, r: (layer, b, 0, 0, 0)),
            pl.BlockSpec((1, 8, CONV_DIM), lambda b, r: (b, 0, 0)),
        ],
        out_shape=[
            jax.ShapeDtypeStruct((t, D_SSM), BF16),
            jax.ShapeDtypeStruct((DEPTH, batch) + state_block[1:], F32),
            jax.ShapeDtypeStruct((batch, 8, CONV_DIM), F32),
        ],
        input_output_aliases=aliases,
        scratch_shapes=[
            pltpu.VMEM((rb, CONV_DIM), F32),
            pltpu.VMEM((rb, D_SSM), F32),
            pltpu.VMEM((rb, LANES), F32),
            pltpu.VMEM((rb, LANES), F32),
            pltpu.VMEM((D_STATE, D_SSM), F32),
            pltpu.VMEM((8, CONV_DIM), F32),
        ],
        compiler_params=_cparams(("parallel", "arbitrary")),
        name="ssd_mixer",
    )(xc, pa, pb, *prev_args, w['cw'], w['cb'], w['dtb'], w['alog'], w['dskip'], w['nw'], emat, *st_args)


def _mla_prep_body(qa_ref, kva_ref, ma_ref, mb_ref, qnw_ref, kvnw_ref, wq1_ref, wq2_ref, wk_ref, wv_ref,
                   cq_ref, sq_ref, ck_ref, sk_ref, q_ref, k_ref, v_ref, ckv_ref, kr_ref):
    qa16 = _rms_norm(qa_ref[...], qnw_ref[...]).astype(BF16)
    q1 = _dot(qa16, wq1_ref[...])
    q2 = _dot(qa16, wq2_ref[...])
    cq = jnp.tile(cq_ref[...], (1, N_MLA_HEADS))
    sq = jnp.tile(sq_ref[...], (1, N_MLA_HEADS))
    q_ref[...] = (q1 * cq + q2 * sq).astype(BF16)
    ckv = _rms_norm(kva_ref[...], kvnw_ref[...])
    ckv_ref[...] = ckv
    ckv16 = ckv.astype(BF16)
    kr = ma_ref[...] * ck_ref[...] + mb_ref[...] * sk_ref[...]
    kr_ref[...] = kr
    k_ref[...] = (_dot(ckv16, wk_ref[...]) + jnp.tile(kr, (1, N_MLA_HEADS))).astype(BF16)
    lane = lax.broadcasted_iota(jnp.int32, (1, N_MLA_HEADS * HEAD_PAD), 1)
    ones_col = jnp.where(lane % HEAD_PAD == D_V, 1.0, 0.0)
    v_ref[...] = (_dot(ckv16, wv_ref[...]) + ones_col).astype(BF16)


def _mla_prep_call(proj, w, cq, sq, ck, sk, layer, *, seq, tm, prev=None):
    t = proj.shape[0]
    assert seq % tm == 0 and t % seq == 0, (t, seq, tm)
    npos = seq // tm
    hp = N_MLA_HEADS * HEAD_PAD
    tab = pl.BlockSpec((tm, LANES), lambda i: (i % npos, 0))
    stacks = [None, None] if prev is None else [(prev[0], 3), (prev[1], 4)]
    body, st_specs, st_args, aliases = _stack_io(_mla_prep_body, 14, stacks)
    return pl.pallas_call(
        body,
        grid=(t // tm,),
        in_specs=[
            pl.BlockSpec((tm, Q_LORA), lambda i: (i, COL_QA // Q_LORA)),
            pl.BlockSpec((tm, KV_LORA), lambda i: (i, COL_KVA // KV_LORA)),
            pl.BlockSpec((tm, LANES), lambda i: (i, COL_MISC_A // LANES)),
            pl.BlockSpec((tm, LANES), lambda i: (i, COL_MISC_B // LANES)),
            _layer_spec((1, Q_LORA), layer), _layer_spec((1, KV_LORA), layer),
            _layer_spec((Q_LORA, hp), layer), _layer_spec((Q_LORA, hp), layer),
            _layer_spec((KV_LORA, hp), layer), _layer_spec((KV_LORA, hp), layer),
            tab, tab, tab, tab,
        ] + st_specs,
        out_specs=[
            pl.BlockSpec((tm, hp), lambda i: (i, 0)),
            pl.BlockSpec((tm, hp), lambda i: (i, 0)),
            pl.BlockSpec((tm, hp), lambda i: (i, 0)),
            pl.BlockSpec((None, tm, KV_LORA), lambda i: (layer, i, 0)),
            pl.BlockSpec((None, tm, LANES), lambda i: (layer, i, 0)),
        ],
        out_shape=[
            jax.ShapeDtypeStruct((t, hp), BF16),
            jax.ShapeDtypeStruct((t, hp), BF16),
            jax.ShapeDtypeStruct((t, hp), BF16),
            jax.ShapeDtypeStruct((DEPTH, t, KV_LORA), F32),
            jax.ShapeDtypeStruct((DEPTH, t, LANES), F32),
        ],
        input_output_aliases=aliases,
        compiler_params=_cparams(("parallel",)),
        name="mla_prep",
    )(proj, proj, proj, proj, w['qnw'], w['kvnw'], w['wq1'], w['wq2'], w['wk'], w['wv'], cq, sq, ck, sk, *st_args)


STEP_PAIR, STEP_PAIR_DIAG, STEP_SINGLE_DIAG = 0, 1, 2


def _attn_body(qi_ref, ja_ref, jb_ref, kind_ref, q_ref, ka_ref, kb_ref, va_ref, vb_ref, o_ref, m_ref, acc_ref, s_ref,
               *, tq):
    t = pl.program_id(1)
    kind = kind_ref[t]

    @pl.when(ja_ref[t] == 0)
    def _():
        m_ref[...] = jnp.full(m_ref.shape, NEG_INF, F32)
        acc_ref[...] = jnp.zeros(acc_ref.shape, F32)

    def step(two_tiles, bias):
        nk = 2 * tq if two_tiles else tq

        def scores(h):
            hs = slice(h * HEAD_PAD, (h + 1) * HEAD_PAD)
            s_ref[h % 2, :, 0:tq] = _dot_nt(q_ref[0, :, hs], ka_ref[0, :, hs])
            if two_tiles:
                s_ref[h % 2, :, tq:nk] = _dot_nt(q_ref[0, :, hs], kb_ref[0, :, hs])

        scores(0)
        for h in range(N_MLA_HEADS):
            hs = slice(h * HEAD_PAD, (h + 1) * HEAD_PAD)
            if h + 1 < N_MLA_HEADS:
                scores(h + 1)
            s = s_ref[h % 2, :, 0:nk]
            if bias is not None:
                s = jnp.concatenate([s[:, :nk - tq], s[:, nk - tq:] + bias], axis=1) if two_tiles else s + bias
            m_prev = m_ref[h]
            m_new = jnp.maximum(m_prev, jnp.max(s, axis=-1, keepdims=True))
            p = jnp.exp2(s - jnp.tile(m_new, (1, nk // LANES))).astype(BF16)
            pv = _dot(p[:, 0:tq], va_ref[0, :, hs])
            if two_tiles:
                pv = pv + _dot(p[:, tq:nk], vb_ref[0, :, hs])
            acc_ref[h] = jnp.exp2(m_prev - m_new) * acc_ref[h] + pv
            m_ref[h] = m_new

    def diag_bias():
        row = lax.broadcasted_iota(jnp.int32, (tq, tq), 0)
        col = lax.broadcasted_iota(jnp.int32, (tq, tq), 1)
        return jnp.where(col < ((row // CHUNK + 1) * CHUNK), 0.0, NEG_INF)

    def emit():
        for h in range(0, N_MLA_HEADS, 2):
            a = acc_ref[h]
            b = acc_ref[h + 1]
            a = a[:, :D_V] / a[:, D_V:D_V + 1]
            b = b[:, :D_V] / b[:, D_V:D_V + 1]
            o_ref[0, :, h * D_V:(h + 2) * D_V] = jnp.concatenate([a, b], axis=1)

    @pl.when(kind == STEP_PAIR)
    def _():
        step(True, None)

    @pl.when(kind == STEP_PAIR_DIAG)
    def _():
        step(True, diag_bias())
        emit()

    @pl.when(kind == STEP_SINGLE_DIAG)
    def _():
        step(False, diag_bias())
        emit()


def _attn_call(q, k, v, *, batch, seq, tq):
    hp = N_MLA_HEADS * HEAD_PAD
    q = q.reshape(batch, seq, hp)
    k = k.reshape(batch, seq, hp)
    v = v.reshape(batch, seq, hp)
    assert seq % tq == 0 and tq % CHUNK == 0 and tq % LANES == 0, (seq, tq)
    nq = seq // tq
    steps = []
    for i in range(nq):
        tiles = list(range(i + 1))
        while tiles:
            if len(tiles) >= 2:
                ja, jb = tiles[0], tiles[1]
                tiles = tiles[2:]
                steps.append((i, ja, jb, STEP_PAIR if tiles else STEP_PAIR_DIAG))
            else:
                ja = tiles.pop()
                steps.append((i, ja, steps[-1][2] if steps else 0, STEP_SINGLE_DIAG))
    tab = [jnp.asarray([st[c] for st in steps], jnp.int32) for c in range(4)]
    q_spec = pl.BlockSpec((1, tq, hp), lambda b, t, qi, ja, jb, kd: (b, qi[t], 0))
    a_spec = pl.BlockSpec((1, tq, hp), lambda b, t, qi, ja, jb, kd: (b, ja[t], 0))
    b_spec = pl.BlockSpec((1, tq, hp), lambda b, t, qi, ja, jb, kd: (b, jb[t], 0))
    body = functools.partial(_attn_body, tq=tq)
    out = pl.pallas_call(
        body,
        grid_spec=pltpu.PrefetchScalarGridSpec(
            num_scalar_prefetch=4,
            grid=(batch, len(steps)),
            in_specs=[q_spec, a_spec, b_spec, a_spec, b_spec],
            out_specs=pl.BlockSpec((1, tq, D_MLA), lambda b, t, qi, ja, jb, kd: (b, qi[t], 0)),
            scratch_shapes=[
                pltpu.VMEM((N_MLA_HEADS, tq, LANES), F32),
                pltpu.VMEM((N_MLA_HEADS, tq, LANES), F32),
                pltpu.VMEM((2, tq, 2 * tq), F32),
            ],
        ),
        out_shape=jax.ShapeDtypeStruct((batch, seq, D_MLA), F32),
        compiler_params=_cparams(("parallel", "arbitrary")),
        name="mla_attn",
    )(*tab, q, k, k, v, v)
    return out.reshape(batch * seq, D_MLA)


def _mla_prep_s_body(qa_ref, kva_ref, ma_ref, mb_ref, qnw_ref, kvnw_ref, wqn_ref, wqr_ref, wqr2_ref, wukt_ref,
                     cq_ref, sq_ref, ck_ref, sk_ref, qlat_ref, qr_ref, ckv_ref, kr_ref):
    qa16 = _rms_norm(qa_ref[...], qnw_ref[...]).astype(BF16)
    qn16 = _dot(qa16, wqn_ref[...]).astype(BF16)
    for h in range(N_MLA_HEADS):
        qh = qn16[:, h * D_NOPE:(h + 1) * D_NOPE]
        qlat_ref[:, h * KV_LORA:(h + 1) * KV_LORA] = (_dot(qh, wukt_ref[h]) * (MLA_SCALE * LOG2E)).astype(BF16)
    qr = _dot(qa16, wqr_ref[...]) * cq_ref[...] + _dot(qa16, wqr2_ref[...]) * sq_ref[...]
    qr_ref[...] = qr.astype(BF16)
    ckv_ref[...] = _rms_norm(kva_ref[...], kvnw_ref[...])
    kr_ref[...] = ma_ref[...] * ck_ref[...] + mb_ref[...] * sk_ref[...]


def _mla_prep_s_call(proj, w, cq, sq, ck, sk, layer):
    t = proj.shape[0]
    nr = N_MLA_HEADS * D_ROPE
    return pl.pallas_call(
        _mla_prep_s_body,
        grid=(1,),
        in_specs=[
            pl.BlockSpec((t, Q_LORA), lambda i: (0, COL_QA // Q_LORA)),
            pl.BlockSpec((t, KV_LORA), lambda i: (0, COL_KVA // KV_LORA)),
            pl.BlockSpec((t, LANES), lambda i: (0, COL_MISC_A // LANES)),
            pl.BlockSpec((t, LANES), lambda i: (0, COL_MISC_B // LANES)),
            _layer_spec((1, Q_LORA), layer), _layer_spec((1, KV_LORA), layer),
            _layer_spec((Q_LORA, D_MLA), layer), _layer_spec((Q_LORA, nr), layer), _layer_spec((Q_LORA, nr), layer),
            _layer_spec((N_MLA_HEADS, D_NOPE, KV_LORA), layer),
            _const_spec((t, nr)), _const_spec((t, nr)), _const_spec((t, LANES)), _const_spec((t, LANES)),
        ],
        out_specs=[
            _const_spec((t, N_MLA_HEADS * KV_LORA)),
            _const_spec((t, nr)),
            _const_spec((t, KV_LORA)),
            _const_spec((t, LANES)),
        ],
        out_shape=[
            jax.ShapeDtypeStruct((t, N_MLA_HEADS * KV_LORA), BF16),
            jax.ShapeDtypeStruct((t, nr), BF16),
            jax.ShapeDtypeStruct((t, KV_LORA), F32),
            jax.ShapeDtypeStruct((t, LANES), F32),
        ],
        compiler_params=_cparams(("arbitrary",)),
        name="mla_prep_dec",
    )(proj, proj, proj, proj, w['qnw'], w['kvnw'], w['wqn'], w['wqr'], w['wqr2'], w['wukt'], cq, sq, ck, sk)


def _attn_s_body(ql_ref, qr_ref, cc_ref, ckr_ref, nc_ref, nkr_ref, wuv_ref, o_ref, m_ref, l_ref, acc_ref, *, lq):
    j = pl.program_id(1)
    nj = pl.num_programs(1)
    ql = ql_ref[0]
    qr = qr_ref[0]

    @pl.when(j == 0)
    def _():
        m_ref[...] = jnp.full(m_ref.shape, NEG_INF, F32)
        l_ref[...] = jnp.zeros(l_ref.shape, F32)
        acc_ref[...] = jnp.zeros(acc_ref.shape, F32)

    def update(c16, kr16):
        s = _dot_nt(ql, c16) + _dot_nt(qr, kr16)
        m_prev = m_ref[...]
        m_new = jnp.maximum(m_prev, jnp.max(s, axis=-1, keepdims=True))
        alpha = jnp.exp2(m_prev - m_new)
        p = jnp.exp2(s - m_new)
        l_ref[...] = alpha * l_ref[...] + jnp.sum(p, axis=-1, keepdims=True)
        acc_ref[...] = alpha * acc_ref[...] + _dot(p.astype(BF16), c16)
        m_ref[...] = m_new

    update(cc_ref[0].astype(BF16), ckr_ref[0].astype(BF16))

    @pl.when(j == nj - 1)
    def _():
        update(nc_ref[0].astype(BF16), nkr_ref[0].astype(BF16))
        olat = (acc_ref[...] / l_ref[...]).astype(BF16)
        for h in range(N_MLA_HEADS):
            o_ref[0, h * lq:(h + 1) * lq, :] = _dot(olat[h * lq:(h + 1) * lq, :], wuv_ref[h])


def _attn_s_call(qlat, qr, cache_ckv, cache_kr, new_ckv, new_kr, wuv, layer, *, tk):
    b, rows, _ = qlat.shape
    lq = rows // N_MLA_HEADS
    past = cache_ckv.shape[2]
    assert past % tk == 0, (past, tk)
    body = functools.partial(_attn_s_body, lq=lq)
    return pl.pallas_call(
        body,
        grid=(b, past // tk),
        in_specs=[
            pl.BlockSpec((1, rows, KV_LORA), lambda i, j: (i, 0, 0)),
            pl.BlockSpec((1, rows, D_ROPE), lambda i, j: (i, 0, 0)),
            pl.BlockSpec((None, 1, tk, KV_LORA), lambda i, j: (layer, i, j, 0)),
            pl.BlockSpec((None, 1, tk, D_ROPE), lambda i, j: (layer, i, j, 0)),
            pl.BlockSpec((1, lq, KV_LORA), lambda i, j: (i, 0, 0)),
            pl.BlockSpec((1, lq, D_ROPE), lambda i, j: (i, 0, 0)),
            _layer_spec((N_MLA_HEADS, KV_LORA, D_V), layer),
        ],
        out_specs=pl.BlockSpec((1, rows, D_V), lambda i, j: (i, 0, 0)),
        out_shape=jax.ShapeDtypeStruct((b, rows, D_V), F32),
        scratch_shapes=[
            pltpu.VMEM((rows, 1), F32),
            pltpu.VMEM((rows, 1), F32),
            pltpu.VMEM((rows, KV_LORA), F32),
        ],
        compiler_params=_cparams(("parallel", "arbitrary")),
        name="mla_attn_dec",
    )(qlat, qr, cache_ckv, cache_kr, new_ckv, new_kr, wuv)


def _mem_attn_body(q_ref, mk_ref, mv_ref, o_ref, s_ref, *, head_split):
    scale = MEM_HEAD_DIM ** -0.5

    def head(ref, h):
        return ref[0, :, h, :] if head_split else ref[0, :, h * MEM_HEAD_DIM:(h + 1) * MEM_HEAD_DIM]

    def scores(h):
        hs = slice(h * MEM_HEAD_DIM, (h + 1) * MEM_HEAD_DIM)
        s_ref[h % 2] = _dot_nt(q_ref[:, hs].astype(BF16), head(mk_ref, h).astype(BF16)) * scale

    lookahead = not head_split
    if lookahead:
        scores(0)
    for h in range(MEM_HEADS):
        hs = slice(h * MEM_HEAD_DIM, (h + 1) * MEM_HEAD_DIM)
        if not lookahead:
            scores(h)
        elif h + 1 < MEM_HEADS:
            scores(h + 1)
        s = s_ref[h % 2]
        p = jnp.exp(s - jnp.max(s, axis=-1, keepdims=True))
        l = jnp.sum(p, axis=-1, keepdims=True)
        o_ref[:, hs] = (_dot(p.astype(BF16), head(mv_ref, h).astype(BF16)) / l).astype(BF16)


def _mem_attn_call(proj, mk, mv, layer, *, batch, seq, tm):
    t = batch * seq
    assert seq % tm == 0, (seq, tm)
    nrb = seq // tm
    head_split = mk.ndim == 5
    if head_split:
        mem_spec = pl.BlockSpec((None, 1, N_MEM, MEM_HEADS, MEM_HEAD_DIM), lambda b, r: (layer, b, 0, 0, 0))
    else:
        mem_spec = pl.BlockSpec((None, 1, N_MEM, D_MEM), lambda b, r: (layer, b, 0, 0))
    return pl.pallas_call(
        functools.partial(_mem_attn_body, head_split=head_split),
        grid=(batch, nrb),
        in_specs=[
            pl.BlockSpec((tm, D_MEM), lambda b, r: (b * nrb + r, COL_QMEM // D_MEM)),
            mem_spec,
            mem_spec,
        ],
        out_specs=pl.BlockSpec((tm, D_MEM), lambda b, r: (b * nrb + r, 0)),
        out_shape=jax.ShapeDtypeStruct((t, D_MEM), BF16),
        scratch_shapes=[pltpu.VMEM((2, tm, N_MEM), F32)],
        compiler_params=_cparams(("parallel", "parallel")),
        name="mem_attn",
    )(proj, mk, mv)


def _merge_body(y_ref, o_ref, g_ref, om_ref, gates_ref, x_ref, wos_ref, woa_ref, wom_ref, wout_ref,
                lng_ref, lnb_ref, x32_ref, x16_ref):
    ys = _dot(y_ref[...], wos_ref[...])
    ya = _dot((o_ref[...] * _silu(g_ref[...].astype(F32))).astype(BF16), woa_ref[...])
    ym = _dot(om_ref[...], wom_ref[...])
    gs = jax.nn.sigmoid(gates_ref[...].astype(F32))
    h = gs[:, 0:D_MODEL] * ys + gs[:, D_MODEL:2 * D_MODEL] * ya + gs[:, 2 * D_MODEL:3 * D_MODEL] * ym
    out = _dot(h.astype(BF16), wout_ref[...])
    xn = _layer_norm(ALPHA * x_ref[...] + out, lng_ref[...], lnb_ref[...])
    x32_ref[...] = xn
    x16_ref[...] = xn.astype(BF16)


def _merge_call(y, o, proj, om, x, w, layer, *, tm):
    t = x.shape[0]
    assert t % tm == 0, (t, tm)
    rowspec = lambda width: pl.BlockSpec((tm, width), lambda i: (i, 0))
    return pl.pallas_call(
        _merge_body,
        grid=(t // tm,),
        in_specs=[
            rowspec(D_SSM), rowspec(D_MLA),
            pl.BlockSpec((tm, D_MLA), lambda i: (i, COL_GMLA // D_MLA)),
            rowspec(D_MEM),
            pl.BlockSpec((tm, 3 * D_MODEL), lambda i: (i, COL_GATES // (3 * D_MODEL))),
            rowspec(D_MODEL),
            _layer_spec((D_SSM, D_MODEL), layer), _layer_spec((D_MLA, D_MODEL), layer),
            _layer_spec((D_MEM, D_MODEL), layer), _layer_spec((D_MODEL, D_MODEL), layer),
            _layer_spec((1, D_MODEL), layer), _layer_spec((1, D_MODEL), layer),
        ],
        out_specs=[rowspec(D_MODEL), rowspec(D_MODEL)],
        out_shape=[jax.ShapeDtypeStruct((t, D_MODEL), F32), jax.ShapeDtypeStruct((t, D_MODEL), BF16)],
        compiler_params=_cparams(("parallel",)),
        name="merge_out",
    )(y, o, proj, om, proj, x, w['wos'], w['woa'], w['wom'], w['wout'], w['lng'], w['lnb'])


def _rot_cols(w):
    half = D_ROPE // 2
    return jnp.concatenate([-w[..., half:], w[..., :half]], axis=-1)


def _pack_weights(p):
    w_in = p['w_in']
    seg = {}
    off = 0
    for name, width in (('z', D_SSM), ('xbc', CONV_DIM), ('dt', N_SSM_HEADS), ('qa', Q_LORA), ('kva', KV_LORA),
                        ('kr', D_ROPE), ('gmla', D_MLA), ('qmem', D_MEM), ('gates', 3 * D_MODEL)):
        seg[name] = w_in[:, :, off:off + width]
        off += width
    zc = lambda n: jnp.zeros((DEPTH, D_MODEL, n), F32)
    w_proj = jnp.concatenate([
        seg['xbc'], seg['gates'], seg['gmla'], seg['z'], seg['qmem'], seg['qa'], seg['kva'],
        seg['dt'], zc(32), seg['kr'], zc(32),
        zc(64), _rot_cols(seg['kr']), zc(32),
    ], axis=2).astype(BF16)

    wqb = p['w_q_b'].astype(BF16).reshape(DEPTH, Q_LORA, N_MLA_HEADS, D_NOPE + D_ROPE)
    wq_n = wqb[..., :D_NOPE]
    wq_r = wqb[..., D_NOPE:]
    zq = jnp.zeros((DEPTH, Q_LORA, N_MLA_HEADS, HEAD_PAD - D_NOPE - D_ROPE), BF16)
    flat = lambda a: a.reshape(DEPTH, a.shape[1], -1)
    wq1 = flat(jnp.concatenate([wq_n, wq_r, zq], axis=-1))
    wq2 = flat(jnp.concatenate([jnp.zeros_like(wq_n), _rot_cols(wq_r), zq], axis=-1))
    wuk = p['w_uk'].astype(BF16)
    wuv = p['w_uv'].astype(BF16)
    zk = jnp.zeros((DEPTH, KV_LORA, N_MLA_HEADS, HEAD_PAD - D_NOPE), BF16)
    wk = flat(jnp.concatenate([wuk, zk], axis=-1))
    wv = flat(jnp.concatenate([wuv, zk], axis=-1))

    pad128 = lambda v: jnp.pad(v, ((0, 0), (0, LANES - v.shape[1]))).reshape(DEPTH, 1, LANES)
    row = lambda v: v.reshape(DEPTH, 1, v.shape[-1])
    return dict(
        w_proj=w_proj,
        cw=p['conv_w'], cb=row(p['conv_b']),
        dtb=pad128(p['dt_bias']), alog=pad128(p['a_log']),
        dskip=row(jnp.repeat(p['d_skip'], SSM_HEAD_DIM, axis=1)),
        nw=row(p['ssm_norm_w']),
        qnw=row(p['q_a_norm_w']), kvnw=row(p['kv_norm_w']),
        wq1=wq1, wq2=wq2, wk=wk, wv=wv,
        wqn=flat(wq_n), wqr=flat(wq_r), wqr2=flat(_rot_cols(wq_r)),
        wukt=jnp.transpose(wuk, (0, 2, 3, 1)),
        wuv=jnp.transpose(wuv, (0, 2, 1, 3)),
        wmk=p['w_mem_k'].astype(BF16), wmv=p['w_mem_v'].astype(BF16),
        wos=p['w_o_ssm'].astype(BF16), woa=p['w_o_mla'].astype(BF16),
        wom=p['w_o_mem'].astype(BF16), wout=p['w_out'].astype(BF16),
        lng=row(p['ln_g']), lnb=row(p['ln_b']),
    )


def _rope_tables(pos):
    half = D_ROPE // 2
    inv = ROPE_THETA ** (-jnp.arange(half, dtype=F32) / half)
    ang = pos.astype(F32)[:, None] * inv[None, :]
    cos = jnp.cos(ang)
    sin = jnp.sin(ang)
    return jnp.concatenate([cos, cos], axis=-1), jnp.concatenate([sin, sin], axis=-1)


def _expansion_matrix():
    e = np.zeros((LANES, D_SSM), np.float32)
    for h in range(N_SSM_HEADS):
        e[h, h * SSM_HEAD_DIM:(h + 1) * SSM_HEAD_DIM] = 1.0
    return jnp.asarray(e, dtype=BF16)


def kernel(x_prompt, x_sample, mem_prompt, state_ssm, state_conv, cache_ckv, cache_krope, cache_mem_k, cache_mem_v,
           ln_in_g, ln_in_b, w_in, conv_w, conv_b, dt_bias, a_log, d_skip, ssm_norm_w, w_o_ssm, q_a_norm_w, w_q_b,
           kv_norm_w, w_uk, w_uv, w_o_mla, w_mem_k, w_mem_v, w_o_mem, w_out, ln_g, ln_b):
    params = dict(w_in=w_in, conv_w=conv_w, conv_b=conv_b, dt_bias=dt_bias, a_log=a_log, d_skip=d_skip,
                  ssm_norm_w=ssm_norm_w, w_o_ssm=w_o_ssm, q_a_norm_w=q_a_norm_w, w_q_b=w_q_b, kv_norm_w=kv_norm_w,
                  w_uk=w_uk, w_uv=w_uv, w_o_mla=w_o_mla, w_mem_k=w_mem_k, w_mem_v=w_mem_v, w_o_mem=w_o_mem,
                  w_out=w_out, ln_g=ln_g, ln_b=ln_b)
    bp, lp, _ = x_prompt.shape
    bs, ls, _ = x_sample.shape
    past = cache_ckv.shape[2]
    tp = bp * lp
    ts = bs * ls
    assert w_in.shape == (DEPTH, D_MODEL, N_PROJ - 192), w_in.shape
    assert (past // CHUNK + 1) * CHUNK >= past + ls, "decode block would need a causal mask"
    assert ls >= CONV_W - 1 and lp >= CONV_W - 1

    w = _pack_weights(params)
    emat = _expansion_matrix()
    tm_p = min(2048, tp)
    xp32, xp16 = _ln_call(x_prompt.reshape(tp, D_MODEL), ln_in_g, ln_in_b, tm_p)
    xs32, xs16 = _ln_call(x_sample.reshape(ts, D_MODEL), ln_in_g, ln_in_b, ts)
    mem16 = mem_prompt.reshape(bp * N_MEM, D_MODEL).astype(BF16)

    cos_p, sin_p = _rope_tables(jnp.arange(lp, dtype=jnp.int32))
    cos_s, sin_s = _rope_tables(past + jnp.arange(ls, dtype=jnp.int32))
    z64 = lambda c: jnp.zeros((c.shape[0], 64), F32)
    z32 = lambda c: jnp.zeros((c.shape[0], 32), F32)
    one64 = jnp.ones((lp, 64), F32)
    cq_p = jnp.concatenate([one64, cos_p, z32(cos_p)], axis=1) * (MLA_SCALE * LOG2E)
    sq_p = jnp.concatenate([z64(sin_p), sin_p, z32(sin_p)], axis=1) * (MLA_SCALE * LOG2E)
    ck_p = jnp.concatenate([z64(cos_p), cos_p, z32(cos_p)], axis=1)
    sk_p = jnp.concatenate([z64(sin_p), sin_p, z32(sin_p)], axis=1)
    cq_s = jnp.tile(jnp.tile(cos_s, (1, N_MLA_HEADS)), (bs, 1)) * (MLA_SCALE * LOG2E)
    sq_s = jnp.tile(jnp.tile(sin_s, (1, N_MLA_HEADS)), (bs, 1)) * (MLA_SCALE * LOG2E)
    ck_s = jnp.tile(jnp.concatenate([z64(cos_s), cos_s, z32(cos_s)], axis=1), (bs, 1))
    sk_s = jnp.tile(jnp.concatenate([z64(sin_s), sin_s, z32(sin_s)], axis=1), (bs, 1))

    tail_rows = 8 - (CONV_W - 1)
    rb_p = min(512, lp)
    tc_p = min(128, lp)
    tq = min(512, lp)
    tmm = min(2048, tp)
    tk_s = min(1024, past)

    outs = {k: [] for k in ('p_conv', 's_conv', 's_ckv', 's_kr')}
    p_ssm = p_ckv = p_kr = p_mk = p_mv = s_ssm = None
    for i in range(DEPTH):
        xc, pa, pb = _inproj_call(xp16, w, i, tm=tmm, tn=1024, wide_dtype=BF16, name="in_proj")
        y, p_ssm, tail = _ssm_call(xc, pa, pb, None, w, emat, i, batch=bp, seq=lp, rb=rb_p, tc=tc_p, prev_out=p_ssm)
        q, k, v, p_ckv, p_kr = _mla_prep_call(pb, w, cq_p, sq_p, ck_p, sk_p, i, seq=lp, tm=tq,
                                              prev=None if i == 0 else (p_ckv, p_kr))
        o = _attn_call(q, k, v, batch=bp, seq=lp, tq=tq)
        p_mk = _mm_call(mem16, w['wmk'], i, bp * N_MEM, 1024, "mem_k_proj", prev=p_mk)
        p_mv = _mm_call(mem16, w['wmv'], i, bp * N_MEM, 1024, "mem_v_proj", prev=p_mv)
        om = _mem_attn_call(pa, p_mk.reshape(DEPTH, bp, N_MEM, D_MEM), p_mv.reshape(DEPTH, bp, N_MEM, D_MEM), i,
                            batch=bp, seq=lp, tm=min(1024, lp))
        xp32, xp16 = _merge_call(y, o, pa, om, xp32, w, i, tm=min(512, tp))
        outs['p_conv'].append(tail[:, tail_rows:, :])

        xc_s, pa_s, pb_s = _inproj_call(xs16, w, i, tm=ts, tn=1024, wide_dtype=F32, name="in_proj_dec")
        tail0 = jnp.pad(state_conv[i], ((0, 0), (tail_rows, 0), (0, 0)))
        y_s, s_ssm, tail_s = _ssm_call(xc_s, pa_s, pb_s, (tail0, state_ssm), w, emat, i,
                                       batch=bs, seq=ls, rb=ls, tc=ls, prev_out=s_ssm)
        qlat, qr, ckv_s, kr_s = _mla_prep_s_call(pb_s, w, cq_s, sq_s, ck_s, sk_s, i)
        kr_s = kr_s[:, 64:64 + D_ROPE]
        to_hq = lambda a, d: jnp.transpose(a.reshape(bs, ls, N_MLA_HEADS, d), (0, 2, 1, 3)).reshape(bs, N_MLA_HEADS * ls, d)
        o_s = _attn_s_call(to_hq(qlat, KV_LORA), to_hq(qr, D_ROPE), cache_ckv, cache_krope,
                           ckv_s.reshape(bs, ls, KV_LORA), kr_s.reshape(bs, ls, D_ROPE), w['wuv'], i, tk=tk_s)
        o_s = jnp.transpose(o_s.reshape(bs, N_MLA_HEADS, ls, D_V), (0, 2, 1, 3)).reshape(ts, D_MLA)
        om_s = _mem_attn_call(pa_s, cache_mem_k, cache_mem_v, i, batch=bs, seq=ls, tm=ls)
        xs32, xs16 = _merge_call(y_s, o_s, pa_s, om_s, xs32, w, i, tm=min(256, ts))
        outs['s_conv'].append(tail_s[:, tail_rows:, :])
        outs['s_ckv'].append(ckv_s.reshape(bs, ls, KV_LORA))
        outs['s_kr'].append(kr_s.reshape(bs, ls, D_ROPE))

    st = lambda k: jnp.stack(outs[k])
    mem5 = (DEPTH, bp, N_MEM, MEM_HEADS, MEM_HEAD_DIM)
    return (xp32.reshape(bp, lp, D_MODEL), xs32.reshape(bs, ls, D_MODEL), p_ssm, st('p_conv'),
            p_ckv.reshape(DEPTH, bp, lp, KV_LORA), p_kr[:, :, 64:64 + D_ROPE].reshape(DEPTH, bp, lp, D_ROPE),
            p_mk.reshape(mem5), p_mv.reshape(mem5), s_ssm, st('s_conv'), st('s_ckv'), st('s_kr'))
```

```python
import functools
import math

import numpy as np
import jax
import jax.numpy as jnp
from jax import lax
from jax.experimental import pallas as pl
from jax.experimental.pallas import tpu as pltpu

F32 = jnp.float32
BF16 = jnp.bfloat16

D_MODEL = 1024
DEPTH = 4
CHUNK = 64
D_SSM = 2048
SSM_HEAD_DIM = 64
N_SSM_HEADS = 32
SSM_GROUPS = 4
HEADS_PER_GROUP = N_SSM_HEADS // SSM_GROUPS
D_STATE = 128
CONV_W = 4
CONV_DIM = D_SSM + 2 * SSM_GROUPS * D_STATE
N_MLA_HEADS = 16
D_NOPE = 64
D_ROPE = 32
D_V = 64
Q_LORA = 512
KV_LORA = 256
D_MLA = N_MLA_HEADS * D_V
MLA_SCALE = (D_NOPE + D_ROPE) ** -0.5
ROPE_THETA = 10000.0
N_MEM = 256
MEM_HEADS = 4
MEM_HEAD_DIM = 256
D_MEM = MEM_HEADS * MEM_HEAD_DIM
ALPHA = (2 * DEPTH) ** 0.25
EPS = 1e-5
NEG_INF = -1e30
LOG2E = math.log2(math.e)

LANES = 128
HEAD_PAD = 128
VMEM_LIMIT = 56 * 1024 * 1024

COL_GATES = 0
COL_GMLA = 3072
COL_Z = 4096
COL_QMEM = 6144
N_WIDE = 7168
COL_QA = 0
COL_KVA = 512
COL_MISC_A = 768
COL_MISC_B = 896
N_NARROW = 1024
N_PROJ = CONV_DIM + N_WIDE + N_NARROW


def _cparams(sem):
    return pltpu.CompilerParams(dimension_semantics=sem, vmem_limit_bytes=VMEM_LIMIT)


def _const_spec(shape):
    nd = len(shape)
    return pl.BlockSpec(shape, lambda *_: (0,) * nd)


def _layer_spec(shape, layer):
    nd = len(shape)
    return pl.BlockSpec((None,) + tuple(shape), lambda *_: (layer,) + (0,) * nd, pipeline_mode=pl.Buffered(1))


def _stack_io(body, n_in, stacks):
    stacks = [s for s in stacks if s is not None]
    if not stacks:
        return body, [], [], {}

    def wrapped(*refs):
        return body(*refs[:n_in], *refs[n_in + len(stacks):])
    specs = [pl.BlockSpec(memory_space=pl.ANY)] * len(stacks)
    return wrapped, specs, [a for a, _ in stacks], {n_in + k: out for k, (_, out) in enumerate(stacks)}


def _silu(x):
    return x * jax.nn.sigmoid(x)


def _softplus(x):
    return jnp.maximum(x, 0.0) + jnp.log1p(jnp.exp(-jnp.abs(x)))


def _dot(a, b):
    return jnp.dot(a, b, preferred_element_type=F32)


def _dot_nt(a, b):
    return lax.dot_general(a, b, (((1,), (1,)), ((), ())), preferred_element_type=F32)


def _layer_norm(x, g, b):
    mu = jnp.mean(x, axis=-1, keepdims=True)
    xc = x - mu
    var = jnp.mean(xc * xc, axis=-1, keepdims=True)
    return xc * lax.rsqrt(var + EPS) * g + b


def _rms_norm(x, w):
    return x * lax.rsqrt(jnp.mean(x * x, axis=-1, keepdims=True) + EPS) * w


def _ln_body(x_ref, g_ref, b_ref, o32_ref, o16_ref):
    y = _layer_norm(x_ref[...], g_ref[...], b_ref[...])
    o32_ref[...] = y
    o16_ref[...] = y.astype(BF16)


def _ln_call(x, g, b, tm):
    t, d = x.shape
    assert t % tm == 0, (t, tm)
    return pl.pallas_call(
        _ln_body,
        grid=(t // tm,),
        in_specs=[pl.BlockSpec((tm, d), lambda i: (i, 0)), _const_spec((1, d)), _const_spec((1, d))],
        out_specs=[pl.BlockSpec((tm, d), lambda i: (i, 0)), pl.BlockSpec((tm, d), lambda i: (i, 0))],
        out_shape=[jax.ShapeDtypeStruct((t, d), F32), jax.ShapeDtypeStruct((t, d), BF16)],
        compiler_params=_cparams(("parallel",)),
        name="ln_in",
    )(x, g.reshape(1, d), b.reshape(1, d))


def _mm_body(x_ref, w_ref, o_ref):
    o_ref[...] = _dot(x_ref[...], w_ref[...]).astype(o_ref.dtype)


def _mm_call(x, w, layer, tm, tn, name, prev=None):
    m, k = x.shape
    n = w.shape[2]
    assert m % tm == 0 and n % tn == 0, (m, n, tm, tn)
    body, st_specs, st_args, aliases = _stack_io(_mm_body, 2, [None if prev is None else (prev, 0)])
    return pl.pallas_call(
        body,
        grid=(m // tm, n // tn),
        in_specs=[pl.BlockSpec((tm, k), lambda i, j: (i, 0)),
                  pl.BlockSpec((None, k, tn), lambda i, j: (layer, 0, j))] + st_specs,
        out_specs=pl.BlockSpec((None, tm, tn), lambda i, j: (layer, i, j)),
        out_shape=jax.ShapeDtypeStruct((DEPTH, m, n), F32),
        input_output_aliases=aliases,
        compiler_params=_cparams(("parallel", "parallel")),
        name=name,
    )(x, w, *st_args)


def _causal_conv(x, t8, cw, cb):
    acc = cb + cw[3:4, :] * x
    for j in range(1, CONV_W):
        acc = acc + cw[3 - j:4 - j, :] * pltpu.roll(x, j, 0)
    x8 = x[0:8, :]
    row8 = lax.broadcasted_iota(jnp.int32, x8.shape, 0)
    acc8 = cb + cw[3:4, :] * x8
    for j in range(1, CONV_W):
        xj = jnp.where(row8 < j, pltpu.roll(t8, j, 0), pltpu.roll(x8, j, 0))
        acc8 = acc8 + cw[3 - j:4 - j, :] * xj
    return jnp.concatenate([acc8, acc[8:, :]], axis=0)


def _inproj_body(x_ref, w_ref, xo_ref, pa_ref, pb_ref, *, n_conv, n_a):
    j = pl.program_id(1)

    @pl.when(j < n_conv)
    def _():
        xo_ref[...] = _dot(x_ref[...], w_ref[...]).astype(xo_ref.dtype)

    @pl.when(jnp.logical_and(j >= n_conv, j < n_conv + n_a))
    def _():
        pa_ref[...] = _dot(x_ref[...], w_ref[...]).astype(pa_ref.dtype)

    @pl.when(j >= n_conv + n_a)
    def _():
        pb_ref[...] = _dot(x_ref[...], w_ref[...])


def _inproj_call(x, w, layer, *, tm, tn, wide_dtype, name):
    t, k = x.shape
    assert t % tm == 0 and CONV_DIM % tn == 0 and N_WIDE % tn == 0 and N_NARROW % tn == 0, (t, tm, tn)
    n_conv = CONV_DIM // tn
    n_a = N_WIDE // tn
    n_b = N_NARROW // tn
    body = functools.partial(_inproj_body, n_conv=n_conv, n_a=n_a)
    return pl.pallas_call(
        body,
        grid=(t // tm, n_conv + n_a + n_b),
        in_specs=[pl.BlockSpec((tm, k), lambda i, j: (i, 0)),
                  pl.BlockSpec((None, k, tn), lambda i, j: (layer, 0, j))],
        out_specs=[pl.BlockSpec((tm, tn), lambda i, j: (i, jnp.minimum(j, n_conv - 1))),
                   pl.BlockSpec((tm, tn), lambda i, j: (i, jnp.clip(j - n_conv, 0, n_a - 1))),
                   pl.BlockSpec((tm, tn), lambda i, j: (i, jnp.maximum(j - n_conv - n_a, 0)))],
        out_shape=[jax.ShapeDtypeStruct((t, CONV_DIM), wide_dtype),
                   jax.ShapeDtypeStruct((t, N_WIDE), wide_dtype),
                   jax.ShapeDtypeStruct((t, N_NARROW), F32)],
        compiler_params=_cparams(("parallel", "arbitrary")),
        name=name,
    )(x, w['w_proj'])


def _cumsum_rows(x, t):
    row = lax.broadcasted_iota(jnp.int32, x.shape, 0)
    s = x
    k = 1
    while k < t:
        s = s + jnp.where(row >= k, pltpu.roll(s, k, 0), 0.0)
        k *= 2
    return s


def _expand_heads(w, e_ref):
    hi = w.astype(BF16)
    lo = (w - hi.astype(F32)).astype(BF16)
    e = e_ref[...]
    return _dot(hi, e) + _dot(lo, e)


def _ssm_body(*refs, rb, tc, has_prev):
    if has_prev:
        (xbc_ref, z_ref, misc_ref, tail0_ref, s0_ref, cw_ref, cb_ref, dtb_ref, alog_ref, dskip_ref, nw_ref, e_ref,
         y_ref, sfin_ref, tail_ref, cbuf, ybuf, dabuf, dtbuf, state, tail) = refs
    else:
        (xbc_ref, z_ref, misc_ref, cw_ref, cb_ref, dtb_ref, alog_ref, dskip_ref, nw_ref, e_ref,
         y_ref, sfin_ref, tail_ref, cbuf, ybuf, dabuf, dtbuf, state, tail) = refs
    r = pl.program_id(1)
    nr = pl.num_programs(1)
    pair_w = 2 * SSM_HEAD_DIM

    @pl.when(r == 0)
    def _():
        if has_prev:
            for hp in range(N_SSM_HEADS // 2):
                blk = s0_ref[0, 2 * hp:2 * hp + 2].reshape(pair_w, D_STATE)
                state[:, hp * pair_w:(hp + 1) * pair_w] = blk.T
            tail[...] = tail0_ref[0]
        else:
            state[...] = jnp.zeros(state.shape, F32)
            tail[...] = jnp.zeros(tail.shape, F32)

    x = xbc_ref[...].astype(F32)
    cbuf[...] = _silu(_causal_conv(x, tail[...], cw_ref[...], cb_ref[...]))
    tail[...] = x[rb - 8:rb, :]

    dt = _softplus(misc_ref[...] + dtb_ref[...])
    dtbuf[...] = dt
    dabuf[...] = dt * (-jnp.exp(alog_ref[...]) * LOG2E)

    ri = lax.broadcasted_iota(jnp.int32, (tc, tc), 0)
    ci = lax.broadcasted_iota(jnp.int32, (tc, tc), 1)
    lower = ri >= ci
    pad_rows = LANES - tc

    def to_rows(a):
        if pad_rows:
            a = jnp.concatenate([a, jnp.zeros((pad_rows, LANES), F32)], axis=0)
        return a.T[:, :tc]

    def chunk(c, carry):
        r0 = pl.multiple_of(c * tc, tc)
        rows = pl.ds(r0, tc)
        da = dabuf[rows, :]
        dtc = dtbuf[rows, :]
        acs = _cumsum_rows(da, tc)
        acs_t = to_rows(acs)
        dt_t = to_rows(dtc)
        total = acs[tc - 1:tc, :]
        wexp = _expand_heads(jnp.exp2(total - acs) * dtc, e_ref)
        etot = _expand_heads(jnp.broadcast_to(jnp.exp2(total), (8, LANES)), e_ref)[0:1, :]
        xs = cbuf[rows, 0:D_SSM]
        xs16 = xs.astype(BF16)
        xw16 = (xs * wexp).astype(BF16)
        for g in range(SSM_GROUPS):
            bg = cbuf[rows, D_SSM + g * D_STATE:D_SSM + (g + 1) * D_STATE]
            cg = cbuf[rows, D_SSM + SSM_GROUPS * D_STATE + g * D_STATE:D_SSM + SSM_GROUPS * D_STATE + (g + 1) * D_STATE]
            bg16 = bg.astype(BF16)
            cg16 = cg.astype(BF16)
            cbm = _dot_nt(cg16, bg16)
            gcols = slice(g * HEADS_PER_GROUP * SSM_HEAD_DIM, (g + 1) * HEADS_PER_GROUP * SSM_HEAD_DIM)
            s_old = state[:, gcols]
            s16 = s_old.astype(BF16)
            for hh in range(0, HEADS_PER_GROUP, 2):
                pair = []
                for h in (g * HEADS_PER_GROUP + hh, g * HEADS_PER_GROUP + hh + 1):
                    colb = jnp.broadcast_to(acs[:, h:h + 1], (tc, LANES))
                    rowb = jnp.broadcast_to(acs_t[h:h + 1, :], (tc, tc))
                    dtb = jnp.broadcast_to(dt_t[h:h + 1, :], (tc, tc))
                    lmat = jnp.where(lower, jnp.exp2(colb[:, :tc] - rowb), 0.0)
                    m16 = (cbm * lmat * dtb).astype(BF16)
                    cexp16 = (cg * jnp.exp2(colb)).astype(BF16)
                    hl = h - g * HEADS_PER_GROUP
                    xh = xs16[:, h * SSM_HEAD_DIM:(h + 1) * SSM_HEAD_DIM]
                    sh = s16[:, hl * SSM_HEAD_DIM:(hl + 1) * SSM_HEAD_DIM]
                    if tc % LANES == 0:
                        yh = _dot(jnp.concatenate([m16, cexp16], axis=1), jnp.concatenate([xh, sh], axis=0))
                    else:
                        yh = _dot(m16, xh) + _dot(cexp16, sh)
                    pair.append(yh)
                h0 = g * HEADS_PER_GROUP + hh
                ybuf[rows, h0 * SSM_HEAD_DIM:(h0 + 2) * SSM_HEAD_DIM] = jnp.concatenate(pair, axis=1)
            bg_t = to_rows(bg).astype(BF16)
            state[:, gcols] = s_old * etot[:, gcols] + _dot(bg_t, xw16[:, gcols])
        y = ybuf[rows, :] + dskip_ref[...] * xs
        zc = z_ref[rows, :].astype(F32)
        gt = y * _silu(zc)
        gw = D_SSM // SSM_GROUPS
        for g in range(SSM_GROUPS):
            gg = gt[:, g * gw:(g + 1) * gw]
            ms = jnp.mean(gg * gg, axis=-1, keepdims=True)
            y_ref[rows, g * gw:(g + 1) * gw] = (gg * lax.rsqrt(ms + EPS) * nw_ref[:, g * gw:(g + 1) * gw]).astype(BF16)
        return carry

    lax.fori_loop(0, rb // tc, chunk, 0)

    @pl.when(r == nr - 1)
    def _():
        for hp in range(N_SSM_HEADS // 2):
            blk = state[:, hp * pair_w:(hp + 1) * pair_w].T
            sfin_ref[0, 2 * hp:2 * hp + 2] = blk.reshape(2, SSM_HEAD_DIM, D_STATE)
        tail_ref[0] = tail[...]


def _ssm_call(xc, pa, pb, prev, w, emat, layer, *, batch, seq, rb, tc, prev_out=None):
    t = batch * seq
    assert seq % rb == 0 and rb % tc == 0 and rb % 8 == 0, (seq, rb, tc)
    nrb = seq // rb
    row = lambda b, r: b * nrb + r
    has_prev = prev is not None
    body = functools.partial(_ssm_body, rb=rb, tc=tc, has_prev=has_prev)
    state_block = (1, N_SSM_HEADS, SSM_HEAD_DIM, D_STATE)
    prev_specs = []
    prev_args = []
    if has_prev:
        prev_specs = [pl.BlockSpec((1, 8, CONV_DIM), lambda b, r: (b, 0, 0)),
                      pl.BlockSpec((None,) + state_block, lambda b, r: (layer, b, 0, 0, 0))]
        prev_args = list(prev)
    n_in = 10 + len(prev_args)
    body, st_specs, st_args, aliases = _stack_io(body, n_in, [None if prev_out is None else (prev_out, 1)])
    return pl.pallas_call(
        body,
        grid=(batch, nrb),
        in_specs=[
            pl.BlockSpec((rb, CONV_DIM), lambda b, r: (row(b, r), 0)),
            pl.BlockSpec((rb, D_SSM), lambda b, r: (row(b, r), COL_Z // D_SSM)),
            pl.BlockSpec((rb, LANES), lambda b, r: (row(b, r), COL_MISC_A // LANES)),
            *prev_specs,
            _layer_spec((CONV_W, CONV_DIM), layer),
            _layer_spec((1, CONV_DIM), layer),
            _layer_spec((1, LANES), layer),
            _layer_spec((1, LANES), layer),
            _layer_spec((1, D_SSM), layer),
            _layer_spec((1, D_SSM), layer),
            _const_spec((LANES, D_SSM)),
        ] + st_specs,
        out_specs=[
            pl.BlockSpec((rb, D_SSM), lambda b, r: (row(b, r), 0)),
            pl.BlockSpec((None,) + state_block, lambda b, r: (layer, b, 0, 0, 0)),
            pl.BlockSpec((1, 8, CONV_DIM), lambda b, r: (b, 0, 0)),
        ],
        out_shape=[
            jax.ShapeDtypeStruct((t, D_SSM), BF16),
            jax.ShapeDtypeStruct((DEPTH, batch) + state_block[1:], F32),
            jax.ShapeDtypeStruct((batch, 8, CONV_DIM), F32),
        ],
        input_output_aliases=aliases,
        scratch_shapes=[
            pltpu.VMEM((rb, CONV_DIM), F32),
            pltpu.VMEM((rb, D_SSM), F32),
            pltpu.VMEM((rb, LANES), F32),
            pltpu.VMEM((rb, LANES), F32),
            pltpu.VMEM((D_STATE, D_SSM), F32),
            pltpu.VMEM((8, CONV_DIM), F32),
        ],
        compiler_params=_cparams(("parallel", "arbitrary")),
        name="ssd_mixer",
    )(xc, pa, pb, *prev_args, w['cw'], w['cb'], w['dtb'], w['alog'], w['dskip'], w['nw'], emat, *st_args)


def _mla_prep_body(qa_ref, kva_ref, ma_ref, mb_ref, qnw_ref, kvnw_ref, wq1_ref, wq2_ref, wk_ref, wv_ref,
                   cq_ref, sq_ref, ck_ref, sk_ref, q_ref, k_ref, v_ref, ckv_ref, kr_ref):
    qa16 = _rms_norm(qa_ref[...], qnw_ref[...]).astype(BF16)
    q1 = _dot(qa16, wq1_ref[...])
    q2 = _dot(qa16, wq2_ref[...])
    cq = jnp.tile(cq_ref[...], (1, N_MLA_HEADS))
    sq = jnp.tile(sq_ref[...], (1, N_MLA_HEADS))
    q_ref[...] = (q1 * cq + q2 * sq).astype(BF16)
    ckv = _rms_norm(kva_ref[...], kvnw_ref[...])
    ckv_ref[...] = ckv
    ckv16 = ckv.astype(BF16)
    kr = ma_ref[...] * ck_ref[...] + mb_ref[...] * sk_ref[...]
    kr_ref[...] = kr
    k_ref[...] = (_dot(ckv16, wk_ref[...]) + jnp.tile(kr, (1, N_MLA_HEADS))).astype(BF16)
    lane = lax.broadcasted_iota(jnp.int32, (1, N_MLA_HEADS * HEAD_PAD), 1)
    ones_col = jnp.where(lane % HEAD_PAD == D_V, 1.0, 0.0)
    v_ref[...] = (_dot(ckv16, wv_ref[...]) + ones_col).astype(BF16)


def _mla_prep_call(proj, w, cq, sq, ck, sk, layer, *, seq, tm, prev=None):
    t = proj.shape[0]
    assert seq % tm == 0 and t % seq == 0, (t, seq, tm)
    npos = seq // tm
    hp = N_MLA_HEADS * HEAD_PAD
    tab = pl.BlockSpec((tm, LANES), lambda i: (i % npos, 0))
    stacks = [None, None] if prev is None else [(prev[0], 3), (prev[1], 4)]
    body, st_specs, st_args, aliases = _stack_io(_mla_prep_body, 14, stacks)
    return pl.pallas_call(
        body,
        grid=(t // tm,),
        in_specs=[
            pl.BlockSpec((tm, Q_LORA), lambda i: (i, COL_QA // Q_LORA)),
            pl.BlockSpec((tm, KV_LORA), lambda i: (i, COL_KVA // KV_LORA)),
            pl.BlockSpec((tm, LANES), lambda i: (i, COL_MISC_A // LANES)),
            pl.BlockSpec((tm, LANES), lambda i: (i, COL_MISC_B // LANES)),
            _layer_spec((1, Q_LORA), layer), _layer_spec((1, KV_LORA), layer),
            _layer_spec((Q_LORA, hp), layer), _layer_spec((Q_LORA, hp), layer),
            _layer_spec((KV_LORA, hp), layer), _layer_spec((KV_LORA, hp), layer),
            tab, tab, tab, tab,
        ] + st_specs,
        out_specs=[
            pl.BlockSpec((tm, hp), lambda i: (i, 0)),
            pl.BlockSpec((tm, hp), lambda i: (i, 0)),
            pl.BlockSpec((tm, hp), lambda i: (i, 0)),
            pl.BlockSpec((None, tm, KV_LORA), lambda i: (layer, i, 0)),
            pl.BlockSpec((None, tm, LANES), lambda i: (layer, i, 0)),
        ],
        out_shape=[
            jax.ShapeDtypeStruct((t, hp), BF16),
            jax.ShapeDtypeStruct((t, hp), BF16),
            jax.ShapeDtypeStruct((t, hp), BF16),
            jax.ShapeDtypeStruct((DEPTH, t, KV_LORA), F32),
            jax.ShapeDtypeStruct((DEPTH, t, LANES), F32),
        ],
        input_output_aliases=aliases,
        compiler_params=_cparams(("parallel",)),
        name="mla_prep",
    )(proj, proj, proj, proj, w['qnw'], w['kvnw'], w['wq1'], w['wq2'], w['wk'], w['wv'], cq, sq, ck, sk, *st_args)


STEP_PAIR, STEP_PAIR_DIAG, STEP_SINGLE_DIAG = 0, 1, 2


def _attn_body(qi_ref, ja_ref, jb_ref, kind_ref, q_ref, ka_ref, kb_ref, va_ref, vb_ref, o_ref, m_ref, acc_ref, s_ref,
               *, tq):
    t = pl.program_id(1)
    kind = kind_ref[t]

    @pl.when(ja_ref[t] == 0)
    def _():
        m_ref[...] = jnp.full(m_ref.shape, NEG_INF, F32)
        acc_ref[...] = jnp.zeros(acc_ref.shape, F32)

    def step(two_tiles, bias):
        nk = 2 * tq if two_tiles else tq

        def scores(h):
            hs = slice(h * HEAD_PAD, (h + 1) * HEAD_PAD)
            s_ref[h % 2, :, 0:tq] = _dot_nt(q_ref[0, :, hs], ka_ref[0, :, hs])
            if two_tiles:
                s_ref[h % 2, :, tq:nk] = _dot_nt(q_ref[0, :, hs], kb_ref[0, :, hs])

        scores(0)
        for h in range(N_MLA_HEADS):
            hs = slice(h * HEAD_PAD, (h + 1) * HEAD_PAD)
            if h + 1 < N_MLA_HEADS:
                scores(h + 1)
            s = s_ref[h % 2, :, 0:nk]
            if bias is not None:
                s = jnp.concatenate([s[:, :nk - tq], s[:, nk - tq:] + bias], axis=1) if two_tiles else s + bias
            m_prev = m_ref[h]
            m_new = jnp.maximum(m_prev, jnp.max(s, axis=-1, keepdims=True))
            p = jnp.exp2(s - jnp.tile(m_new, (1, nk // LANES))).astype(BF16)
            pv = _dot(p[:, 0:tq], va_ref[0, :, hs])
            if two_tiles:
                pv = pv + _dot(p[:, tq:nk], vb_ref[0, :, hs])
            acc_ref[h] = jnp.exp2(m_prev - m_new) * acc_ref[h] + pv
            m_ref[h] = m_new

    def diag_bias():
        row = lax.broadcasted_iota(jnp.int32, (tq, tq), 0)
        col = lax.broadcasted_iota(jnp.int32, (tq, tq), 1)
        return jnp.where(col < ((row // CHUNK + 1) * CHUNK), 0.0, NEG_INF)

    def emit():
        for h in range(0, N_MLA_HEADS, 2):
            a = acc_ref[h]
            b = acc_ref[h + 1]
            a = a[:, :D_V] / a[:, D_V:D_V + 1]
            b = b[:, :D_V] / b[:, D_V:D_V + 1]
            o_ref[0, :, h * D_V:(h + 2) * D_V] = jnp.concatenate([a, b], axis=1)

    @pl.when(kind == STEP_PAIR)
    def _():
        step(True, None)

    @pl.when(kind == STEP_PAIR_DIAG)
    def _():
        step(True, diag_bias())
        emit()

    @pl.when(kind == STEP_SINGLE_DIAG)
    def _():
        step(False, diag_bias())
        emit()


def _attn_call(q, k, v, *, batch, seq, tq):
    hp = N_MLA_HEADS * HEAD_PAD
    q = q.reshape(batch, seq, hp)
    k = k.reshape(batch, seq, hp)
    v = v.reshape(batch, seq, hp)
    assert seq % tq == 0 and tq % CHUNK == 0 and tq % LANES == 0, (seq, tq)
    nq = seq // tq
    steps = []
    for i in range(nq):
        tiles = list(range(i + 1))
        while tiles:
            if len(tiles) >= 2:
                ja, jb = tiles[0], tiles[1]
                tiles = tiles[2:]
                steps.append((i, ja, jb, STEP_PAIR if tiles else STEP_PAIR_DIAG))
            else:
                ja = tiles.pop()
                steps.append((i, ja, steps[-1][2] if steps else 0, STEP_SINGLE_DIAG))
    tab = [jnp.asarray([st[c] for st in steps], jnp.int32) for c in range(4)]
    q_spec = pl.BlockSpec((1, tq, hp), lambda b, t, qi, ja, jb, kd: (b, qi[t], 0))
    a_spec = pl.BlockSpec((1, tq, hp), lambda b, t, qi, ja, jb, kd: (b, ja[t], 0))
    b_spec = pl.BlockSpec((1, tq, hp), lambda b, t, qi, ja, jb, kd: (b, jb[t], 0))
    body = functools.partial(_attn_body, tq=tq)
    out = pl.pallas_call(
        body,
        grid_spec=pltpu.PrefetchScalarGridSpec(
            num_scalar_prefetch=4,
            grid=(batch, len(steps)),
            in_specs=[q_spec, a_spec, b_spec, a_spec, b_spec],
            out_specs=pl.BlockSpec((1, tq, D_MLA), lambda b, t, qi, ja, jb, kd: (b, qi[t], 0)),
            scratch_shapes=[
                pltpu.VMEM((N_MLA_HEADS, tq, LANES), F32),
                pltpu.VMEM((N_MLA_HEADS, tq, LANES), F32),
                pltpu.VMEM((2, tq, 2 * tq), F32),
            ],
        ),
        out_shape=jax.ShapeDtypeStruct((batch, seq, D_MLA), F32),
        compiler_params=_cparams(("parallel", "arbitrary")),
        name="mla_attn",
    )(*tab, q, k, k, v, v)
    return out.reshape(batch * seq, D_MLA)


def _mla_prep_s_body(qa_ref, kva_ref, ma_ref, mb_ref, qnw_ref, kvnw_ref, wqn_ref, wqr_ref, wqr2_ref, wukt_ref,
                     cq_ref, sq_ref, ck_ref, sk_ref, qlat_ref, qr_ref, ckv_ref, kr_ref):
    qa16 = _rms_norm(qa_ref[...], qnw_ref[...]).astype(BF16)
    qn16 = _dot(qa16, wqn_ref[...]).astype(BF16)
    for h in range(N_MLA_HEADS):
        qh = qn16[:, h * D_NOPE:(h + 1) * D_NOPE]
        qlat_ref[:, h * KV_LORA:(h + 1) * KV_LORA] = (_dot(qh, wukt_ref[h]) * (MLA_SCALE * LOG2E)).astype(BF16)
    qr = _dot(qa16, wqr_ref[...]) * cq_ref[...] + _dot(qa16, wqr2_ref[...]) * sq_ref[...]
    qr_ref[...] = qr.astype(BF16)
    ckv_ref[...] = _rms_norm(kva_ref[...], kvnw_ref[...])
    kr_ref[...] = ma_ref[...] * ck_ref[...] + mb_ref[...] * sk_ref[...]


def _mla_prep_s_call(proj, w, cq, sq, ck, sk, layer):
    t = proj.shape[0]
    nr = N_MLA_HEADS * D_ROPE
    return pl.pallas_call(
        _mla_prep_s_body,
        grid=(1,),
        in_specs=[
            pl.BlockSpec((t, Q_LORA), lambda i: (0, COL_QA // Q_LORA)),
            pl.BlockSpec((t, KV_LORA), lambda i: (0, COL_KVA // KV_LORA)),
            pl.BlockSpec((t, LANES), lambda i: (0, COL_MISC_A // LANES)),
            pl.BlockSpec((t, LANES), lambda i: (0, COL_MISC_B // LANES)),
            _layer_spec((1, Q_LORA), layer), _layer_spec((1, KV_LORA), layer),
            _layer_spec((Q_LORA, D_MLA), layer), _layer_spec((Q_LORA, nr), layer), _layer_spec((Q_LORA, nr), layer),
            _layer_spec((N_MLA_HEADS, D_NOPE, KV_LORA), layer),
            _const_spec((t, nr)), _const_spec((t, nr)), _const_spec((t, LANES)), _const_spec((t, LANES)),
        ],
        out_specs=[
            _const_spec((t, N_MLA_HEADS * KV_LORA)),
            _const_spec((t, nr)),
            _const_spec((t, KV_LORA)),
            _const_spec((t, LANES)),
        ],
        out_shape=[
            jax.ShapeDtypeStruct((t, N_MLA_HEADS * KV_LORA), BF16),
            jax.ShapeDtypeStruct((t, nr), BF16),
            jax.ShapeDtypeStruct((t, KV_LORA), F32),
            jax.ShapeDtypeStruct((t, LANES), F32),
        ],
        compiler_params=_cparams(("arbitrary",)),
        name="mla_prep_dec",
    )(proj, proj, proj, proj, w['qnw'], w['kvnw'], w['wqn'], w['wqr'], w['wqr2'], w['wukt'], cq, sq, ck, sk)


def _attn_s_body(ql_ref, qr_ref, cc_ref, ckr_ref, nc_ref, nkr_ref, wuv_ref, o_ref, m_ref, l_ref, acc_ref, *, lq):
    j = pl.program_id(1)
    nj = pl.num_programs(1)
    ql = ql_ref[0]
    qr = qr_ref[0]

    @pl.when(j == 0)
    def _():
        m_ref[...] = jnp.full(m_ref.shape, NEG_INF, F32)
        l_ref[...] = jnp.zeros(l_ref.shape, F32)
        acc_ref[...] = jnp.zeros(acc_ref.shape, F32)

    def update(c16, kr16):
        s = _dot_nt(ql, c16) + _dot_nt(qr, kr16)
        m_prev = m_ref[...]
        m_new = jnp.maximum(m_prev, jnp.max(s, axis=-1, keepdims=True))
        alpha = jnp.exp2(m_prev - m_new)
        p = jnp.exp2(s - m_new)
        l_ref[...] = alpha * l_ref[...] + jnp.sum(p, axis=-1, keepdims=True)
        acc_ref[...] = alpha * acc_ref[...] + _dot(p.astype(BF16), c16)
        m_ref[...] = m_new

    update(cc_ref[0].astype(BF16), ckr_ref[0].astype(BF16))

    @pl.when(j == nj - 1)
    def _():
        update(nc_ref[0].astype(BF16), nkr_ref[0].astype(BF16))
        olat = (acc_ref[...] / l_ref[...]).astype(BF16)
        for h in range(N_MLA_HEADS):
            o_ref[0, h * lq:(h + 1) * lq, :] = _dot(olat[h * lq:(h + 1) * lq, :], wuv_ref[h])


def _attn_s_call(qlat, qr, cache_ckv, cache_kr, new_ckv, new_kr, wuv, layer, *, tk):
    b, rows, _ = qlat.shape
    lq = rows // N_MLA_HEADS
    past = cache_ckv.shape[2]
    assert past % tk == 0, (past, tk)
    body = functools.partial(_attn_s_body, lq=lq)
    return pl.pallas_call(
        body,
        grid=(b, past // tk),
        in_specs=[
            pl.BlockSpec((1, rows, KV_LORA), lambda i, j: (i, 0, 0)),
            pl.BlockSpec((1, rows, D_ROPE), lambda i, j: (i, 0, 0)),
            pl.BlockSpec((None, 1, tk, KV_LORA), lambda i, j: (layer, i, j, 0)),
            pl.BlockSpec((None, 1, tk, D_ROPE), lambda i, j: (layer, i, j, 0)),
            pl.BlockSpec((1, lq, KV_LORA), lambda i, j: (i, 0, 0)),
            pl.BlockSpec((1, lq, D_ROPE), lambda i, j: (i, 0, 0)),
            _layer_spec((N_MLA_HEADS, KV_LORA, D_V), layer),
        ],
        out_specs=pl.BlockSpec((1, rows, D_V), lambda i, j: (i, 0, 0)),
        out_shape=jax.ShapeDtypeStruct((b, rows, D_V), F32),
        scratch_shapes=[
            pltpu.VMEM((rows, 1), F32),
            pltpu.VMEM((rows, 1), F32),
            pltpu.VMEM((rows, KV_LORA), F32),
        ],
        compiler_params=_cparams(("parallel", "arbitrary")),
        name="mla_attn_dec",
    )(qlat, qr, cache_ckv, cache_kr, new_ckv, new_kr, wuv)


def _mem_attn_body(q_ref, mk_ref, mv_ref, o_ref, s_ref, *, head_split):
    scale = MEM_HEAD_DIM ** -0.5

    def head(ref, h):
        return ref[0, :, h, :] if head_split else ref[0, :, h * MEM_HEAD_DIM:(h + 1) * MEM_HEAD_DIM]

    def scores(h):
        hs = slice(h * MEM_HEAD_DIM, (h + 1) * MEM_HEAD_DIM)
        s_ref[h % 2] = _dot_nt(q_ref[:, hs].astype(BF16), head(mk_ref, h).astype(BF16)) * scale

    lookahead = not head_split
    if lookahead:
        scores(0)
    for h in range(MEM_HEADS):
        hs = slice(h * MEM_HEAD_DIM, (h + 1) * MEM_HEAD_DIM)
        if not lookahead:
            scores(h)
        elif h + 1 < MEM_HEADS:
            scores(h + 1)
        s = s_ref[h % 2]
        p = jnp.exp(s - jnp.max(s, axis=-1, keepdims=True))
        l = jnp.sum(p, axis=-1, keepdims=True)
        o_ref[:, hs] = (_dot(p.astype(BF16), head(mv_ref, h).astype(BF16)) / l).astype(BF16)


def _mem_attn_call(proj, mk, mv, layer, *, batch, seq, tm):
    t = batch * seq
    assert seq % tm == 0, (seq, tm)
    nrb = seq // tm
    head_split = mk.ndim == 5
    if head_split:
        mem_spec = pl.BlockSpec((None, 1, N_MEM, MEM_HEADS, MEM_HEAD_DIM), lambda b, r: (layer, b, 0, 0, 0))
    else:
        mem_spec = pl.BlockSpec((None, 1, N_MEM, D_MEM), lambda b, r: (layer, b, 0, 0))
    return pl.pallas_call(
        functools.partial(_mem_attn_body, head_split=head_split),
        grid=(batch, nrb),
        in_specs=[
            pl.BlockSpec((tm, D_MEM), lambda b, r: (b * nrb + r, COL_QMEM // D_MEM)),
            mem_spec,
            mem_spec,
        ],
        out_specs=pl.BlockSpec((tm, D_MEM), lambda b, r: (b * nrb + r, 0)),
        out_shape=jax.ShapeDtypeStruct((t, D_MEM), BF16),
        scratch_shapes=[pltpu.VMEM((2, tm, N_MEM), F32)],
        compiler_params=_cparams(("parallel", "parallel")),
        name="mem_attn",
    )(proj, mk, mv)


def _merge_body(y_ref, o_ref, g_ref, om_ref, gates_ref, x_ref, wos_ref, woa_ref, wom_ref, wout_ref,
                lng_ref, lnb_ref, x32_ref, x16_ref):
    ys = _dot(y_ref[...], wos_ref[...])
    ya = _dot((o_ref[...] * _silu(g_ref[...].astype(F32))).astype(BF16), woa_ref[...])
    ym = _dot(om_ref[...], wom_ref[...])
    gs = jax.nn.sigmoid(gates_ref[...].astype(F32))
    h = gs[:, 0:D_MODEL] * ys + gs[:, D_MODEL:2 * D_MODEL] * ya + gs[:, 2 * D_MODEL:3 * D_MODEL] * ym
    out = _dot(h.astype(BF16), wout_ref[...])
    xn = _layer_norm(ALPHA * x_ref[...] + out, lng_ref[...], lnb_ref[...])
    x32_ref[...] = xn
    x16_ref[...] = xn.astype(BF16)


def _merge_call(y, o, proj, om, x, w, layer, *, tm):
    t = x.shape[0]
    assert t % tm == 0, (t, tm)
    rowspec = lambda width: pl.BlockSpec((tm, width), lambda i: (i, 0))
    return pl.pallas_call(
        _merge_body,
        grid=(t // tm,),
        in_specs=[
            rowspec(D_SSM), rowspec(D_MLA),
            pl.BlockSpec((tm, D_MLA), lambda i: (i, COL_GMLA // D_MLA)),
            rowspec(D_MEM),
            pl.BlockSpec((tm, 3 * D_MODEL), lambda i: (i, COL_GATES // (3 * D_MODEL))),
            rowspec(D_MODEL),
            _layer_spec((D_SSM, D_MODEL), layer), _layer_spec((D_MLA, D_MODEL), layer),
            _layer_spec((D_MEM, D_MODEL), layer), _layer_spec((D_MODEL, D_MODEL), layer),
            _layer_spec((1, D_MODEL), layer), _layer_spec((1, D_MODEL), layer),
        ],
        out_specs=[rowspec(D_MODEL), rowspec(D_MODEL)],
        out_shape=[jax.ShapeDtypeStruct((t, D_MODEL), F32), jax.ShapeDtypeStruct((t, D_MODEL), BF16)],
        compiler_params=_cparams(("parallel",)),
        name="merge_out",
    )(y, o, proj, om, proj, x, w['wos'], w['woa'], w['wom'], w['wout'], w['lng'], w['lnb'])


def _rot_cols(w):
    half = D_ROPE // 2
    return jnp.concatenate([-w[..., half:], w[..., :half]], axis=-1)


def _pack_weights(p):
    w_in = p['w_in']
    seg = {}
    off = 0
    for name, width in (('z', D_SSM), ('xbc', CONV_DIM), ('dt', N_SSM_HEADS), ('qa', Q_LORA), ('kva', KV_LORA),
                        ('kr', D_ROPE), ('gmla', D_MLA), ('qmem', D_MEM), ('gates', 3 * D_MODEL)):
        seg[name] = w_in[:, :, off:off + width]
        off += width
    zc = lambda n: jnp.zeros((DEPTH, D_MODEL, n), F32)
    w_proj = jnp.concatenate([
        seg['xbc'], seg['gates'], seg['gmla'], seg['z'], seg['qmem'], seg['qa'], seg['kva'],
        seg['dt'], zc(32), seg['kr'], zc(32),
        zc(64), _rot_cols(seg['kr']), zc(32),
    ], axis=2).astype(BF16)

    wqb = p['w_q_b'].astype(BF16).reshape(DEPTH, Q_LORA, N_MLA_HEADS, D_NOPE + D_ROPE)
    wq_n = wqb[..., :D_NOPE]
    wq_r = wqb[..., D_NOPE:]
    zq = jnp.zeros((DEPTH, Q_LORA, N_MLA_HEADS, HEAD_PAD - D_NOPE - D_ROPE), BF16)
    flat = lambda a: a.reshape(DEPTH, a.shape[1], -1)
    wq1 = flat(jnp.concatenate([wq_n, wq_r, zq], axis=-1))
    wq2 = flat(jnp.concatenate([jnp.zeros_like(wq_n), _rot_cols(wq_r), zq], axis=-1))
    wuk = p['w_uk'].astype(BF16)
    wuv = p['w_uv'].astype(BF16)
    zk = jnp.zeros((DEPTH, KV_LORA, N_MLA_HEADS, HEAD_PAD - D_NOPE), BF16)
    wk = flat(jnp.concatenate([wuk, zk], axis=-1))
    wv = flat(jnp.concatenate([wuv, zk], axis=-1))

    pad128 = lambda v: jnp.pad(v, ((0, 0), (0, LANES - v.shape[1]))).reshape(DEPTH, 1, LANES)
    row = lambda v: v.reshape(DEPTH, 1, v.shape[-1])
    return dict(
        w_proj=w_proj,
        cw=p['conv_w'], cb=row(p['conv_b']),
        dtb=pad128(p['dt_bias']), alog=pad128(p['a_log']),
        dskip=row(jnp.repeat(p['d_skip'], SSM_HEAD_DIM, axis=1)),
        nw=row(p['ssm_norm_w']),
        qnw=row(p['q_a_norm_w']), kvnw=row(p['kv_norm_w']),
        wq1=wq1, wq2=wq2, wk=wk, wv=wv,
        wqn=flat(wq_n), wqr=flat(wq_r), wqr2=flat(_rot_cols(wq_r)),
        wukt=jnp.transpose(wuk, (0, 2, 3, 1)),
        wuv=jnp.transpose(wuv, (0, 2, 1, 3)),
        wmk=p['w_mem_k'].astype(BF16), wmv=p['w_mem_v'].astype(BF16),
        wos=p['w_o_ssm'].astype(BF16), woa=p['w_o_mla'].astype(BF16),
        wom=p['w_o_mem'].astype(BF16), wout=p['w_out'].astype(BF16),
        lng=row(p['ln_g']), lnb=row(p['ln_b']),
    )


def _rope_tables(pos):
    half = D_ROPE // 2
    inv = ROPE_THETA ** (-jnp.arange(half, dtype=F32) / half)
    ang = pos.astype(F32)[:, None] * inv[None, :]
    cos = jnp.cos(ang)
    sin = jnp.sin(ang)
    return jnp.concatenate([cos, cos], axis=-1), jnp.concatenate([sin, sin], axis=-1)


def _expansion_matrix():
    e = np.zeros((LANES, D_SSM), np.float32)
    for h in range(N_SSM_HEADS):
        e[h, h * SSM_HEAD_DIM:(h + 1) * SSM_HEAD_DIM] = 1.0
    return jnp.asarray(e, dtype=BF16)


def kernel(x_prompt, x_sample, mem_prompt, state_ssm, state_conv, cache_ckv, cache_krope, cache_mem_k, cache_mem_v,
           ln_in_g, ln_in_b, w_in, conv_w, conv_b, dt_bias, a_log, d_skip, ssm_norm_w, w_o_ssm, q_a_norm_w, w_q_b,
           kv_norm_w, w_uk, w_uv, w_o_mla, w_mem_k, w_mem_v, w_o_mem, w_out, ln_g, ln_b):
    params = dict(w_in=w_in, conv_w=conv_w, conv_b=conv_b, dt_bias=dt_bias, a_log=a_log, d_skip=d_skip,
                  ssm_norm_w=ssm_norm_w, w_o_ssm=w_o_ssm, q_a_norm_w=q_a_norm_w, w_q_b=w_q_b, kv_norm_w=kv_norm_w,
                  w_uk=w_uk, w_uv=w_uv, w_o_mla=w_o_mla, w_mem_k=w_mem_k, w_mem_v=w_mem_v, w_o_mem=w_o_mem,
                  w_out=w_out, ln_g=ln_g, ln_b=ln_b)
    bp, lp, _ = x_prompt.shape
    bs, ls, _ = x_sample.shape
    past = cache_ckv.shape[2]
    tp = bp * lp
    ts = bs * ls
    assert w_in.shape == (DEPTH, D_MODEL, N_PROJ - 192), w_in.shape
    assert (past // CHUNK + 1) * CHUNK >= past + ls, "decode block would need a causal mask"
    assert ls >= CONV_W - 1 and lp >= CONV_W - 1

    w = _pack_weights(params)
    emat = _expansion_matrix()
    tm_p = min(2048, tp)
    xp32, xp16 = _ln_call(x_prompt.reshape(tp, D_MODEL), ln_in_g, ln_in_b, tm_p)
    xs32, xs16 = _ln_call(x_sample.reshape(ts, D_MODEL), ln_in_g, ln_in_b, ts)
    mem16 = mem_prompt.reshape(bp * N_MEM, D_MODEL).astype(BF16)

    cos_p, sin_p = _rope_tables(jnp.arange(lp, dtype=jnp.int32))
    cos_s, sin_s = _rope_tables(past + jnp.arange(ls, dtype=jnp.int32))
    z64 = lambda c: jnp.zeros((c.shape[0], 64), F32)
    z32 = lambda c: jnp.zeros((c.shape[0], 32), F32)
    one64 = jnp.ones((lp, 64), F32)
    cq_p = jnp.concatenate([one64, cos_p, z32(cos_p)], axis=1) * (MLA_SCALE * LOG2E)
    sq_p = jnp.concatenate([z64(sin_p), sin_p, z32(sin_p)], axis=1) * (MLA_SCALE * LOG2E)
    ck_p = jnp.concatenate([z64(cos_p), cos_p, z32(cos_p)], axis=1)
    sk_p = jnp.concatenate([z64(sin_p), sin_p, z32(sin_p)], axis=1)
    cq_s = jnp.tile(jnp.tile(cos_s, (1, N_MLA_HEADS)), (bs, 1)) * (MLA_SCALE * LOG2E)
    sq_s = jnp.tile(jnp.tile(sin_s, (1, N_MLA_HEADS)), (bs, 1)) * (MLA_SCALE * LOG2E)
    ck_s = jnp.tile(jnp.concatenate([z64(cos_s), cos_s, z32(cos_s)], axis=1), (bs, 1))
    sk_s = jnp.tile(jnp.concatenate([z64(sin_s), sin_s, z32(sin_s)], axis=1), (bs, 1))

    tail_rows = 8 - (CONV_W - 1)
    rb_p = min(512, lp)
    tc_p = min(128, lp)
    tq = min(512, lp)
    tmm = min(2048, tp)
    tk_s = min(1024, past)

    outs = {k: [] for k in ('p_conv', 's_conv', 's_ckv', 's_kr')}
    state_dims = (N_SSM_HEADS, SSM_HEAD_DIM, D_STATE)
    p_ssm = jnp.zeros((DEPTH, bp) + state_dims, F32)
    s_ssm = jnp.zeros((DEPTH, bs) + state_dims, F32)
    p_ckv = jnp.zeros((DEPTH, tp, KV_LORA), F32)
    p_kr = jnp.zeros((DEPTH, tp, LANES), F32)
    p_mk = jnp.zeros((DEPTH, bp * N_MEM, D_MEM), F32)
    p_mv = jnp.zeros((DEPTH, bp * N_MEM, D_MEM), F32)
    for i in range(DEPTH):
        xc, pa, pb = _inproj_call(xp16, w, i, tm=tmm, tn=1024, wide_dtype=BF16, name="in_proj")
        y, p_ssm, tail = _ssm_call(xc, pa, pb, None, w, emat, i, batch=bp, seq=lp, rb=rb_p, tc=tc_p, prev_out=p_ssm)
        q, k, v, p_ckv, p_kr = _mla_prep_call(pb, w, cq_p, sq_p, ck_p, sk_p, i, seq=lp, tm=tq,
                                              prev=(p_ckv, p_kr))
        o = _attn_call(q, k, v, batch=bp, seq=lp, tq=tq)
        p_mk = _mm_call(mem16, w['wmk'], i, bp * N_MEM, 1024, "mem_k_proj", prev=p_mk)
        p_mv = _mm_call(mem16, w['wmv'], i, bp * N_MEM, 1024, "mem_v_proj", prev=p_mv)
        om = _mem_attn_call(pa, p_mk.reshape(DEPTH, bp, N_MEM, D_MEM), p_mv.reshape(DEPTH, bp, N_MEM, D_MEM), i,
                            batch=bp, seq=lp, tm=min(1024, lp))
        xp32, xp16 = _merge_call(y, o, pa, om, xp32, w, i, tm=min(512, tp))
        outs['p_conv'].append(tail[:, tail_rows:, :])

        xc_s, pa_s, pb_s = _inproj_call(xs16, w, i, tm=ts, tn=1024, wide_dtype=F32, name="in_proj_dec")
        tail0 = jnp.pad(state_conv[i], ((0, 0), (tail_rows, 0), (0, 0)))
        y_s, s_ssm, tail_s = _ssm_call(xc_s, pa_s, pb_s, (tail0, state_ssm), w, emat, i,
                                       batch=bs, seq=ls, rb=ls, tc=ls, prev_out=s_ssm)
        qlat, qr, ckv_s, kr_s = _mla_prep_s_call(pb_s, w, cq_s, sq_s, ck_s, sk_s, i)
        kr_s = kr_s[:, 64:64 + D_ROPE]
        to_hq = lambda a, d: jnp.transpose(a.reshape(bs, ls, N_MLA_HEADS, d), (0, 2, 1, 3)).reshape(bs, N_MLA_HEADS * ls, d)
        o_s = _attn_s_call(to_hq(qlat, KV_LORA), to_hq(qr, D_ROPE), cache_ckv, cache_krope,
                           ckv_s.reshape(bs, ls, KV_LORA), kr_s.reshape(bs, ls, D_ROPE), w['wuv'], i, tk=tk_s)
        o_s = jnp.transpose(o_s.reshape(bs, N_MLA_HEADS, ls, D_V), (0, 2, 1, 3)).reshape(ts, D_MLA)
        om_s = _mem_attn_call(pa_s, cache_mem_k, cache_mem_v, i, batch=bs, seq=ls, tm=ls)
        xs32, xs16 = _merge_call(y_s, o_s, pa_s, om_s, xs32, w, i, tm=min(256, ts))
        outs['s_conv'].append(tail_s[:, tail_rows:, :])
        outs['s_ckv'].append(ckv_s.reshape(bs, ls, KV_LORA))
        outs['s_kr'].append(kr_s.reshape(bs, ls, D_ROPE))

    st = lambda k: jnp.stack(outs[k])
    mem5 = (DEPTH, bp, N_MEM, MEM_HEADS, MEM_HEAD_DIM)
    return (xp32.reshape(bp, lp, D_MODEL), xs32.reshape(bs, ls, D_MODEL), p_ssm, st('p_conv'),
            p_ckv.reshape(DEPTH, bp, lp, KV_LORA), p_kr[:, :, 64:64 + D_ROPE].reshape(DEPTH, bp, lp, D_ROPE),
            p_mk.reshape(mem5), p_mv.reshape(mem5), s_ssm, st('s_conv'), st('s_ckv'), st('s_kr'))
```
